```python
import jax, jax.numpy as jnp
from jax import lax
import numpy as np

D_MODEL = 2048
BATCH = 4
SEQ = 2048
DEPTH = 1

HEAD_DIM = 128
GRID_W = 64
NORM_EPS = 1e-6
Q_BLOCK = 128

A_Q_HEADS = 8
A_KV_HEADS = 2
A_WIDTH = A_Q_HEADS * HEAD_DIM
A_KV_WIDTH = A_KV_HEADS * HEAD_DIM
AXIAL_THETA = 10000.0

B_PATTERNS = ((128, 1), (512, 4), (2048, 16))
B_GROUPS = len(B_PATTERNS)
B_HEADS_PER_GROUP = 4
B_WIDTH = B_HEADS_PER_GROUP * HEAD_DIM
PARTIAL_ROPE_DIM = HEAD_DIM // 4
ROPE_THETA = 500000.0

IN_SIZES = (A_WIDTH, A_KV_WIDTH, A_KV_WIDTH, A_WIDTH,
            B_GROUPS * B_WIDTH, B_GROUPS * B_WIDTH, B_GROUPS * B_WIDTH, B_WIDTH,
            D_MODEL, D_MODEL)
IN_COLS = sum(IN_SIZES)

kernel_name = "hybrid_gated_grid_gqa_dilated_attention_encoder"


def rms_norm(x, g):
    xf = x.astype(jnp.float32)
    y = xf * lax.rsqrt(jnp.mean(xf * xf, axis=-1, keepdims=True) + NORM_EPS)
    return (y * g.astype(jnp.float32)).astype(x.dtype)


def rope_angles(pos, dim, theta):
    expo = jnp.arange(0, dim, 2, dtype=jnp.float32) / dim
    inv_freq = 1.0 / jnp.power(jnp.asarray(theta, jnp.float32), expo)
    ang = pos.astype(jnp.float32)[:, None] * inv_freq[None, :]
    return jnp.cos(ang), jnp.sin(ang)


def apply_rotary(x, cos, sin):
    x1, x2 = jnp.split(x, 2, axis=-1)
    c = cos[None, :, None, :].astype(x.dtype)
    s = sin[None, :, None, :].astype(x.dtype)
    return jnp.concatenate([x1 * c - x2 * s, x1 * s + x2 * c], axis=-1)


def axial_rotary(x, row_id, col_id):
    half = HEAD_DIM // 2
    cr, sr = rope_angles(row_id, half, AXIAL_THETA)
    cc, sc = rope_angles(col_id, half, AXIAL_THETA)
    return jnp.concatenate([apply_rotary(x[..., :half], cr, sr),
                            apply_rotary(x[..., half:], cc, sc)], axis=-1)


def partial_rotary(x, pos):
    c, s = rope_angles(pos, PARTIAL_ROPE_DIM, ROPE_THETA)
    return jnp.concatenate([apply_rotary(x[..., :PARTIAL_ROPE_DIM], c, s),
                            x[..., PARTIAL_ROPE_DIM:]], axis=-1)


def grid_gqa_attention(q, k, v):
    B, S, Hq, D = q.shape
    G = Hq // A_KV_HEADS
    nb = S // Q_BLOCK
    scale = D ** -0.5
    qb = q.reshape(B, nb, Q_BLOCK, A_KV_HEADS, G, D).transpose(1, 0, 2, 3, 4, 5)

    def block(q_blk):
        s = jnp.einsum('bqhgd,bkhd->bhgqk', q_blk, k).astype(jnp.float32) * scale
        p = jax.nn.softmax(s, axis=-1).astype(v.dtype)
        return jnp.einsum('bhgqk,bkhd->bqhgd', p, v)

    o = lax.map(block, qb)
    return o.transpose(1, 0, 2, 3, 4, 5).reshape(B, S, Hq * D)


def dilated_band_attention(q, k, v, dilation, half):
    B, S, H, D = q.shape
    L = S // dilation
    C = half
    nblk = -(-L // C)
    Lp = nblk * C
    scale = D ** -0.5

    def by_stride(t):
        return t.reshape(B, L, dilation, H, D).transpose(0, 2, 1, 3, 4)

    qr, kr, vr = by_stride(q), by_stride(k), by_stride(v)
    qr = jnp.pad(qr, ((0, 0), (0, 0), (0, Lp - L), (0, 0), (0, 0)))
    kv_pad = ((0, 0), (0, 0), (C, Lp - L + C), (0, 0), (0, 0))
    kp = jnp.pad(kr, kv_pad).reshape(B, dilation, nblk + 2, C, H, D)
    vp = jnp.pad(vr, kv_pad).reshape(B, dilation, nblk + 2, C, H, D)

    def band(t):
        return jnp.concatenate([t[:, :, :-2], t[:, :, 1:-1], t[:, :, 2:]], axis=3)

    kband, vband = band(kp), band(vp)
    qb = qr.reshape(B, dilation, nblk, C, H, D)

    qi = jnp.arange(C)[:, None]
    kj = jnp.arange(3 * C)[None, :]
    rel = kj - C - qi
    kpos = jnp.arange(nblk)[:, None, None] * C + kj[None] - C
    valid = (jnp.abs(rel) <= half)[None] & (kpos >= 0) & (kpos < L)

    s = jnp.einsum('brnqhe,brnkhe->brnhqk', qb, kband).astype(jnp.float32) * scale
    s = jnp.where(valid[None, None, :, None], s, -1e30)
    m = jnp.max(s, axis=-1, keepdims=True)
    p = jnp.exp(s - m)
    l = jnp.sum(p, axis=-1)
    o = jnp.einsum('brnhqk,brnkhe->brnqhe', p.astype(v.dtype), vband)
    o = o / l.transpose(0, 1, 2, 4, 3)[..., None].astype(o.dtype)
    lse = (m[..., 0] + jnp.log(l)).transpose(0, 1, 2, 4, 3)

    o = o.reshape(B, dilation, Lp, H, D)[:, :, :L].transpose(0, 2, 1, 3, 4).reshape(B, S, H, D)
    lse = lse.reshape(B, dilation, Lp, H)[:, :, :L].transpose(0, 2, 1, 3).reshape(B, S, H)
    return o, lse


def setup_inputs(seed: int = 0) -> dict:
    key = jax.random.key(seed)
    ks = jax.random.split(key, 10)
    f32 = jnp.float32
    x = jax.random.normal(ks[0], (BATCH, SEQ, D_MODEL), f32)
    norm_gain = 1.0 + 0.02 * jax.random.normal(ks[1], (DEPTH, D_MODEL), f32)
    w_in = jax.random.normal(ks[2], (DEPTH, D_MODEL, IN_COLS), f32) * D_MODEL ** -0.5
    q_norm_gain = 1.0 + 0.02 * jax.random.normal(ks[3], (DEPTH, HEAD_DIM), f32)
    k_norm_gain = 1.0 + 0.02 * jax.random.normal(ks[4], (DEPTH, HEAD_DIM), f32)
    merge_gate_bias = 0.01 * jax.random.normal(ks[5], (DEPTH, 2, D_MODEL), f32)
    w_branch_a = jax.random.normal(ks[6], (DEPTH, A_WIDTH, D_MODEL), f32) * A_WIDTH ** -0.5
    w_branch_b = jax.random.normal(ks[7], (DEPTH, B_WIDTH, D_MODEL), f32) * B_WIDTH ** -0.5
    w_out = jax.random.normal(ks[8], (DEPTH, D_MODEL, D_MODEL), f32) * D_MODEL ** -0.5
    final_norm_gain = 1.0 + 0.02 * jax.random.normal(ks[9], (D_MODEL,), f32)
    return {"x": x, "norm_gain": norm_gain, "w_in": w_in, "q_norm_gain": q_norm_gain,
            "k_norm_gain": k_norm_gain, "merge_gate_bias": merge_gate_bias,
            "w_branch_a": w_branch_a, "w_branch_b": w_branch_b, "w_out": w_out,
            "final_norm_gain": final_norm_gain}


def reference(x, norm_gain, w_in, q_norm_gain, k_norm_gain, merge_gate_bias,
              w_branch_a, w_branch_b, w_out, final_norm_gain):
    B, S, _ = x.shape
    rows = S // GRID_W
    row_grid, col_grid = jnp.meshgrid(jnp.arange(rows, dtype=jnp.int32),
                                      jnp.arange(GRID_W, dtype=jnp.int32), indexing='ij')
    row_id, col_id = row_grid.reshape(-1), col_grid.reshape(-1)
    pos = jnp.arange(S, dtype=jnp.int32)
    split_at = [int(c) for c in np.cumsum(IN_SIZES)[:-1]]

    for l in range(DEPTH):
        h = rms_norm(x, norm_gain[l])
        proj = jnp.einsum('bsd,dc->bsc', h, w_in[l])
        qa, ka, va, ga, qb, kb, vb, gb, za, zb = jnp.split(proj, split_at, axis=-1)

        qa = axial_rotary(rms_norm(qa.reshape(B, S, A_Q_HEADS, HEAD_DIM), q_norm_gain[l]), row_id, col_id)
        ka = axial_rotary(rms_norm(ka.reshape(B, S, A_KV_HEADS, HEAD_DIM), k_norm_gain[l]), row_id, col_id)
        va = va.reshape(B, S, A_KV_HEADS, HEAD_DIM)
        ya = grid_gqa_attention(qa, ka, va) * jax.nn.silu(ga)
        pa = jnp.einsum('bsc,cd->bsd', ya, w_branch_a[l])

        nh = B_GROUPS * B_HEADS_PER_GROUP
        qb = partial_rotary(qb.reshape(B, S, nh, HEAD_DIM), pos).reshape(B, S, B_GROUPS, B_HEADS_PER_GROUP, HEAD_DIM)
        kb = partial_rotary(kb.reshape(B, S, nh, HEAD_DIM), pos).reshape(B, S, B_GROUPS, B_HEADS_PER_GROUP, HEAD_DIM)
        vb = vb.reshape(B, S, B_GROUPS, B_HEADS_PER_GROUP, HEAD_DIM)
        outs, lses = [], []
        for g, (window, dilation) in enumerate(B_PATTERNS):
            o_g, lse_g = dilated_band_attention(qb[:, :, g], kb[:, :, g], vb[:, :, g],
                                                dilation, window // (2 * dilation))
            outs.append(o_g)
            lses.append(lse_g)
        wts = jax.nn.softmax(jnp.stack(lses, axis=0), axis=0)
        ob = jnp.sum(wts[..., None].astype(x.dtype) * jnp.stack(outs, axis=0), axis=0)
        yb = ob.reshape(B, S, B_WIDTH) * jax.nn.silu(gb)
        pb = jnp.einsum('bsc,cd->bsd', yb, w_branch_b[l])

        merged = (jax.nn.sigmoid(za + merge_gate_bias[l, 0]) * pa
                  + jax.nn.sigmoid(zb + merge_gate_bias[l, 1]) * pb)
        x = x + jnp.einsum('bsd,de->bse', merged, w_out[l])

    return rms_norm(x, final_norm_gain)
```

```python
import functools

import jax
import jax.numpy as jnp
from jax import lax
from jax.experimental import pallas as pl
from jax.experimental.pallas import tpu as pltpu

F32 = jnp.float32
BF16 = jnp.bfloat16

HEAD_DIM = 128
GRID_W = 64
NORM_EPS = 1e-6
A_Q_HEADS = 8
A_KV_HEADS = 2
A_GROUP = A_Q_HEADS // A_KV_HEADS
A_WIDTH = A_Q_HEADS * HEAD_DIM
A_KV_WIDTH = A_KV_HEADS * HEAD_DIM
AXIAL_THETA = 10000.0
B_PATTERNS = ((128, 1), (512, 4), (2048, 16))
B_GROUPS = len(B_PATTERNS)
B_HEADS = 4
B_WIDTH = B_HEADS * HEAD_DIM
PARTIAL_ROPE_DIM = HEAD_DIM // 4
ROPE_THETA = 500000.0
BAND_HALF = 64
BAND_TQ = 128
MASK_VALUE = -1e30

COL_BLK = 512
_OFF_QA = 0
_OFF_KA = A_WIDTH
_OFF_VA = _OFF_KA + A_KV_WIDTH
_OFF_GA = _OFF_VA + A_KV_WIDTH
_OFF_QB = _OFF_GA + A_WIDTH
_OFF_KB = _OFF_QB + B_GROUPS * B_WIDTH
_OFF_VB = _OFF_KB + B_GROUPS * B_WIDTH
_OFF_GB = _OFF_VB + B_GROUPS * B_WIDTH
_OFF_ZA = _OFF_GB + B_WIDTH
IN_COLS_TOTAL = _OFF_ZA + 2 * 2048

V7X_VMEM_LIMIT_BYTES = 56 * 1024 * 1024


def _rotary(x, cos, sin_lo, sin_hi, shift):
    up = pltpu.roll(x, HEAD_DIM - shift, 1)
    dn = pltpu.roll(x, shift, 1)
    return x * cos + up * sin_lo + dn * sin_hi


def _rms(x, gain):
    ms = jnp.mean(x * x, axis=-1, keepdims=True)
    return x * lax.rsqrt(ms + NORM_EPS) * gain


def _silu(g):
    return g * jax.nn.sigmoid(g)


def _proj_kernel(x_ref, g_ref, w_ref, o_ref, h_ref):
    @pl.when(pl.program_id(1) == 0)
    def _():
        h_ref[...] = _rms(x_ref[...], g_ref[...]).astype(BF16)

    o_ref[...] = jnp.dot(h_ref[...], w_ref[...], preferred_element_type=F32).astype(BF16)


def _input_projection(x2, gain, w_bf16, tm=1024, tn=512):
    T, D = x2.shape
    N = w_bf16.shape[1]
    return pl.pallas_call(
        _proj_kernel,
        grid=(T // tm, N // tn),
        in_specs=[
            pl.BlockSpec((tm, D), lambda i, j: (i, 0)),
            pl.BlockSpec((1, D), lambda i, j: (0, 0)),
            pl.BlockSpec((D, tn), lambda i, j: (0, j)),
        ],
        out_specs=pl.BlockSpec((tm, tn), lambda i, j: (i, j)),
        out_shape=jax.ShapeDtypeStruct((T, N), BF16),
        scratch_shapes=[pltpu.VMEM((tm, D), BF16)],
        compiler_params=pltpu.CompilerParams(
            dimension_semantics=("parallel", "arbitrary"),
            vmem_limit_bytes=V7X_VMEM_LIMIT_BYTES),
        name="input_projection",
    )(x2, gain, w_bf16)


def _attn_a_kernel(q_ref, k_ref, v_ref, ga_ref, gq_ref, gk_ref, cos_ref, slo_ref, shi_ref,
                   o_ref, kn_ref, *, tq):
    qi = pl.program_id(2)
    shift = HEAD_DIM // 4

    @pl.when(qi == 0)
    def _():
        k = _rms(k_ref[0].astype(F32), gk_ref[...])
        kn_ref[...] = _rotary(k, cos_ref[...], slo_ref[...], shi_ref[...], shift).astype(BF16)

    rows = pl.ds(pl.multiple_of(qi * tq, tq), tq)
    cos, slo, shi = cos_ref[rows, :], slo_ref[rows, :], shi_ref[rows, :]
    kn = kn_ref[...]
    v = v_ref[0]
    scale = HEAD_DIM ** -0.5
    for h in range(A_GROUP):
        cols = slice(h * HEAD_DIM, (h + 1) * HEAD_DIM)
        q = _rms(q_ref[0, :, cols].astype(F32), gq_ref[...])
        q = (_rotary(q, cos, slo, shi, shift) * scale).astype(BF16)
        s = lax.dot_general(q, kn, (((1,), (1,)), ((), ())), preferred_element_type=F32)
        m = jnp.max(s, axis=-1, keepdims=True)
        p = jnp.exp(s - m)
        l = jnp.sum(p, axis=-1, keepdims=True)
        o = jnp.dot(p.astype(BF16), v, preferred_element_type=F32)
        g = ga_ref[0, :, cols].astype(F32)
        o_ref[0, :, cols] = (o * (1.0 / l) * _silu(g)).astype(BF16)


def _attention_a(proj3, q_gain, k_gain, tables, tq=256):
    B, S, _ = proj3.shape
    cos, slo, shi = tables
    gw = A_GROUP * HEAD_DIM
    ka_blk = _OFF_KA // HEAD_DIM
    va_blk = _OFF_VA // HEAD_DIM
    ga_blk = _OFF_GA // gw
    full = pl.BlockSpec((S, HEAD_DIM), lambda b, g, i: (0, 0))
    vec = pl.BlockSpec((1, HEAD_DIM), lambda b, g, i: (0, 0))
    return pl.pallas_call(
        functools.partial(_attn_a_kernel, tq=tq),
        grid=(B, A_KV_HEADS, S // tq),
        in_specs=[
            pl.BlockSpec((1, tq, gw), lambda b, g, i: (b, i, g)),
            pl.BlockSpec((1, S, HEAD_DIM), lambda b, g, i: (b, 0, ka_blk + g)),
            pl.BlockSpec((1, S, HEAD_DIM), lambda b, g, i: (b, 0, va_blk + g)),
            pl.BlockSpec((1, tq, gw), lambda b, g, i: (b, i, ga_blk + g)),
            vec, vec, full, full, full,
        ],
        out_specs=pl.BlockSpec((1, tq, gw), lambda b, g, i: (b, i, g)),
        out_shape=jax.ShapeDtypeStruct((B, S, A_WIDTH), BF16),
        scratch_shapes=[pltpu.VMEM((S, HEAD_DIM), BF16)],
        compiler_params=pltpu.CompilerParams(
            dimension_semantics=("parallel", "parallel", "arbitrary"),
            vmem_limit_bytes=V7X_VMEM_LIMIT_BYTES),
        name="attention_a",
    )(proj3, proj3, proj3, proj3, q_gain, k_gain, cos, slo, shi)


def _band_kernel(q_ref, k_ref, v_ref, cos_ref, slo_ref, shi_ref, o_ref, lse_ref,
                 kp_ref, vp_ref, *, L):
    shift = PARTIAL_ROPE_DIM // 2
    win = BAND_TQ + 2 * BAND_HALF
    cos, slo, shi = cos_ref[...], slo_ref[...], shi_ref[...]
    scale = HEAD_DIM ** -0.5

    zeros = jnp.zeros((BAND_HALF, B_WIDTH), BF16)
    kp_ref[0:BAND_HALF, :] = zeros
    vp_ref[0:BAND_HALF, :] = zeros
    kp_ref[BAND_HALF + L:, :] = zeros
    vp_ref[BAND_HALF + L:, :] = zeros
    vp_ref[BAND_HALF:BAND_HALF + L, :] = v_ref[0]
    for h in range(B_HEADS):
        cols = slice(h * HEAD_DIM, (h + 1) * HEAD_DIM)
        k = _rotary(k_ref[0, :, cols].astype(F32), cos, slo, shi, shift)
        kp_ref[BAND_HALF:BAND_HALF + L, cols] = k.astype(BF16)

    qi = lax.broadcasted_iota(jnp.int32, (BAND_TQ, win), 0)
    kj = lax.broadcasted_iota(jnp.int32, (BAND_TQ, win), 1)
    in_band = (kj >= qi) & (kj <= qi + 2 * BAND_HALF)

    def tile(t, carry):
        q0 = pl.multiple_of(t * BAND_TQ, BAND_TQ)
        kpos = kj + (q0 - BAND_HALF)
        valid = in_band & (kpos >= 0) & (kpos < L)
        for h in range(B_HEADS):
            cols = slice(h * HEAD_DIM, (h + 1) * HEAD_DIM)
            rows = pl.ds(q0, BAND_TQ)
            q = _rotary(q_ref[0, rows, cols].astype(F32), cos_ref[rows, :], slo_ref[rows, :],
                        shi_ref[rows, :], shift)
            q = (q * scale).astype(BF16)
            kw = kp_ref[pl.ds(q0, win), cols]
            vw = vp_ref[pl.ds(q0, win), cols]
            s = lax.dot_general(q, kw, (((1,), (1,)), ((), ())), preferred_element_type=F32)
            s = jnp.where(valid, s, MASK_VALUE)
            m = jnp.max(s, axis=-1, keepdims=True)
            p = jnp.exp(s - m)
            l = jnp.sum(p, axis=-1, keepdims=True)
            o = jnp.dot(p.astype(BF16), vw, preferred_element_type=F32)
            o_ref[0, rows, cols] = (o * (1.0 / l)).astype(BF16)
            lse_ref[0, rows, cols] = jnp.broadcast_to(m + jnp.log(l), (BAND_TQ, HEAD_DIM))
        return carry

    lax.fori_loop(0, L // BAND_TQ, tile, 0)


def _band_attention(proj3, group, dilation, tables):
    B, S, NC = proj3.shape
    L = S // dilation
    cb = NC // COL_BLK
    pv = proj3.reshape(B, L, dilation * NC)
    cos, slo, shi = (t.reshape(L, dilation * HEAD_DIM) for t in tables)
    qb, kb, vb = (off // COL_BLK + group for off in (_OFF_QB, _OFF_KB, _OFF_VB))
    tab = pl.BlockSpec((L, HEAD_DIM), lambda b, r: (0, r))
    o, lse = pl.pallas_call(
        functools.partial(_band_kernel, L=L),
        grid=(B, dilation),
        in_specs=[
            pl.BlockSpec((1, L, COL_BLK), lambda b, r: (b, 0, r * cb + qb)),
            pl.BlockSpec((1, L, COL_BLK), lambda b, r: (b, 0, r * cb + kb)),
            pl.BlockSpec((1, L, COL_BLK), lambda b, r: (b, 0, r * cb + vb)),
            tab, tab, tab,
        ],
        out_specs=[
            pl.BlockSpec((1, L, B_WIDTH), lambda b, r: (b, 0, r)),
            pl.BlockSpec((1, L, B_WIDTH), lambda b, r: (b, 0, r)),
        ],
        out_shape=[
            jax.ShapeDtypeStruct((B, L, dilation * B_WIDTH), BF16),
            jax.ShapeDtypeStruct((B, L, dilation * B_WIDTH), F32),
        ],
        scratch_shapes=[
            pltpu.VMEM((L + 2 * BAND_HALF, B_WIDTH), BF16),
            pltpu.VMEM((L + 2 * BAND_HALF, B_WIDTH), BF16),
        ],
        compiler_params=pltpu.CompilerParams(
            dimension_semantics=("parallel", "parallel"),
            vmem_limit_bytes=V7X_VMEM_LIMIT_BYTES),
        name=f"band_attention_d{dilation}",
    )(pv, pv, pv, cos, slo, shi)
    return o.reshape(B * S, B_WIDTH), lse.reshape(B * S, B_WIDTH)


def _merge_kernel(x_ref, ya_ref, o0_ref, o1_ref, o2_ref, l0_ref, l1_ref, l2_ref, gb_ref,
                  za0, za1, za2, za3, zb0, zb1, zb2, zb3, bias_ref,
                  wa_ref, wb_ref, wo_ref, fg_ref, out_ref, merged_ref, *, final_norm):
    l0, l1, l2 = l0_ref[...], l1_ref[...], l2_ref[...]
    lmax = jnp.maximum(jnp.maximum(l0, l1), l2)
    e0, e1, e2 = jnp.exp(l0 - lmax), jnp.exp(l1 - lmax), jnp.exp(l2 - lmax)
    ob = (e0 * o0_ref[...].astype(F32) + e1 * o1_ref[...].astype(F32)
          + e2 * o2_ref[...].astype(F32)) * (1.0 / (e0 + e1 + e2))
    yb = (ob * _silu(gb_ref[...].astype(F32))).astype(BF16)

    pa = jnp.dot(ya_ref[...], wa_ref[...], preferred_element_type=F32)
    pb = jnp.dot(yb, wb_ref[...], preferred_element_type=F32)
    for c, (za, zb) in enumerate(((za0, zb0), (za1, zb1), (za2, zb2), (za3, zb3))):
        cols = slice(c * COL_BLK, (c + 1) * COL_BLK)
        gate_a = jax.nn.sigmoid(za[...].astype(F32) + bias_ref[0:1, cols])
        gate_b = jax.nn.sigmoid(zb[...].astype(F32) + bias_ref[1:2, cols])
        merged_ref[:, cols] = (gate_a * pa[:, cols] + gate_b * pb[:, cols]).astype(BF16)

    y = x_ref[...] + jnp.dot(merged_ref[...], wo_ref[...], preferred_element_type=F32)
    out_ref[...] = _rms(y, fg_ref[...]) if final_norm else y


def _merge(x2, ya, outs, lses, proj2, bias, wa, wb, wo, final_gain, final_norm, tm=256):
    T, D = x2.shape
    gb_blk = _OFF_GB // COL_BLK
    za_blk = _OFF_ZA // COL_BLK
    n_z = D // COL_BLK

    def row_blk(width, col=0):
        return pl.BlockSpec((tm, width), lambda i: (i, col))

    def whole(a):
        return pl.BlockSpec(a.shape, lambda i: (0, 0))

    z_specs = [row_blk(COL_BLK, za_blk + c) for c in range(2 * n_z)]
    return pl.pallas_call(
        functools.partial(_merge_kernel, final_norm=final_norm),
        grid=(T // tm,),
        in_specs=[row_blk(D), row_blk(A_WIDTH)]
        + [row_blk(B_WIDTH)] * (2 * B_GROUPS)
        + [row_blk(COL_BLK, gb_blk)] + z_specs
        + [whole(bias), whole(wa), whole(wb), whole(wo), whole(final_gain)],
        out_specs=row_blk(D),
        out_shape=jax.ShapeDtypeStruct((T, D), F32),
        scratch_shapes=[pltpu.VMEM((tm, D), BF16)],
        compiler_params=pltpu.CompilerParams(
            dimension_semantics=("parallel",),
            vmem_limit_bytes=V7X_VMEM_LIMIT_BYTES),
        name="merge_output",
    )(x2, ya, *outs, *lses, proj2, *([proj2] * (2 * n_z)), bias, wa, wb, wo, final_gain)


def _angles(pos, dim, theta):
    expo = jnp.arange(0, dim, 2, dtype=F32) / dim
    inv_freq = 1.0 / jnp.power(jnp.asarray(theta, F32), expo)
    return pos.astype(F32)[:, None] * inv_freq[None, :]


def _lane_tables(ang_blocks):
    cos, slo, shi = [], [], []
    for blk in ang_blocks:
        if isinstance(blk, int):
            S = cos[0].shape[0]
            cos.append(jnp.ones((S, blk), F32))
            slo.append(jnp.zeros((S, blk), F32))
            shi.append(jnp.zeros((S, blk), F32))
            continue
        c, s, z = jnp.cos(blk), jnp.sin(blk), jnp.zeros_like(blk)
        cos += [c, c]
        slo += [-s, z]
        shi += [z, s]
    return tuple(jnp.concatenate(t, axis=-1) for t in (cos, slo, shi))


def _axial_tables(S):
    pos = jnp.arange(S, dtype=jnp.int32)
    half = HEAD_DIM // 2
    return _lane_tables([_angles(pos // GRID_W, half, AXIAL_THETA),
                         _angles(pos % GRID_W, half, AXIAL_THETA)])


def _partial_tables(S):
    pos = jnp.arange(S, dtype=jnp.int32)
    return _lane_tables([_angles(pos, PARTIAL_ROPE_DIM, ROPE_THETA),
                         HEAD_DIM - PARTIAL_ROPE_DIM])


def kernel(x, norm_gain, w_in, q_norm_gain, k_norm_gain, merge_gate_bias, w_branch_a,
           w_branch_b, w_out, final_norm_gain):
    B, S, D = x.shape
    depth = norm_gain.shape[0]
    assert w_in.shape[2] == IN_COLS_TOTAL and S % GRID_W == 0
    assert all(w // (2 * d) == BAND_HALF for w, d in B_PATTERNS)
    axial = _axial_tables(S)
    partial = _partial_tables(S)
    x2 = x.reshape(B * S, D)
    for l in range(depth):
        proj2 = _input_projection(x2, norm_gain[l][None, :], w_in[l].astype(BF16))
        proj3 = proj2.reshape(B, S, IN_COLS_TOTAL)
        ya = _attention_a(proj3, q_norm_gain[l][None, :], k_norm_gain[l][None, :], axial)
        outs, lses = [], []
        for g, (_, dilation) in enumerate(B_PATTERNS):
            o_g, lse_g = _band_attention(proj3, g, dilation, partial)
            outs.append(o_g)
            lses.append(lse_g)
        x2 = _merge(x2, ya.reshape(B * S, A_WIDTH), outs, lses, proj2, merge_gate_bias[l],
                    w_branch_a[l].astype(BF16), w_branch_b[l].astype(BF16),
                    w_out[l].astype(BF16), final_norm_gain[None, :],
                    final_norm=(l == depth - 1))
    return x2.reshape(B, S, D)
```

```python
import functools

import jax
import jax.numpy as jnp
from jax import lax
from jax.experimental import pallas as pl
from jax.experimental.pallas import tpu as pltpu

F32 = jnp.float32
BF16 = jnp.bfloat16

HEAD_DIM = 128
GRID_W = 64
NORM_EPS = 1e-6
A_Q_HEADS = 8
A_KV_HEADS = 2
A_GROUP = A_Q_HEADS // A_KV_HEADS
A_WIDTH = A_Q_HEADS * HEAD_DIM
A_KV_WIDTH = A_KV_HEADS * HEAD_DIM
AXIAL_THETA = 10000.0
B_PATTERNS = ((128, 1), (512, 4), (2048, 16))
B_GROUPS = len(B_PATTERNS)
B_HEADS = 4
B_WIDTH = B_HEADS * HEAD_DIM
PARTIAL_ROPE_DIM = HEAD_DIM // 4
ROPE_THETA = 500000.0
BAND_HALF = 64
BAND_TQ = 128
BAND_TILES_PER_STEP = 4
MASK_VALUE = -1e30

COL_BLK = 512
_OFF_QA = 0
_OFF_KA = A_WIDTH
_OFF_VA = _OFF_KA + A_KV_WIDTH
_OFF_GA = _OFF_VA + A_KV_WIDTH
_OFF_QB = _OFF_GA + A_WIDTH
_OFF_KB = _OFF_QB + B_GROUPS * B_WIDTH
_OFF_VB = _OFF_KB + B_GROUPS * B_WIDTH
_OFF_GB = _OFF_VB + B_GROUPS * B_WIDTH
_OFF_ZA = _OFF_GB + B_WIDTH
IN_COLS_TOTAL = _OFF_ZA + 2 * 2048

V7X_VMEM_LIMIT_BYTES = 56 * 1024 * 1024


def _rotary(x, cos, sin_lo, sin_hi, shift):
    up = pltpu.roll(x, HEAD_DIM - shift, 1)
    dn = pltpu.roll(x, shift, 1)
    return x * cos + up * sin_lo + dn * sin_hi


def _rms(x, gain):
    ms = jnp.mean(x * x, axis=-1, keepdims=True)
    return x * lax.rsqrt(ms + NORM_EPS) * gain


def _silu(g):
    return g * jax.nn.sigmoid(g)


def _proj_kernel(x_ref, g_ref, w_ref, o_ref, h_ref):
    @pl.when(pl.program_id(1) == 0)
    def _():
        h_ref[...] = _rms(x_ref[...], g_ref[...]).astype(BF16)

    o_ref[...] = jnp.dot(h_ref[...], w_ref[...], preferred_element_type=F32).astype(BF16)


def _input_projection(x2, gain, w_bf16, tm=1024, tn=512):
    T, D = x2.shape
    N = w_bf16.shape[1]
    return pl.pallas_call(
        _proj_kernel,
        grid=(T // tm, N // tn),
        in_specs=[
            pl.BlockSpec((tm, D), lambda i, j: (i, 0)),
            pl.BlockSpec((1, D), lambda i, j: (0, 0)),
            pl.BlockSpec((D, tn), lambda i, j: (0, j)),
        ],
        out_specs=pl.BlockSpec((tm, tn), lambda i, j: (i, j)),
        out_shape=jax.ShapeDtypeStruct((T, N), BF16),
        scratch_shapes=[pltpu.VMEM((tm, D), BF16)],
        compiler_params=pltpu.CompilerParams(
            dimension_semantics=("parallel", "arbitrary"),
            vmem_limit_bytes=V7X_VMEM_LIMIT_BYTES),
        name="input_projection",
    )(x2, gain, w_bf16)


def _attn_a_kernel(q_ref, k_ref, v_ref, ga_ref, gq_ref, gk_ref, cos_ref, slo_ref, shi_ref,
                   o_ref, kn_ref, *, tq):
    qi = pl.program_id(2)
    shift = HEAD_DIM // 4

    @pl.when(qi == 0)
    def _():
        k = _rms(k_ref[0].astype(F32), gk_ref[...])
        kn_ref[...] = _rotary(k, cos_ref[...], slo_ref[...], shi_ref[...], shift).astype(BF16)

    rows = pl.ds(pl.multiple_of(qi * tq, tq), tq)
    cos, slo, shi = cos_ref[rows, :], slo_ref[rows, :], shi_ref[rows, :]
    kn = kn_ref[...]
    v = v_ref[0]
    scale = HEAD_DIM ** -0.5
    for h in range(A_GROUP):
        cols = slice(h * HEAD_DIM, (h + 1) * HEAD_DIM)
        q = _rms(q_ref[0, :, cols].astype(F32), gq_ref[...])
        q = (_rotary(q, cos, slo, shi, shift) * scale).astype(BF16)
        s = lax.dot_general(q, kn, (((1,), (1,)), ((), ())), preferred_element_type=F32)
        m = jnp.max(s, axis=-1, keepdims=True)
        p = jnp.exp(s - m)
        l = jnp.sum(p, axis=-1, keepdims=True)
        o = jnp.dot(p.astype(BF16), v, preferred_element_type=F32)
        g = ga_ref[0, :, cols].astype(F32)
        o_ref[0, :, cols] = (o * (1.0 / l) * _silu(g)).astype(BF16)


def _attention_a(proj3, q_gain, k_gain, tables, tq=256):
    B, S, _ = proj3.shape
    cos, slo, shi = tables
    gw = A_GROUP * HEAD_DIM
    ka_blk = _OFF_KA // HEAD_DIM
    va_blk = _OFF_VA // HEAD_DIM
    ga_blk = _OFF_GA // gw
    full = pl.BlockSpec((S, HEAD_DIM), lambda b, g, i: (0, 0))
    vec = pl.BlockSpec((1, HEAD_DIM), lambda b, g, i: (0, 0))
    return pl.pallas_call(
        functools.partial(_attn_a_kernel, tq=tq),
        grid=(B, A_KV_HEADS, S // tq),
        in_specs=[
            pl.BlockSpec((1, tq, gw), lambda b, g, i: (b, i, g)),
            pl.BlockSpec((1, S, HEAD_DIM), lambda b, g, i: (b, 0, ka_blk + g)),
            pl.BlockSpec((1, S, HEAD_DIM), lambda b, g, i: (b, 0, va_blk + g)),
            pl.BlockSpec((1, tq, gw), lambda b, g, i: (b, i, ga_blk + g)),
            vec, vec, full, full, full,
        ],
        out_specs=pl.BlockSpec((1, tq, gw), lambda b, g, i: (b, i, g)),
        out_shape=jax.ShapeDtypeStruct((B, S, A_WIDTH), BF16),
        scratch_shapes=[pltpu.VMEM((S, HEAD_DIM), BF16)],
        compiler_params=pltpu.CompilerParams(
            dimension_semantics=("parallel", "parallel", "arbitrary"),
            vmem_limit_bytes=V7X_VMEM_LIMIT_BYTES),
        name="attention_a",
    )(proj3, proj3, proj3, proj3, q_gain, k_gain, cos, slo, shi)


def _band_tile(q, kw, vw, valid):
    s = lax.dot_general(q, kw, (((1,), (1,)), ((), ())), preferred_element_type=F32)
    s = jnp.where(valid, s, MASK_VALUE)
    m = jnp.max(s, axis=-1, keepdims=True)
    p = jnp.exp(s - m)
    l = jnp.sum(p, axis=-1, keepdims=True)
    o = jnp.dot(p.astype(BF16), vw, preferred_element_type=F32)
    return o * (1.0 / l), jnp.broadcast_to(m + jnp.log(l), (BAND_TQ, HEAD_DIM))


def _band_mix_kernel(*refs, S, dilations):
    n_g = len(dilations)
    qkv_refs = refs[:3 * n_g]
    gb_ref, cos_ref, slo_ref, shi_ref, y_ref = refs[3 * n_g:3 * n_g + 5]
    qf_ref, kf_ref, vf_ref, kp_ref, vp_ref, og_ref, lg_ref = refs[3 * n_g + 5:]
    shift = PARTIAL_ROPE_DIM // 2
    scale = HEAD_DIM ** -0.5
    win = BAND_TQ + 2 * BAND_HALF
    cos, slo, shi = cos_ref[...], slo_ref[...], shi_ref[...]

    for g, d in enumerate(dilations):
        q_ref, k_ref, v_ref = qkv_refs[3 * g:3 * g + 3]
        L = S // d
        n_tiles = L // BAND_TQ
        qf_ref[...] = _rotary(q_ref[0].astype(F32), cos, slo, shi, shift) * scale
        kf_ref[...] = _rotary(k_ref[0].astype(F32), cos, slo, shi, shift)
        vf_ref[...] = v_ref[0].astype(F32)

        def rows_of(r, t):
            return pl.ds(r + t * (BAND_TQ * d), BAND_TQ, stride=d) if d > 1 else \
                pl.ds(pl.multiple_of(t * BAND_TQ, BAND_TQ), BAND_TQ)

        if n_tiles == 1:
            qi = lax.broadcasted_iota(jnp.int32, (BAND_TQ, BAND_TQ), 0)
            kj = lax.broadcasted_iota(jnp.int32, (BAND_TQ, BAND_TQ), 1)
            valid = (kj >= qi - BAND_HALF) & (kj <= qi + BAND_HALF)

            def step(i, carry):
                for u in range(BAND_TILES_PER_STEP):
                    rows = rows_of(i * BAND_TILES_PER_STEP + u, 0)
                    o, lse = _band_tile(qf_ref[rows, :].astype(BF16),
                                        kf_ref[rows, :].astype(BF16),
                                        vf_ref[rows, :].astype(BF16), valid)
                    og_ref[g, rows, :] = o
                    lg_ref[g, rows, :] = lse
                return carry

            lax.fori_loop(0, d // BAND_TILES_PER_STEP, step, 0)
            continue

        zeros = jnp.zeros((BAND_HALF, HEAD_DIM), BF16)
        kp_ref[0:BAND_HALF, :] = zeros
        vp_ref[0:BAND_HALF, :] = zeros
        kp_ref[BAND_HALF + L:2 * BAND_HALF + L, :] = zeros
        vp_ref[BAND_HALF + L:2 * BAND_HALF + L, :] = zeros
        qi = lax.broadcasted_iota(jnp.int32, (BAND_TQ, win), 0)
        kj = lax.broadcasted_iota(jnp.int32, (BAND_TQ, win), 1)
        in_band = (kj >= qi) & (kj <= qi + 2 * BAND_HALF)

        def tiles(r, t0, n):
            for u in range(n):
                t = t0 + u
                q0 = t * BAND_TQ
                kpos = kj + (q0 - BAND_HALF)
                valid = in_band & (kpos >= 0) & (kpos < L)
                if not isinstance(q0, int):
                    q0 = pl.multiple_of(q0, BAND_TQ)
                rows = rows_of(r, t)
                o, lse = _band_tile(qf_ref[rows, :].astype(BF16),
                                    kp_ref[pl.ds(q0, win), :], vp_ref[pl.ds(q0, win), :], valid)
                og_ref[g, rows, :] = o
                lg_ref[g, rows, :] = lse

        def residue(r, carry):
            seq = pl.ds(r, L, stride=d) if d > 1 else slice(None)
            kp_ref[BAND_HALF:BAND_HALF + L, :] = kf_ref[seq, :].astype(BF16)
            vp_ref[BAND_HALF:BAND_HALF + L, :] = vf_ref[seq, :].astype(BF16)
            if n_tiles <= BAND_TILES_PER_STEP:
                tiles(r, 0, n_tiles)
            else:
                lax.fori_loop(
                    0, n_tiles // BAND_TILES_PER_STEP,
                    lambda i, c: (tiles(r, i * BAND_TILES_PER_STEP, BAND_TILES_PER_STEP), c)[1],
                    0)
            return carry

        if d == 1:
            residue(0, 0)
        else:
            lax.fori_loop(0, d, residue, 0)

    chunk = 2 * BAND_TQ

    def mix(i, carry):
        rows = pl.ds(pl.multiple_of(i * chunk, chunk), chunk)
        lses = [lg_ref[g, rows, :] for g in range(n_g)]
        lmax = functools.reduce(jnp.maximum, lses)
        es = [jnp.exp(l - lmax) for l in lses]
        num = sum(e * og_ref[g, rows, :] for g, e in enumerate(es))
        ob = num * (1.0 / sum(es))
        y_ref[0, rows, :] = (ob * _silu(gb_ref[0, rows, :].astype(F32))).astype(BF16)
        return carry

    lax.fori_loop(0, S // chunk, mix, 0)


def _band_mix(proj3, tables):
    B, S, NC = proj3.shape
    dilations = tuple(d for _, d in B_PATTERNS)
    n_g = len(dilations)
    max_l = max(S // d for d in dilations)
    assert all((S // d) % BAND_TQ == 0 for d in dilations)
    assert all(d % BAND_TILES_PER_STEP == 0 for d in dilations if S // d == BAND_TQ)

    def head_blk(off, g):
        first = off // HEAD_DIM + g * B_HEADS
        return pl.BlockSpec((1, S, HEAD_DIM), lambda b, h: (b, 0, first + h))

    qkv_specs = [head_blk(off, g) for g in range(n_g) for off in (_OFF_QB, _OFF_KB, _OFF_VB)]
    gb_first = _OFF_GB // HEAD_DIM
    tab = pl.BlockSpec((S, HEAD_DIM), lambda b, h: (0, 0))
    return pl.pallas_call(
        functools.partial(_band_mix_kernel, S=S, dilations=dilations),
        grid=(B, B_HEADS),
        in_specs=qkv_specs
        + [pl.BlockSpec((1, S, HEAD_DIM), lambda b, h: (b, 0, gb_first + h)), tab, tab, tab],
        out_specs=pl.BlockSpec((1, S, HEAD_DIM), lambda b, h: (b, 0, h)),
        out_shape=jax.ShapeDtypeStruct((B, S, B_WIDTH), BF16),
        scratch_shapes=[
            pltpu.VMEM((S, HEAD_DIM), F32),
            pltpu.VMEM((S, HEAD_DIM), F32),
            pltpu.VMEM((S, HEAD_DIM), F32),
            pltpu.VMEM((max_l + 2 * BAND_HALF, HEAD_DIM), BF16),
            pltpu.VMEM((max_l + 2 * BAND_HALF, HEAD_DIM), BF16),
            pltpu.VMEM((n_g, S, HEAD_DIM), F32),
            pltpu.VMEM((n_g, S, HEAD_DIM), F32),
        ],
        compiler_params=pltpu.CompilerParams(
            dimension_semantics=("parallel", "parallel"),
            vmem_limit_bytes=V7X_VMEM_LIMIT_BYTES),
        name="band_mix",
    )(*([proj3] * (3 * n_g + 1)), *tables)


def _merge_kernel(x_ref, ya_ref, yb_ref, za0, za1, za2, za3, zb0, zb1, zb2, zb3, bias_ref,
                  wa_ref, wb_ref, wo_ref, fg_ref, out_ref, merged_ref, *, final_norm):
    pa = jnp.dot(ya_ref[...], wa_ref[...], preferred_element_type=F32)
    pb = jnp.dot(yb_ref[...], wb_ref[...], preferred_element_type=F32)
    for c, (za, zb) in enumerate(((za0, zb0), (za1, zb1), (za2, zb2), (za3, zb3))):
        cols = slice(c * COL_BLK, (c + 1) * COL_BLK)
        gate_a = jax.nn.sigmoid(za[...].astype(F32) + bias_ref[0:1, cols])
        gate_b = jax.nn.sigmoid(zb[...].astype(F32) + bias_ref[1:2, cols])
        merged_ref[:, cols] = (gate_a * pa[:, cols] + gate_b * pb[:, cols]).astype(BF16)

    y = x_ref[...] + jnp.dot(merged_ref[...], wo_ref[...], preferred_element_type=F32)
    out_ref[...] = _rms(y, fg_ref[...]) if final_norm else y


def _merge(x2, ya, yb, proj2, bias, wa, wb, wo, final_gain, final_norm, tm=256):
    T, D = x2.shape
    za_blk = _OFF_ZA // COL_BLK
    n_z = D // COL_BLK

    def row_blk(width, col=0):
        return pl.BlockSpec((tm, width), lambda i: (i, col))

    def whole(a):
        return pl.BlockSpec(a.shape, lambda i: (0, 0))

    z_specs = [row_blk(COL_BLK, za_blk + c) for c in range(2 * n_z)]
    return pl.pallas_call(
        functools.partial(_merge_kernel, final_norm=final_norm),
        grid=(T // tm,),
        in_specs=[row_blk(D), row_blk(A_WIDTH), row_blk(B_WIDTH)] + z_specs
        + [whole(bias), whole(wa), whole(wb), whole(wo), whole(final_gain)],
        out_specs=row_blk(D),
        out_shape=jax.ShapeDtypeStruct((T, D), F32),
        scratch_shapes=[pltpu.VMEM((tm, D), BF16)],
        compiler_params=pltpu.CompilerParams(
            dimension_semantics=("parallel",),
            vmem_limit_bytes=V7X_VMEM_LIMIT_BYTES),
        name="merge_output",
    )(x2, ya, yb, *([proj2] * (2 * n_z)), bias, wa, wb, wo, final_gain)


def _angles(pos, dim, theta):
    expo = jnp.arange(0, dim, 2, dtype=F32) / dim
    inv_freq = 1.0 / jnp.power(jnp.asarray(theta, F32), expo)
    return pos.astype(F32)[:, None] * inv_freq[None, :]


def _lane_tables(ang_blocks):
    cos, slo, shi = [], [], []
    for blk in ang_blocks:
        if isinstance(blk, int):
            S = cos[0].shape[0]
            cos.append(jnp.ones((S, blk), F32))
            slo.append(jnp.zeros((S, blk), F32))
            shi.append(jnp.zeros((S, blk), F32))
            continue
        c, s, z = jnp.cos(blk), jnp.sin(blk), jnp.zeros_like(blk)
        cos += [c, c]
        slo += [-s, z]
        shi += [z, s]
    return tuple(jnp.concatenate(t, axis=-1) for t in (cos, slo, shi))


def _axial_tables(S):
    pos = jnp.arange(S, dtype=jnp.int32)
    half = HEAD_DIM // 2
    return _lane_tables([_angles(pos // GRID_W, half, AXIAL_THETA),
                         _angles(pos % GRID_W, half, AXIAL_THETA)])


def _partial_tables(S):
    pos = jnp.arange(S, dtype=jnp.int32)
    return _lane_tables([_angles(pos, PARTIAL_ROPE_DIM, ROPE_THETA),
                         HEAD_DIM - PARTIAL_ROPE_DIM])


def kernel(x, norm_gain, w_in, q_norm_gain, k_norm_gain, merge_gate_bias, w_branch_a,
           w_branch_b, w_out, final_norm_gain):
    B, S, D = x.shape
    depth = norm_gain.shape[0]
    assert w_in.shape[2] == IN_COLS_TOTAL and S % GRID_W == 0
    assert all(w // (2 * d) == BAND_HALF for w, d in B_PATTERNS)
    axial = _axial_tables(S)
    partial = _partial_tables(S)
    x2 = x.reshape(B * S, D)
    for l in range(depth):
        proj2 = _input_projection(x2, norm_gain[l][None, :], w_in[l].astype(BF16))
        proj3 = proj2.reshape(B, S, IN_COLS_TOTAL)
        ya = _attention_a(proj3, q_norm_gain[l][None, :], k_norm_gain[l][None, :], axial)
        yb = _band_mix(proj3, partial)
        x2 = _merge(x2, ya.reshape(B * S, A_WIDTH), yb.reshape(B * S, B_WIDTH), proj2,
                    merge_gate_bias[l], w_branch_a[l].astype(BF16),
                    w_branch_b[l].astype(BF16), w_out[l].astype(BF16),
                    final_norm_gain[None, :], final_norm=(l == depth - 1))
    return x2.reshape(B, S, D)
```

```python
import functools
import math

import numpy as np
import jax
import jax.numpy as jnp
from jax import lax
from jax.experimental import pallas as pl
from jax.experimental.pallas import tpu as pltpu

F32 = jnp.float32
BF16 = jnp.bfloat16

HEAD_DIM = 128
GRID_W = 64
NORM_EPS = 1e-6
A_Q_HEADS = 8
A_KV_HEADS = 2
A_GROUP = A_Q_HEADS // A_KV_HEADS
A_WIDTH = A_Q_HEADS * HEAD_DIM
A_KV_WIDTH = A_KV_HEADS * HEAD_DIM
AXIAL_THETA = 10000.0
AXIAL_SHIFT = HEAD_DIM // 4
B_PATTERNS = ((128, 1), (512, 4), (2048, 16))
B_GROUPS = len(B_PATTERNS)
B_HEADS = 4
B_WIDTH = B_HEADS * HEAD_DIM
PARTIAL_ROPE_DIM = HEAD_DIM // 4
PARTIAL_SHIFT = PARTIAL_ROPE_DIM // 2
ROPE_THETA = 500000.0
BAND_HALF = 64
BAND_TQ = 256
BAND_WIN = BAND_TQ + 2 * BAND_HALF
SPLIT = 4
MASK_VALUE = -1e30
LOG2_E = math.log2(math.e)
LN_2 = math.log(2.0)
Q_SCALE = HEAD_DIM ** -0.5 * LOG2_E

COL_BLK = 512
_OFF_QA = 0
_OFF_KA = A_WIDTH
_OFF_VA = _OFF_KA + A_KV_WIDTH
_OFF_GA = _OFF_VA + A_KV_WIDTH
_OFF_QB = _OFF_GA + A_WIDTH
_OFF_KB = _OFF_QB + B_GROUPS * B_WIDTH
_OFF_VB = _OFF_KB + B_GROUPS * B_WIDTH
_OFF_GB = _OFF_VB + B_GROUPS * B_WIDTH
_OFF_ZA = _OFF_GB + B_WIDTH

V7X_VMEM_LIMIT_BYTES = 56 * 1024 * 1024


def _rotary(x, cos, sin, shift):
    lane = lax.broadcasted_iota(jnp.int32, x.shape, 1)
    up = pltpu.roll(x, HEAD_DIM - shift, 1)
    dn = pltpu.roll(x, shift, 1)
    partner = jnp.where(lane % (2 * shift) < shift, up, dn)
    return x * cos + partner * sin


def _rms(x, gain):
    ms = jnp.mean(x * x, axis=-1, keepdims=True)
    return x * lax.rsqrt(ms + NORM_EPS) * gain


def _silu(g):
    return g * jax.nn.sigmoid(g)


def _proj_kernel(x_ref, g_ref, w_ref, o_ref, h_ref):
    @pl.when(pl.program_id(1) == 0)
    def _():
        h_ref[...] = _rms(x_ref[...], g_ref[...]).astype(BF16)

    o_ref[...] = jnp.dot(h_ref[...], w_ref[...], preferred_element_type=F32).astype(BF16)


def _input_projection(x2, gain, w_bf16, tm=1024, tn=COL_BLK):
    T, D = x2.shape
    N = w_bf16.shape[1]
    return pl.pallas_call(
        _proj_kernel,
        grid=(T // tm, N // tn),
        in_specs=[
            pl.BlockSpec((tm, D), lambda i, j: (i, 0)),
            pl.BlockSpec((1, D), lambda i, j: (0, 0)),
            pl.BlockSpec((D, tn), lambda i, j: (0, j)),
        ],
        out_specs=pl.BlockSpec((tm, tn), lambda i, j: (i, j)),
        out_shape=jax.ShapeDtypeStruct((T, N), BF16),
        scratch_shapes=[pltpu.VMEM((tm, D), BF16)],
        compiler_params=pltpu.CompilerParams(
            dimension_semantics=("parallel", "arbitrary"),
            vmem_limit_bytes=V7X_VMEM_LIMIT_BYTES),
        name="input_projection",
    )(x2, gain, w_bf16)


def _attn_a_kernel(q_ref, k_ref, v_ref, ga_ref, gq_ref, gk_ref, cos_ref, sin_ref,
                   o_ref, kn_ref, *, tq):
    qi = pl.program_id(2)

    @pl.when(qi == 0)
    def _():
        k = _rms(k_ref[0].astype(F32), gk_ref[...])
        kn_ref[...] = _rotary(k, cos_ref[...], sin_ref[...], AXIAL_SHIFT).astype(BF16)

    rows = pl.ds(pl.multiple_of(qi * tq, tq), tq)
    cos, sin = cos_ref[rows, :], sin_ref[rows, :]
    kn = kn_ref[...]
    v = v_ref[0]
    for h in range(A_GROUP):
        cols = slice(h * HEAD_DIM, (h + 1) * HEAD_DIM)
        q = _rms(q_ref[0, :, cols].astype(F32), gq_ref[...])
        q = (_rotary(q, cos, sin, AXIAL_SHIFT) * Q_SCALE).astype(BF16)
        s = lax.dot_general(q, kn, (((1,), (1,)), ((), ())), preferred_element_type=F32)
        m = jnp.max(s, axis=-1, keepdims=True)
        p = jnp.exp2(s - m)
        l = jnp.sum(p, axis=-1, keepdims=True)
        o = jnp.dot(p.astype(BF16), v, preferred_element_type=F32)
        g = ga_ref[0, :, cols].astype(F32)
        o_ref[0, :, cols] = (o * (1.0 / l) * _silu(g)).astype(BF16)


def _attention_a(proj3, q_gain, k_gain, tables, tq=256):
    B, S, _ = proj3.shape
    gw = A_GROUP * HEAD_DIM
    ka_blk = _OFF_KA // HEAD_DIM
    va_blk = _OFF_VA // HEAD_DIM
    ga_blk = _OFF_GA // gw
    full = pl.BlockSpec((S, HEAD_DIM), lambda b, g, i: (0, 0))
    vec = pl.BlockSpec((1, HEAD_DIM), lambda b, g, i: (0, 0))
    return pl.pallas_call(
        functools.partial(_attn_a_kernel, tq=tq),
        grid=(B, A_KV_HEADS, S // tq),
        in_specs=[
            pl.BlockSpec((1, tq, gw), lambda b, g, i: (b, i, g)),
            pl.BlockSpec((1, S, HEAD_DIM), lambda b, g, i: (b, 0, ka_blk + g)),
            pl.BlockSpec((1, S, HEAD_DIM), lambda b, g, i: (b, 0, va_blk + g)),
            pl.BlockSpec((1, tq, gw), lambda b, g, i: (b, i, ga_blk + g)),
            vec, vec, full, full,
        ],
        out_specs=pl.BlockSpec((1, tq, gw), lambda b, g, i: (b, i, g)),
        out_shape=jax.ShapeDtypeStruct((B, S, A_WIDTH), BF16),
        scratch_shapes=[pltpu.VMEM((S, HEAD_DIM), BF16)],
        compiler_params=pltpu.CompilerParams(
            dimension_semantics=("parallel", "parallel", "arbitrary"),
            vmem_limit_bytes=V7X_VMEM_LIMIT_BYTES),
        name="attention_a",
    )(proj3, proj3, proj3, proj3, q_gain, k_gain, *tables)


def _band_tile(q, kw, vw, valid):
    s = lax.dot_general(q, kw, (((1,), (1,)), ((), ())), preferred_element_type=F32)
    s = jnp.where(valid, s, MASK_VALUE)
    m = jnp.max(s, axis=-1, keepdims=True)
    p = jnp.exp2(s - m)
    l = jnp.sum(p, axis=-1, keepdims=True)
    o = jnp.dot(p.astype(BF16), vw, preferred_element_type=F32)
    lse = m * LN_2 + jnp.log(l)
    return o * (1.0 / l), jnp.broadcast_to(lse, (BAND_TQ, HEAD_DIM))


def _window_mask(t, n_tiles):
    qi = lax.broadcasted_iota(jnp.int32, (BAND_TQ, BAND_WIN), 0)
    kj = lax.broadcasted_iota(jnp.int32, (BAND_TQ, BAND_WIN), 1)
    kpos = kj + (t * BAND_TQ - BAND_HALF)
    return (kj >= qi) & (kj <= qi + 2 * BAND_HALF) & (kpos >= 0) & (kpos < n_tiles * BAND_TQ)


def _zero_pads(ref, lead, n_rows):
    zeros = jnp.zeros((BAND_HALF, HEAD_DIM), BF16)
    for idx in lead:
        ref[idx + (slice(0, BAND_HALF),)] = zeros
        ref[idx + (slice(BAND_HALF + n_rows, 2 * BAND_HALF + n_rows),)] = zeros


def _band_mix_kernel(q1_ref, k1_ref, v1_ref, q4_ref, k4_ref, v4_ref, q16_ref, k16_ref, v16_ref,
                     gb_ref, cos_ref, sin_ref, y_ref,
                     nat_ref, tmp_ref, qd1_ref, kp1_ref, vp1_ref, qd4_ref, kp4_ref, vp4_ref,
                     qd16_ref, kd16_ref, vd16_ref, od16_ref, og_ref, lg_ref, *, S):
    L4 = S // SPLIT
    L16 = L4 // SPLIT
    n1, n4 = S // BAND_TQ, L4 // BAND_TQ
    per_tile = BAND_TQ // L16

    def rotated(src_ref, scale=None):
        y = _rotary(src_ref[0].astype(F32), cos_ref[...], sin_ref[...], PARTIAL_SHIFT)
        return y if scale is None else y * scale

    _zero_pads(kp1_ref, [()], S)
    _zero_pads(vp1_ref, [()], S)
    qd1_ref[...] = rotated(q1_ref, Q_SCALE).astype(BF16)
    kp1_ref[BAND_HALF:BAND_HALF + S, :] = rotated(k1_ref).astype(BF16)
    vp1_ref[BAND_HALF:BAND_HALF + S, :] = v1_ref[0]

    for t in range(n1):
        rows = slice(t * BAND_TQ, (t + 1) * BAND_TQ)
        win = slice(t * BAND_TQ, t * BAND_TQ + BAND_WIN)
        o, lse = _band_tile(qd1_ref[rows, :], kp1_ref[win, :], vp1_ref[win, :],
                            _window_mask(t, n1))
        og_ref[0, rows, :] = o
        lg_ref[0, rows, :] = lse

    _zero_pads(kp4_ref, [(r,) for r in range(SPLIT)], L4)
    _zero_pads(vp4_ref, [(r,) for r in range(SPLIT)], L4)
    for n, (val, dst_ref, lead) in enumerate(((rotated(q4_ref, Q_SCALE), qd4_ref, 0),
                                              (rotated(k4_ref), kp4_ref, BAND_HALF),
                                              (v4_ref[0].astype(F32), vp4_ref, BAND_HALF))):
        nat_ref[n] = val
        for r in range(SPLIT):
            dst_ref[r, lead:lead + L4, :] = nat_ref[n, pl.ds(r, L4, stride=SPLIT), :].astype(BF16)

    for r in range(SPLIT):
        for t in range(n4):
            rows = pl.ds(r + t * (BAND_TQ * SPLIT), BAND_TQ, stride=SPLIT)
            win = slice(t * BAND_TQ, t * BAND_TQ + BAND_WIN)
            o, lse = _band_tile(qd4_ref[r, t * BAND_TQ:(t + 1) * BAND_TQ, :],
                                kp4_ref[r, win, :], vp4_ref[r, win, :], _window_mask(t, n4))
            og_ref[1, rows, :] = o
            lg_ref[1, rows, :] = lse

    for n, (val, dst_ref) in enumerate(((rotated(q16_ref, Q_SCALE), qd16_ref),
                                        (rotated(k16_ref), kd16_ref),
                                        (v16_ref[0].astype(F32), vd16_ref))):
        nat_ref[3 + n] = val
        for a in range(SPLIT):
            tmp_ref[n, a] = nat_ref[3 + n, pl.ds(a, L4, stride=SPLIT), :]
            for b in range(SPLIT):
                r = a + SPLIT * b
                dst_ref[r // per_tile, (r % per_tile) * L16:(r % per_tile + 1) * L16, :] = \
                    tmp_ref[n, a, pl.ds(b, L16, stride=SPLIT), :].astype(BF16)

    qi = lax.broadcasted_iota(jnp.int32, (BAND_TQ, BAND_TQ), 0)
    kj = lax.broadcasted_iota(jnp.int32, (BAND_TQ, BAND_TQ), 1)
    same_residue = functools.reduce(
        jnp.logical_or, [(qi >= c * L16) & (qi < (c + 1) * L16) & (kj >= c * L16)
                         & (kj < (c + 1) * L16) for c in range(per_tile)])
    stacked_band = same_residue & (kj >= qi - BAND_HALF) & (kj <= qi + BAND_HALF)
    for j in range(SPLIT * SPLIT // per_tile):
        o, lse = _band_tile(qd16_ref[j], kd16_ref[j], vd16_ref[j], stacked_band)
        od16_ref[0, j] = o
        od16_ref[1, j] = lse

    for n, dst_ref in enumerate((og_ref, lg_ref)):
        for a in range(SPLIT):
            for b in range(SPLIT):
                r = a + SPLIT * b
                tmp_ref[n, a, pl.ds(b, L16, stride=SPLIT), :] = \
                    od16_ref[n, r // per_tile, (r % per_tile) * L16:(r % per_tile + 1) * L16, :]
            dst_ref[2, pl.ds(a, L4, stride=SPLIT), :] = tmp_ref[n, a]

    chunk = BAND_TQ

    def mix(i, carry):
        rows = pl.ds(pl.multiple_of(i * chunk, chunk), chunk)
        lses = [lg_ref[g, rows, :] for g in range(B_GROUPS)]
        lmax = functools.reduce(jnp.maximum, lses)
        es = [jnp.exp(l - lmax) for l in lses]
        num = sum(e * og_ref[g, rows, :] for g, e in enumerate(es))
        ob = num * (1.0 / sum(es))
        y_ref[0, rows, :] = (ob * _silu(gb_ref[0, rows, :].astype(F32))).astype(BF16)
        return carry

    lax.fori_loop(0, S // chunk, mix, 0)


def _band_mix(proj3, tables):
    B, S, NC = proj3.shape
    assert tuple(d for _, d in B_PATTERNS) == (1, SPLIT, SPLIT * SPLIT)
    L4, L16 = S // SPLIT, S // (SPLIT * SPLIT)
    assert L4 % BAND_TQ == 0 and BAND_TQ % L16 == 0
    n16 = S // BAND_TQ

    def head_blk(off, g):
        first = off // HEAD_DIM + g * B_HEADS
        return pl.BlockSpec((1, S, HEAD_DIM), lambda b, h: (b, 0, first + h))

    qkv_specs = [head_blk(off, g) for g in range(B_GROUPS)
                 for off in (_OFF_QB, _OFF_KB, _OFF_VB)]
    gb_first = _OFF_GB // HEAD_DIM
    table = pl.BlockSpec((S, HEAD_DIM), lambda b, h: (0, 0))
    return pl.pallas_call(
        functools.partial(_band_mix_kernel, S=S),
        grid=(B, B_HEADS),
        in_specs=qkv_specs
        + [pl.BlockSpec((1, S, HEAD_DIM), lambda b, h: (b, 0, gb_first + h)), table, table],
        out_specs=pl.BlockSpec((1, S, HEAD_DIM), lambda b, h: (b, 0, h)),
        out_shape=jax.ShapeDtypeStruct((B, S, B_WIDTH), BF16),
        scratch_shapes=[
            pltpu.VMEM((6, S, HEAD_DIM), F32),
            pltpu.VMEM((3, SPLIT, L4, HEAD_DIM), F32),
            pltpu.VMEM((S, HEAD_DIM), BF16),
            pltpu.VMEM((S + 2 * BAND_HALF, HEAD_DIM), BF16),
            pltpu.VMEM((S + 2 * BAND_HALF, HEAD_DIM), BF16),
            pltpu.VMEM((SPLIT, L4, HEAD_DIM), BF16),
            pltpu.VMEM((SPLIT, L4 + 2 * BAND_HALF, HEAD_DIM), BF16),
            pltpu.VMEM((SPLIT, L4 + 2 * BAND_HALF, HEAD_DIM), BF16),
            pltpu.VMEM((n16, BAND_TQ, HEAD_DIM), BF16),
            pltpu.VMEM((n16, BAND_TQ, HEAD_DIM), BF16),
            pltpu.VMEM((n16, BAND_TQ, HEAD_DIM), BF16),
            pltpu.VMEM((2, n16, BAND_TQ, HEAD_DIM), F32),
            pltpu.VMEM((B_GROUPS, S, HEAD_DIM), F32),
            pltpu.VMEM((B_GROUPS, S, HEAD_DIM), F32),
        ],
        compiler_params=pltpu.CompilerParams(
            dimension_semantics=("parallel", "parallel"),
            vmem_limit_bytes=V7X_VMEM_LIMIT_BYTES),
        name="band_mix",
    )(*([proj3] * (3 * B_GROUPS + 1)), *tables)


def _merge_kernel(x_ref, ya_ref, yb_ref, za0, za1, za2, za3, zb0, zb1, zb2, zb3, bias_ref,
                  wa_ref, wb_ref, wo_ref, fg_ref, out_ref, merged_ref, *, final_norm):
    pa = jnp.dot(ya_ref[...], wa_ref[...], preferred_element_type=F32)
    pb = jnp.dot(yb_ref[...], wb_ref[...], preferred_element_type=F32)
    for c, (za, zb) in enumerate(((za0, zb0), (za1, zb1), (za2, zb2), (za3, zb3))):
        cols = slice(c * COL_BLK, (c + 1) * COL_BLK)
        gate_a = jax.nn.sigmoid(za[...].astype(F32) + bias_ref[0:1, cols])
        gate_b = jax.nn.sigmoid(zb[...].astype(F32) + bias_ref[1:2, cols])
        merged_ref[:, cols] = (gate_a * pa[:, cols] + gate_b * pb[:, cols]).astype(BF16)

    y = x_ref[...] + jnp.dot(merged_ref[...], wo_ref[...], preferred_element_type=F32)
    out_ref[...] = _rms(y, fg_ref[...]) if final_norm else y


def _merge(x2, ya, yb, proj2, bias, wa, wb, wo, final_gain, final_norm, tm=256):
    T, D = x2.shape
    za_blk = _OFF_ZA // COL_BLK
    n_z = D // COL_BLK

    def row_blk(width, col=0):
        return pl.BlockSpec((tm, width), lambda i: (i, col))

    def whole(a):
        return pl.BlockSpec(a.shape, lambda i: (0, 0))

    z_specs = [row_blk(COL_BLK, za_blk + c) for c in range(2 * n_z)]
    return pl.pallas_call(
        functools.partial(_merge_kernel, final_norm=final_norm),
        grid=(T // tm,),
        in_specs=[row_blk(D), row_blk(A_WIDTH), row_blk(B_WIDTH)] + z_specs
        + [whole(bias), whole(wa), whole(wb), whole(wo), whole(final_gain)],
        out_specs=row_blk(D),
        out_shape=jax.ShapeDtypeStruct((T, D), F32),
        scratch_shapes=[pltpu.VMEM((tm, D), BF16)],
        compiler_params=pltpu.CompilerParams(
            dimension_semantics=("parallel",),
            vmem_limit_bytes=V7X_VMEM_LIMIT_BYTES),
        name="merge_output",
    )(x2, ya, yb, *([proj2] * (2 * n_z)), bias, wa, wb, wo, final_gain)


def _angles(pos, dim, theta):
    expo = np.arange(0, dim, 2, dtype=np.float64) / dim
    return pos.astype(np.float64)[:, None] / np.power(float(theta), expo)[None, :]


def _lane_tables(ang_blocks):
    cos, sin = [], []
    for blk in ang_blocks:
        if isinstance(blk, int):
            S = cos[0].shape[0]
            cos.append(np.ones((S, blk)))
            sin.append(np.zeros((S, blk)))
            continue
        c, s = np.cos(blk), np.sin(blk)
        cos += [c, c]
        sin += [-s, s]
    return tuple(jnp.asarray(np.concatenate(t, axis=-1), dtype=F32) for t in (cos, sin))


def _axial_tables(S):
    pos = np.arange(S)
    half = HEAD_DIM // 2
    return _lane_tables([_angles(pos // GRID_W, half, AXIAL_THETA),
                         _angles(pos % GRID_W, half, AXIAL_THETA)])


def _partial_tables(S):
    pos = np.arange(S)
    return _lane_tables([_angles(pos, PARTIAL_ROPE_DIM, ROPE_THETA),
                         HEAD_DIM - PARTIAL_ROPE_DIM])


def kernel(x, norm_gain, w_in, q_norm_gain, k_norm_gain, merge_gate_bias, w_branch_a,
           w_branch_b, w_out, final_norm_gain):
    B, S, D = x.shape
    depth = norm_gain.shape[0]
    n_cols = _OFF_ZA + 2 * D
    assert w_in.shape[2] == n_cols and S % GRID_W == 0
    assert all(w // (2 * d) == BAND_HALF for w, d in B_PATTERNS)
    axial = _axial_tables(S)
    partial = _partial_tables(S)
    x2 = x.reshape(B * S, D)
    for l in range(depth):
        proj2 = _input_projection(x2, norm_gain[l][None, :], w_in[l].astype(BF16))
        proj3 = proj2.reshape(B, S, n_cols)
        ya = _attention_a(proj3, q_norm_gain[l][None, :], k_norm_gain[l][None, :], axial)
        yb = _band_mix(proj3, partial)
        x2 = _merge(x2, ya.reshape(B * S, A_WIDTH), yb.reshape(B * S, B_WIDTH), proj2,
                    merge_gate_bias[l], w_branch_a[l].astype(BF16),
                    w_branch_b[l].astype(BF16), w_out[l].astype(BF16),
                    final_norm_gain[None, :], final_norm=(l == depth - 1))
    return x2.reshape(B, S, D)
```

```python
import functools
import math

import numpy as np
import jax
import jax.numpy as jnp
from jax import lax
from jax.experimental import pallas as pl
from jax.experimental.pallas import tpu as pltpu

F32 = jnp.float32
BF16 = jnp.bfloat16

HEAD_DIM = 128
GRID_W = 64
NORM_EPS = 1e-6
A_Q_HEADS = 8
A_KV_HEADS = 2
A_GROUP = A_Q_HEADS // A_KV_HEADS
A_WIDTH = A_Q_HEADS * HEAD_DIM
A_KV_WIDTH = A_KV_HEADS * HEAD_DIM
AXIAL_THETA = 10000.0
AXIAL_SHIFT = HEAD_DIM // 4
B_PATTERNS = ((128, 1), (512, 4), (2048, 16))
B_GROUPS = len(B_PATTERNS)
B_HEADS = 4
B_WIDTH = B_HEADS * HEAD_DIM
PARTIAL_ROPE_DIM = HEAD_DIM // 4
PARTIAL_SHIFT = PARTIAL_ROPE_DIM // 2
ROPE_THETA = 500000.0
BAND_HALF = 64
BAND_TQ = 256
BAND_WIN = BAND_TQ + 2 * BAND_HALF
SPLIT = 4
MASK_VALUE = -1e30
LOG2_E = math.log2(math.e)
LN_2 = math.log(2.0)
Q_SCALE = HEAD_DIM ** -0.5 * LOG2_E

COL_BLK = 512
_OFF_QA = 0
_OFF_KA = A_WIDTH
_OFF_VA = _OFF_KA + A_KV_WIDTH
_OFF_GA = _OFF_VA + A_KV_WIDTH
_OFF_QB = _OFF_GA + A_WIDTH
_OFF_KB = _OFF_QB + B_GROUPS * B_WIDTH
_OFF_VB = _OFF_KB + B_GROUPS * B_WIDTH
_OFF_GB = _OFF_VB + B_GROUPS * B_WIDTH
_OFF_ZA = _OFF_GB + B_WIDTH

V7X_VMEM_LIMIT_BYTES = 56 * 1024 * 1024


def _rotary(x, cos, sin, shift):
    lane = lax.broadcasted_iota(jnp.int32, x.shape, 1)
    up = pltpu.roll(x, HEAD_DIM - shift, 1)
    dn = pltpu.roll(x, shift, 1)
    partner = jnp.where(lane % (2 * shift) < shift, up, dn)
    return x * cos + partner * sin


def _rms(x, gain):
    ms = jnp.mean(x * x, axis=-1, keepdims=True)
    return x * lax.rsqrt(ms + NORM_EPS) * gain


def _silu(g):
    return g * jax.nn.sigmoid(g)


def _proj_kernel(x_ref, g_ref, w_ref, o_ref, h_ref):
    @pl.when(pl.program_id(1) == 0)
    def _():
        h_ref[...] = _rms(x_ref[...], g_ref[...]).astype(BF16)

    w = w_ref[...].astype(BF16)
    o_ref[...] = jnp.dot(h_ref[...], w, preferred_element_type=F32).astype(BF16)


def _input_projection(x2, gain, w, tm=1024, tn=COL_BLK):
    T, D = x2.shape
    N = w.shape[1]
    return pl.pallas_call(
        _proj_kernel,
        grid=(T // tm, N // tn),
        in_specs=[
            pl.BlockSpec((tm, D), lambda i, j: (i, 0)),
            pl.BlockSpec((1, D), lambda i, j: (0, 0)),
            pl.BlockSpec((D, tn), lambda i, j: (0, j)),
        ],
        out_specs=pl.BlockSpec((tm, tn), lambda i, j: (i, j)),
        out_shape=jax.ShapeDtypeStruct((T, N), BF16),
        scratch_shapes=[pltpu.VMEM((tm, D), BF16)],
        compiler_params=pltpu.CompilerParams(
            dimension_semantics=("parallel", "arbitrary"),
            vmem_limit_bytes=V7X_VMEM_LIMIT_BYTES),
        name="input_projection",
    )(x2, gain, w)


def _attn_a_kernel(q_ref, k_ref, v_ref, ga_ref, gq_ref, gk_ref, cos_ref, sin_ref,
                   o_ref, kn_ref, *, tq):
    qi = pl.program_id(2)

    @pl.when(qi == 0)
    def _():
        k = _rms(k_ref[0].astype(F32), gk_ref[...])
        kn_ref[...] = _rotary(k, cos_ref[...], sin_ref[...], AXIAL_SHIFT).astype(BF16)

    rows = pl.ds(pl.multiple_of(qi * tq, tq), tq)
    cos, sin = cos_ref[rows, :], sin_ref[rows, :]
    kn = kn_ref[...]
    v = v_ref[0]
    for h in range(A_GROUP):
        cols = slice(h * HEAD_DIM, (h + 1) * HEAD_DIM)
        q = _rms(q_ref[0, :, cols].astype(F32), gq_ref[...])
        q = (_rotary(q, cos, sin, AXIAL_SHIFT) * Q_SCALE).astype(BF16)
        s = lax.dot_general(q, kn, (((1,), (1,)), ((), ())), preferred_element_type=F32)
        m = jnp.max(s, axis=-1, keepdims=True)
        p = jnp.exp2(s - m)
        l = jnp.sum(p, axis=-1, keepdims=True)
        o = jnp.dot(p.astype(BF16), v, preferred_element_type=F32)
        g = ga_ref[0, :, cols].astype(F32)
        o_ref[0, :, cols] = (o * (1.0 / l) * _silu(g)).astype(BF16)


def _attention_a(proj3, q_gain, k_gain, tables, tq=256):
    B, S, _ = proj3.shape
    gw = A_GROUP * HEAD_DIM
    ka_blk = _OFF_KA // HEAD_DIM
    va_blk = _OFF_VA // HEAD_DIM
    ga_blk = _OFF_GA // gw
    full = pl.BlockSpec((S, HEAD_DIM), lambda b, g, i: (0, 0))
    vec = pl.BlockSpec((1, HEAD_DIM), lambda b, g, i: (0, 0))
    return pl.pallas_call(
        functools.partial(_attn_a_kernel, tq=tq),
        grid=(B, A_KV_HEADS, S // tq),
        in_specs=[
            pl.BlockSpec((1, tq, gw), lambda b, g, i: (b, i, g)),
            pl.BlockSpec((1, S, HEAD_DIM), lambda b, g, i: (b, 0, ka_blk + g)),
            pl.BlockSpec((1, S, HEAD_DIM), lambda b, g, i: (b, 0, va_blk + g)),
            pl.BlockSpec((1, tq, gw), lambda b, g, i: (b, i, ga_blk + g)),
            vec, vec, full, full,
        ],
        out_specs=pl.BlockSpec((1, tq, gw), lambda b, g, i: (b, i, g)),
        out_shape=jax.ShapeDtypeStruct((B, S, A_WIDTH), BF16),
        scratch_shapes=[pltpu.VMEM((S, HEAD_DIM), BF16)],
        compiler_params=pltpu.CompilerParams(
            dimension_semantics=("parallel", "parallel", "arbitrary"),
            vmem_limit_bytes=V7X_VMEM_LIMIT_BYTES),
        name="attention_a",
    )(proj3, proj3, proj3, proj3, q_gain, k_gain, *tables)


def _band_tile(q, kw, vw, valid):
    s = lax.dot_general(q, kw, (((1,), (1,)), ((), ())), preferred_element_type=F32)
    s = jnp.where(valid, s, MASK_VALUE)
    m = jnp.max(s, axis=-1, keepdims=True)
    p = jnp.exp2(s - m)
    l = jnp.sum(p, axis=-1, keepdims=True)
    o = jnp.dot(p.astype(BF16), vw, preferred_element_type=F32)
    lse = m * LN_2 + jnp.log(l)
    return o * (1.0 / l), jnp.broadcast_to(lse, (BAND_TQ, HEAD_DIM))


def _window_mask(t, n_tiles):
    qi = lax.broadcasted_iota(jnp.int32, (BAND_TQ, BAND_WIN), 0)
    kj = lax.broadcasted_iota(jnp.int32, (BAND_TQ, BAND_WIN), 1)
    kpos = kj + (t * BAND_TQ - BAND_HALF)
    return (kj >= qi) & (kj <= qi + 2 * BAND_HALF) & (kpos >= 0) & (kpos < n_tiles * BAND_TQ)


def _zero_pads(ref, lead, n_rows):
    zeros = jnp.zeros((BAND_HALF, HEAD_DIM), BF16)
    for idx in lead:
        ref[idx + (slice(0, BAND_HALF),)] = zeros
        ref[idx + (slice(BAND_HALF + n_rows, 2 * BAND_HALF + n_rows),)] = zeros


def _band_mix_kernel(q1_ref, k1_ref, v1_ref, q4_ref, k4_ref, v4_ref, q16_ref, k16_ref, v16_ref,
                     gb_ref, cos_ref, sin_ref, y_ref,
                     nat_ref, tmp_ref, qd1_ref, kp1_ref, vp1_ref, qd4_ref, kp4_ref, vp4_ref,
                     qd16_ref, kd16_ref, vd16_ref, od16_ref, og_ref, lg_ref, *, S):
    L4 = S // SPLIT
    L16 = L4 // SPLIT
    n1, n4 = S // BAND_TQ, L4 // BAND_TQ
    per_tile = BAND_TQ // L16

    def rotated(src_ref, scale=None):
        y = _rotary(src_ref[0].astype(F32), cos_ref[...], sin_ref[...], PARTIAL_SHIFT)
        return y if scale is None else y * scale

    _zero_pads(kp1_ref, [()], S)
    _zero_pads(vp1_ref, [()], S)
    qd1_ref[...] = rotated(q1_ref, Q_SCALE).astype(BF16)
    kp1_ref[BAND_HALF:BAND_HALF + S, :] = rotated(k1_ref).astype(BF16)
    vp1_ref[BAND_HALF:BAND_HALF + S, :] = v1_ref[0]

    for t in range(n1):
        rows = slice(t * BAND_TQ, (t + 1) * BAND_TQ)
        win = slice(t * BAND_TQ, t * BAND_TQ + BAND_WIN)
        o, lse = _band_tile(qd1_ref[rows, :], kp1_ref[win, :], vp1_ref[win, :],
                            _window_mask(t, n1))
        og_ref[0, rows, :] = o
        lg_ref[0, rows, :] = lse

    _zero_pads(kp4_ref, [(r,) for r in range(SPLIT)], L4)
    _zero_pads(vp4_ref, [(r,) for r in range(SPLIT)], L4)
    for n, (val, dst_ref, lead) in enumerate(((rotated(q4_ref, Q_SCALE), qd4_ref, 0),
                                              (rotated(k4_ref), kp4_ref, BAND_HALF),
                                              (v4_ref[0].astype(F32), vp4_ref, BAND_HALF))):
        nat_ref[n] = val
        for r in range(SPLIT):
            dst_ref[r, lead:lead + L4, :] = nat_ref[n, pl.ds(r, L4, stride=SPLIT), :].astype(BF16)

    for r in range(SPLIT):
        for t in range(n4):
            rows = pl.ds(r + t * (BAND_TQ * SPLIT), BAND_TQ, stride=SPLIT)
            win = slice(t * BAND_TQ, t * BAND_TQ + BAND_WIN)
            o, lse = _band_tile(qd4_ref[r, t * BAND_TQ:(t + 1) * BAND_TQ, :],
                                kp4_ref[r, win, :], vp4_ref[r, win, :], _window_mask(t, n4))
            og_ref[1, rows, :] = o
            lg_ref[1, rows, :] = lse

    for n, (val, dst_ref) in enumerate(((rotated(q16_ref, Q_SCALE), qd16_ref),
                                        (rotated(k16_ref), kd16_ref),
                                        (v16_ref[0].astype(F32), vd16_ref))):
        nat_ref[3 + n] = val
        for a in range(SPLIT):
            tmp_ref[n, a] = nat_ref[3 + n, pl.ds(a, L4, stride=SPLIT), :]
            for b in range(SPLIT):
                r = a + SPLIT * b
                dst_ref[r // per_tile, (r % per_tile) * L16:(r % per_tile + 1) * L16, :] = \
                    tmp_ref[n, a, pl.ds(b, L16, stride=SPLIT), :].astype(BF16)

    qi = lax.broadcasted_iota(jnp.int32, (BAND_TQ, BAND_TQ), 0)
    kj = lax.broadcasted_iota(jnp.int32, (BAND_TQ, BAND_TQ), 1)
    same_residue = functools.reduce(
        jnp.logical_or, [(qi >= c * L16) & (qi < (c + 1) * L16) & (kj >= c * L16)
                         & (kj < (c + 1) * L16) for c in range(per_tile)])
    stacked_band = same_residue & (kj >= qi - BAND_HALF) & (kj <= qi + BAND_HALF)
    for j in range(SPLIT * SPLIT // per_tile):
        o, lse = _band_tile(qd16_ref[j], kd16_ref[j], vd16_ref[j], stacked_band)
        od16_ref[0, j] = o
        od16_ref[1, j] = lse

    for n, dst_ref in enumerate((og_ref, lg_ref)):
        for a in range(SPLIT):
            for b in range(SPLIT):
                r = a + SPLIT * b
                tmp_ref[n, a, pl.ds(b, L16, stride=SPLIT), :] = \
                    od16_ref[n, r // per_tile, (r % per_tile) * L16:(r % per_tile + 1) * L16, :]
            dst_ref[2, pl.ds(a, L4, stride=SPLIT), :] = tmp_ref[n, a]

    chunk = BAND_TQ

    def mix(i, carry):
        rows = pl.ds(pl.multiple_of(i * chunk, chunk), chunk)
        lses = [lg_ref[g, rows, :] for g in range(B_GROUPS)]
        lmax = functools.reduce(jnp.maximum, lses)
        es = [jnp.exp(l - lmax) for l in lses]
        num = sum(e * og_ref[g, rows, :] for g, e in enumerate(es))
        ob = num * (1.0 / sum(es))
        y_ref[0, rows, :] = (ob * _silu(gb_ref[0, rows, :].astype(F32))).astype(BF16)
        return carry

    lax.fori_loop(0, S // chunk, mix, 0)


def _band_mix(proj3, tables):
    B, S, NC = proj3.shape
    assert tuple(d for _, d in B_PATTERNS) == (1, SPLIT, SPLIT * SPLIT)
    L4, L16 = S // SPLIT, S // (SPLIT * SPLIT)
    assert L4 % BAND_TQ == 0 and BAND_TQ % L16 == 0
    n16 = S // BAND_TQ

    def head_blk(off, g):
        first = off // HEAD_DIM + g * B_HEADS
        return pl.BlockSpec((1, S, HEAD_DIM), lambda b, h: (b, 0, first + h))

    qkv_specs = [head_blk(off, g) for g in range(B_GROUPS)
                 for off in (_OFF_QB, _OFF_KB, _OFF_VB)]
    gb_first = _OFF_GB // HEAD_DIM
    table = pl.BlockSpec((S, HEAD_DIM), lambda b, h: (0, 0))
    return pl.pallas_call(
        functools.partial(_band_mix_kernel, S=S),
        grid=(B, B_HEADS),
        in_specs=qkv_specs
        + [pl.BlockSpec((1, S, HEAD_DIM), lambda b, h: (b, 0, gb_first + h)), table, table],
        out_specs=pl.BlockSpec((1, S, HEAD_DIM), lambda b, h: (b, 0, h)),
        out_shape=jax.ShapeDtypeStruct((B, S, B_WIDTH), BF16),
        scratch_shapes=[
            pltpu.VMEM((6, S, HEAD_DIM), F32),
            pltpu.VMEM((3, SPLIT, L4, HEAD_DIM), F32),
            pltpu.VMEM((S, HEAD_DIM), BF16),
            pltpu.VMEM((S + 2 * BAND_HALF, HEAD_DIM), BF16),
            pltpu.VMEM((S + 2 * BAND_HALF, HEAD_DIM), BF16),
            pltpu.VMEM((SPLIT, L4, HEAD_DIM), BF16),
            pltpu.VMEM((SPLIT, L4 + 2 * BAND_HALF, HEAD_DIM), BF16),
            pltpu.VMEM((SPLIT, L4 + 2 * BAND_HALF, HEAD_DIM), BF16),
            pltpu.VMEM((n16, BAND_TQ, HEAD_DIM), BF16),
            pltpu.VMEM((n16, BAND_TQ, HEAD_DIM), BF16),
            pltpu.VMEM((n16, BAND_TQ, HEAD_DIM), BF16),
            pltpu.VMEM((2, n16, BAND_TQ, HEAD_DIM), F32),
            pltpu.VMEM((B_GROUPS, S, HEAD_DIM), F32),
            pltpu.VMEM((B_GROUPS, S, HEAD_DIM), F32),
        ],
        compiler_params=pltpu.CompilerParams(
            dimension_semantics=("parallel", "parallel"),
            vmem_limit_bytes=V7X_VMEM_LIMIT_BYTES),
        name="band_mix",
    )(*([proj3] * (3 * B_GROUPS + 1)), *tables)


def _merge_kernel(x_ref, ya_ref, yb_ref, za0, za1, za2, za3, zb0, zb1, zb2, zb3, bias_ref,
                  wa_ref, wb_ref, wo_ref, fg_ref, out_ref, merged_ref, *, final_norm):
    pa = jnp.dot(ya_ref[...], wa_ref[...], preferred_element_type=F32)
    pb = jnp.dot(yb_ref[...], wb_ref[...], preferred_element_type=F32)
    for c, (za, zb) in enumerate(((za0, zb0), (za1, zb1), (za2, zb2), (za3, zb3))):
        cols = slice(c * COL_BLK, (c + 1) * COL_BLK)
        gate_a = jax.nn.sigmoid(za[...].astype(F32) + bias_ref[0:1, cols])
        gate_b = jax.nn.sigmoid(zb[...].astype(F32) + bias_ref[1:2, cols])
        merged_ref[:, cols] = (gate_a * pa[:, cols] + gate_b * pb[:, cols]).astype(BF16)

    y = x_ref[...] + jnp.dot(merged_ref[...], wo_ref[...], preferred_element_type=F32)
    out_ref[...] = _rms(y, fg_ref[...]) if final_norm else y


def _merge(x2, ya, yb, proj2, bias, wa, wb, wo, final_gain, final_norm, tm=256):
    T, D = x2.shape
    za_blk = _OFF_ZA // COL_BLK
    n_z = D // COL_BLK

    def row_blk(width, col=0):
        return pl.BlockSpec((tm, width), lambda i: (i, col))

    def whole(a):
        return pl.BlockSpec(a.shape, lambda i: (0, 0))

    z_specs = [row_blk(COL_BLK, za_blk + c) for c in range(2 * n_z)]
    return pl.pallas_call(
        functools.partial(_merge_kernel, final_norm=final_norm),
        grid=(T // tm,),
        in_specs=[row_blk(D), row_blk(A_WIDTH), row_blk(B_WIDTH)] + z_specs
        + [whole(bias), whole(wa), whole(wb), whole(wo), whole(final_gain)],
        out_specs=row_blk(D),
        out_shape=jax.ShapeDtypeStruct((T, D), F32),
        scratch_shapes=[pltpu.VMEM((tm, D), BF16)],
        compiler_params=pltpu.CompilerParams(
            dimension_semantics=("parallel",),
            vmem_limit_bytes=V7X_VMEM_LIMIT_BYTES),
        name="merge_output",
    )(x2, ya, yb, *([proj2] * (2 * n_z)), bias, wa, wb, wo, final_gain)


def _angles(pos, dim, theta):
    expo = np.arange(0, dim, 2, dtype=np.float64) / dim
    return pos.astype(np.float64)[:, None] / np.power(float(theta), expo)[None, :]


def _lane_tables(ang_blocks):
    cos, sin = [], []
    for blk in ang_blocks:
        if isinstance(blk, int):
            S = cos[0].shape[0]
            cos.append(np.ones((S, blk)))
            sin.append(np.zeros((S, blk)))
            continue
        c, s = np.cos(blk), np.sin(blk)
        cos += [c, c]
        sin += [-s, s]
    return tuple(jnp.asarray(np.concatenate(t, axis=-1), dtype=F32) for t in (cos, sin))


def _axial_tables(S):
    pos = np.arange(S)
    half = HEAD_DIM // 2
    return _lane_tables([_angles(pos // GRID_W, half, AXIAL_THETA),
                         _angles(pos % GRID_W, half, AXIAL_THETA)])


def _partial_tables(S):
    pos = np.arange(S)
    return _lane_tables([_angles(pos, PARTIAL_ROPE_DIM, ROPE_THETA),
                         HEAD_DIM - PARTIAL_ROPE_DIM])


def kernel(x, norm_gain, w_in, q_norm_gain, k_norm_gain, merge_gate_bias, w_branch_a,
           w_branch_b, w_out, final_norm_gain):
    B, S, D = x.shape
    depth = norm_gain.shape[0]
    n_cols = _OFF_ZA + 2 * D
    assert w_in.shape[2] == n_cols and S % GRID_W == 0
    assert all(w // (2 * d) == BAND_HALF for w, d in B_PATTERNS)
    axial = _axial_tables(S)
    partial = _partial_tables(S)
    x2 = x.reshape(B * S, D)
    for l in range(depth):
        proj2 = _input_projection(x2, norm_gain[l][None, :], w_in[l])
        proj3 = proj2.reshape(B, S, n_cols)
        ya = _attention_a(proj3, q_norm_gain[l][None, :], k_norm_gain[l][None, :], axial)
        yb = _band_mix(proj3, partial)
        x2 = _merge(x2, ya.reshape(B * S, A_WIDTH), yb.reshape(B * S, B_WIDTH), proj2,
                    merge_gate_bias[l], w_branch_a[l].astype(BF16),
                    w_branch_b[l].astype(BF16), w_out[l].astype(BF16),
                    final_norm_gain[None, :], final_norm=(l == depth - 1))
    return x2.reshape(B, S, D)
```

```python
import functools
import math

import numpy as np
import jax
import jax.numpy as jnp
from jax import lax
from jax.experimental import pallas as pl
from jax.experimental.pallas import tpu as pltpu

F32 = jnp.float32
BF16 = jnp.bfloat16

HEAD_DIM = 128
GRID_W = 64
NORM_EPS = 1e-6
A_Q_HEADS = 8
A_KV_HEADS = 2
A_GROUP = A_Q_HEADS // A_KV_HEADS
A_WIDTH = A_Q_HEADS * HEAD_DIM
A_KV_WIDTH = A_KV_HEADS * HEAD_DIM
AXIAL_THETA = 10000.0
AXIAL_SHIFT = HEAD_DIM // 4
B_PATTERNS = ((128, 1), (512, 4), (2048, 16))
B_GROUPS = len(B_PATTERNS)
B_HEADS = 4
B_WIDTH = B_HEADS * HEAD_DIM
PARTIAL_ROPE_DIM = HEAD_DIM // 4
PARTIAL_SHIFT = PARTIAL_ROPE_DIM // 2
ROPE_THETA = 500000.0
BAND_HALF = 64
BAND_TQ = 256
BAND_WIN = BAND_TQ + 2 * BAND_HALF
SPLIT = 4
MASK_VALUE = -1e30
LOG2_E = math.log2(math.e)
LN_2 = math.log(2.0)
Q_SCALE = HEAD_DIM ** -0.5 * LOG2_E

COL_BLK = 512
_OFF_QA = 0
_OFF_KA = A_WIDTH
_OFF_VA = _OFF_KA + A_KV_WIDTH
_OFF_GA = _OFF_VA + A_KV_WIDTH
_OFF_QB = _OFF_GA + A_WIDTH
_OFF_KB = _OFF_QB + B_GROUPS * B_WIDTH
_OFF_VB = _OFF_KB + B_GROUPS * B_WIDTH
_OFF_GB = _OFF_VB + B_GROUPS * B_WIDTH
_OFF_ZA = _OFF_GB + B_WIDTH

V7X_VMEM_LIMIT_BYTES = 56 * 1024 * 1024


def _rotary(x, cos, sin, shift):
    lane = lax.broadcasted_iota(jnp.int32, x.shape, 1)
    up = pltpu.roll(x, HEAD_DIM - shift, 1)
    dn = pltpu.roll(x, shift, 1)
    partner = jnp.where(lane % (2 * shift) < shift, up, dn)
    return x * cos + partner * sin


def _rms(x, gain):
    ms = jnp.mean(x * x, axis=-1, keepdims=True)
    return x * lax.rsqrt(ms + NORM_EPS) * gain


def _silu(g):
    return g * jax.nn.sigmoid(g)


def _proj_kernel(x_ref, g_ref, w_ref, o_ref, h_ref, wb_ref):
    j, r = pl.program_id(1), pl.program_id(2)

    @pl.when(j == 0)
    def _():
        h_ref[r] = _rms(x_ref[...], g_ref[...]).astype(BF16)

    @pl.when(r == 0)
    def _():
        w = w_ref[...].astype(BF16)
        wb_ref[...] = w
        o_ref[...] = jnp.dot(h_ref[0], w, preferred_element_type=F32).astype(BF16)

    @pl.when(r != 0)
    def _():
        o_ref[...] = jnp.dot(h_ref[r], wb_ref[...], preferred_element_type=F32).astype(BF16)


def _input_projection(x2, gain, w, tm=1024, tn=COL_BLK, rows_per_group=2):
    T, D = x2.shape
    N = w.shape[1]
    R = rows_per_group

    def x_index(g, j, r):
        return g * R + jnp.where(j == 0, r, R - 1), 0

    return pl.pallas_call(
        _proj_kernel,
        grid=(T // (tm * R), N // tn, R),
        in_specs=[
            pl.BlockSpec((tm, D), x_index),
            pl.BlockSpec((1, D), lambda g, j, r: (0, 0)),
            pl.BlockSpec((D, tn), lambda g, j, r: (0, j)),
        ],
        out_specs=pl.BlockSpec((tm, tn), lambda g, j, r: (g * R + r, j)),
        out_shape=jax.ShapeDtypeStruct((T, N), BF16),
        scratch_shapes=[pltpu.VMEM((R, tm, D), BF16), pltpu.VMEM((D, tn), BF16)],
        compiler_params=pltpu.CompilerParams(
            dimension_semantics=("parallel", "arbitrary", "arbitrary"),
            vmem_limit_bytes=V7X_VMEM_LIMIT_BYTES),
        name="input_projection",
    )(x2, gain, w)


def _attn_a_kernel(q_ref, k_ref, v_ref, ga_ref, gq_ref, gk_ref, cos_ref, sin_ref,
                   o_ref, kn_ref, *, tq):
    qi = pl.program_id(2)

    @pl.when(qi == 0)
    def _():
        k = _rms(k_ref[0].astype(F32), gk_ref[...])
        kn_ref[...] = _rotary(k, cos_ref[...], sin_ref[...], AXIAL_SHIFT).astype(BF16)

    rows = pl.ds(pl.multiple_of(qi * tq, tq), tq)
    cos, sin = cos_ref[rows, :], sin_ref[rows, :]
    kn = kn_ref[...]
    v = v_ref[0]
    for h in range(A_GROUP):
        cols = slice(h * HEAD_DIM, (h + 1) * HEAD_DIM)
        q = _rms(q_ref[0, :, cols].astype(F32), gq_ref[...])
        q = (_rotary(q, cos, sin, AXIAL_SHIFT) * Q_SCALE).astype(BF16)
        s = lax.dot_general(q, kn, (((1,), (1,)), ((), ())), preferred_element_type=F32)
        m = jnp.max(s, axis=-1, keepdims=True)
        p = jnp.exp2(s - m)
        l = jnp.sum(p, axis=-1, keepdims=True)
        o = jnp.dot(p.astype(BF16), v, preferred_element_type=F32)
        g = ga_ref[0, :, cols].astype(F32)
        o_ref[0, :, cols] = (o * (1.0 / l) * _silu(g)).astype(BF16)


def _attention_a(proj3, q_gain, k_gain, tables, tq=256):
    B, S, _ = proj3.shape
    gw = A_GROUP * HEAD_DIM
    ka_blk = _OFF_KA // HEAD_DIM
    va_blk = _OFF_VA // HEAD_DIM
    ga_blk = _OFF_GA // gw
    full = pl.BlockSpec((S, HEAD_DIM), lambda b, g, i: (0, 0))
    vec = pl.BlockSpec((1, HEAD_DIM), lambda b, g, i: (0, 0))
    return pl.pallas_call(
        functools.partial(_attn_a_kernel, tq=tq),
        grid=(B, A_KV_HEADS, S // tq),
        in_specs=[
            pl.BlockSpec((1, tq, gw), lambda b, g, i: (b, i, g)),
            pl.BlockSpec((1, S, HEAD_DIM), lambda b, g, i: (b, 0, ka_blk + g)),
            pl.BlockSpec((1, S, HEAD_DIM), lambda b, g, i: (b, 0, va_blk + g)),
            pl.BlockSpec((1, tq, gw), lambda b, g, i: (b, i, ga_blk + g)),
            vec, vec, full, full,
        ],
        out_specs=pl.BlockSpec((1, tq, gw), lambda b, g, i: (b, i, g)),
        out_shape=jax.ShapeDtypeStruct((B, S, A_WIDTH), BF16),
        scratch_shapes=[pltpu.VMEM((S, HEAD_DIM), BF16)],
        compiler_params=pltpu.CompilerParams(
            dimension_semantics=("parallel", "parallel", "arbitrary"),
            vmem_limit_bytes=V7X_VMEM_LIMIT_BYTES),
        name="attention_a",
    )(proj3, proj3, proj3, proj3, q_gain, k_gain, *tables)


def _band_tile(q, kw, vw, valid):
    s = lax.dot_general(q, kw, (((1,), (1,)), ((), ())), preferred_element_type=F32)
    s = jnp.where(valid, s, MASK_VALUE)
    m = jnp.max(s, axis=-1, keepdims=True)
    p = jnp.exp2(s - m)
    l = jnp.sum(p, axis=-1, keepdims=True)
    o = jnp.dot(p.astype(BF16), vw, preferred_element_type=F32)
    lse = m * LN_2 + jnp.log(l)
    return o * (1.0 / l), jnp.broadcast_to(lse, (BAND_TQ, HEAD_DIM))


def _window_mask(t, n_tiles):
    qi = lax.broadcasted_iota(jnp.int32, (BAND_TQ, BAND_WIN), 0)
    kj = lax.broadcasted_iota(jnp.int32, (BAND_TQ, BAND_WIN), 1)
    kpos = kj + (t * BAND_TQ - BAND_HALF)
    return (kj >= qi) & (kj <= qi + 2 * BAND_HALF) & (kpos >= 0) & (kpos < n_tiles * BAND_TQ)


def _zero_pads(ref, lead, n_rows):
    zeros = jnp.zeros((BAND_HALF, HEAD_DIM), BF16)
    for idx in lead:
        ref[idx + (slice(0, BAND_HALF),)] = zeros
        ref[idx + (slice(BAND_HALF + n_rows, 2 * BAND_HALF + n_rows),)] = zeros


def _band_mix_kernel(q1_ref, k1_ref, v1_ref, q4_ref, k4_ref, v4_ref, q16_ref, k16_ref, v16_ref,
                     gb_ref, cos_ref, sin_ref, y_ref,
                     nat_ref, tmp_ref, qd1_ref, kp1_ref, vp1_ref, qd4_ref, kp4_ref, vp4_ref,
                     qd16_ref, kd16_ref, vd16_ref, od16_ref, og_ref, lg_ref, *, S):
    L4 = S // SPLIT
    L16 = L4 // SPLIT
    n1, n4 = S // BAND_TQ, L4 // BAND_TQ
    per_tile = BAND_TQ // L16

    def rotated(src_ref, scale=None):
        y = _rotary(src_ref[0].astype(F32), cos_ref[...], sin_ref[...], PARTIAL_SHIFT)
        return y if scale is None else y * scale

    _zero_pads(kp1_ref, [()], S)
    _zero_pads(vp1_ref, [()], S)
    qd1_ref[...] = rotated(q1_ref, Q_SCALE).astype(BF16)
    kp1_ref[BAND_HALF:BAND_HALF + S, :] = rotated(k1_ref).astype(BF16)
    vp1_ref[BAND_HALF:BAND_HALF + S, :] = v1_ref[0]

    for t in range(n1):
        rows = slice(t * BAND_TQ, (t + 1) * BAND_TQ)
        win = slice(t * BAND_TQ, t * BAND_TQ + BAND_WIN)
        o, lse = _band_tile(qd1_ref[rows, :], kp1_ref[win, :], vp1_ref[win, :],
                            _window_mask(t, n1))
        og_ref[0, rows, :] = o
        lg_ref[0, rows, :] = lse

    _zero_pads(kp4_ref, [(r,) for r in range(SPLIT)], L4)
    _zero_pads(vp4_ref, [(r,) for r in range(SPLIT)], L4)
    for n, (val, dst_ref, lead) in enumerate(((rotated(q4_ref, Q_SCALE), qd4_ref, 0),
                                              (rotated(k4_ref), kp4_ref, BAND_HALF),
                                              (v4_ref[0].astype(F32), vp4_ref, BAND_HALF))):
        nat_ref[n] = val
        for r in range(SPLIT):
            dst_ref[r, lead:lead + L4, :] = nat_ref[n, pl.ds(r, L4, stride=SPLIT), :].astype(BF16)

    for r in range(SPLIT):
        for t in range(n4):
            rows = pl.ds(r + t * (BAND_TQ * SPLIT), BAND_TQ, stride=SPLIT)
            win = slice(t * BAND_TQ, t * BAND_TQ + BAND_WIN)
            o, lse = _band_tile(qd4_ref[r, t * BAND_TQ:(t + 1) * BAND_TQ, :],
                                kp4_ref[r, win, :], vp4_ref[r, win, :], _window_mask(t, n4))
            og_ref[1, rows, :] = o
            lg_ref[1, rows, :] = lse

    for n, (val, dst_ref) in enumerate(((rotated(q16_ref, Q_SCALE), qd16_ref),
                                        (rotated(k16_ref), kd16_ref),
                                        (v16_ref[0].astype(F32), vd16_ref))):
        nat_ref[3 + n] = val
        for a in range(SPLIT):
            tmp_ref[n, a] = nat_ref[3 + n, pl.ds(a, L4, stride=SPLIT), :]
            for b in range(SPLIT):
                r = a + SPLIT * b
                dst_ref[r // per_tile, (r % per_tile) * L16:(r % per_tile + 1) * L16, :] = \
                    tmp_ref[n, a, pl.ds(b, L16, stride=SPLIT), :].astype(BF16)

    qi = lax.broadcasted_iota(jnp.int32, (BAND_TQ, BAND_TQ), 0)
    kj = lax.broadcasted_iota(jnp.int32, (BAND_TQ, BAND_TQ), 1)
    same_residue = functools.reduce(
        jnp.logical_or, [(qi >= c * L16) & (qi < (c + 1) * L16) & (kj >= c * L16)
                         & (kj < (c + 1) * L16) for c in range(per_tile)])
    stacked_band = same_residue & (kj >= qi - BAND_HALF) & (kj <= qi + BAND_HALF)
    for j in range(SPLIT * SPLIT // per_tile):
        o, lse = _band_tile(qd16_ref[j], kd16_ref[j], vd16_ref[j], stacked_band)
        od16_ref[0, j] = o
        od16_ref[1, j] = lse

    for n, dst_ref in enumerate((og_ref, lg_ref)):
        for a in range(SPLIT):
            for b in range(SPLIT):
                r = a + SPLIT * b
                tmp_ref[n, a, pl.ds(b, L16, stride=SPLIT), :] = \
                    od16_ref[n, r // per_tile, (r % per_tile) * L16:(r % per_tile + 1) * L16, :]
            dst_ref[2, pl.ds(a, L4, stride=SPLIT), :] = tmp_ref[n, a]

    chunk = BAND_TQ

    def mix(i, carry):
        rows = pl.ds(pl.multiple_of(i * chunk, chunk), chunk)
        lses = [lg_ref[g, rows, :] for g in range(B_GROUPS)]
        lmax = functools.reduce(jnp.maximum, lses)
        es = [jnp.exp(l - lmax) for l in lses]
        num = sum(e * og_ref[g, rows, :] for g, e in enumerate(es))
        ob = num * (1.0 / sum(es))
        y_ref[0, rows, :] = (ob * _silu(gb_ref[0, rows, :].astype(F32))).astype(BF16)
        return carry

    lax.fori_loop(0, S // chunk, mix, 0)


def _band_mix(proj3, tables):
    B, S, NC = proj3.shape
    assert tuple(d for _, d in B_PATTERNS) == (1, SPLIT, SPLIT * SPLIT)
    L4, L16 = S // SPLIT, S // (SPLIT * SPLIT)
    assert L4 % BAND_TQ == 0 and BAND_TQ % L16 == 0
    n16 = S // BAND_TQ

    def head_blk(off, g):
        first = off // HEAD_DIM + g * B_HEADS
        return pl.BlockSpec((1, S, HEAD_DIM), lambda b, h: (b, 0, first + h))

    qkv_specs = [head_blk(off, g) for g in range(B_GROUPS)
                 for off in (_OFF_QB, _OFF_KB, _OFF_VB)]
    gb_first = _OFF_GB // HEAD_DIM
    table = pl.BlockSpec((S, HEAD_DIM), lambda b, h: (0, 0))
    return pl.pallas_call(
        functools.partial(_band_mix_kernel, S=S),
        grid=(B, B_HEADS),
        in_specs=qkv_specs
        + [pl.BlockSpec((1, S, HEAD_DIM), lambda b, h: (b, 0, gb_first + h)), table, table],
        out_specs=pl.BlockSpec((1, S, HEAD_DIM), lambda b, h: (b, 0, h)),
        out_shape=jax.ShapeDtypeStruct((B, S, B_WIDTH), BF16),
        scratch_shapes=[
            pltpu.VMEM((6, S, HEAD_DIM), F32),
            pltpu.VMEM((3, SPLIT, L4, HEAD_DIM), F32),
            pltpu.VMEM((S, HEAD_DIM), BF16),
            pltpu.VMEM((S + 2 * BAND_HALF, HEAD_DIM), BF16),
            pltpu.VMEM((S + 2 * BAND_HALF, HEAD_DIM), BF16),
            pltpu.VMEM((SPLIT, L4, HEAD_DIM), BF16),
            pltpu.VMEM((SPLIT, L4 + 2 * BAND_HALF, HEAD_DIM), BF16),
            pltpu.VMEM((SPLIT, L4 + 2 * BAND_HALF, HEAD_DIM), BF16),
            pltpu.VMEM((n16, BAND_TQ, HEAD_DIM), BF16),
            pltpu.VMEM((n16, BAND_TQ, HEAD_DIM), BF16),
            pltpu.VMEM((n16, BAND_TQ, HEAD_DIM), BF16),
            pltpu.VMEM((2, n16, BAND_TQ, HEAD_DIM), F32),
            pltpu.VMEM((B_GROUPS, S, HEAD_DIM), F32),
            pltpu.VMEM((B_GROUPS, S, HEAD_DIM), F32),
        ],
        compiler_params=pltpu.CompilerParams(
            dimension_semantics=("parallel", "parallel"),
            vmem_limit_bytes=V7X_VMEM_LIMIT_BYTES),
        name="band_mix",
    )(*([proj3] * (3 * B_GROUPS + 1)), *tables)


def _merge_kernel(x_ref, ya_ref, yb_ref, za0, za1, za2, za3, zb0, zb1, zb2, zb3, bias_ref,
                  wa_ref, wb_ref, wo_ref, fg_ref, out_ref, merged_ref, *, final_norm):
    pa = jnp.dot(ya_ref[...], wa_ref[...], preferred_element_type=F32)
    pb = jnp.dot(yb_ref[...], wb_ref[...], preferred_element_type=F32)
    for c, (za, zb) in enumerate(((za0, zb0), (za1, zb1), (za2, zb2), (za3, zb3))):
        cols = slice(c * COL_BLK, (c + 1) * COL_BLK)
        gate_a = jax.nn.sigmoid(za[...].astype(F32) + bias_ref[0:1, cols])
        gate_b = jax.nn.sigmoid(zb[...].astype(F32) + bias_ref[1:2, cols])
        merged_ref[:, cols] = (gate_a * pa[:, cols] + gate_b * pb[:, cols]).astype(BF16)

    y = x_ref[...] + jnp.dot(merged_ref[...], wo_ref[...], preferred_element_type=F32)
    out_ref[...] = _rms(y, fg_ref[...]) if final_norm else y


def _merge(x2, ya, yb, proj2, bias, wa, wb, wo, final_gain, final_norm, tm=256):
    T, D = x2.shape
    za_blk = _OFF_ZA // COL_BLK
    n_z = D // COL_BLK

    def row_blk(width, col=0):
        return pl.BlockSpec((tm, width), lambda i: (i, col))

    def whole(a):
        return pl.BlockSpec(a.shape, lambda i: (0, 0))

    z_specs = [row_blk(COL_BLK, za_blk + c) for c in range(2 * n_z)]
    return pl.pallas_call(
        functools.partial(_merge_kernel, final_norm=final_norm),
        grid=(T // tm,),
        in_specs=[row_blk(D), row_blk(A_WIDTH), row_blk(B_WIDTH)] + z_specs
        + [whole(bias), whole(wa), whole(wb), whole(wo), whole(final_gain)],
        out_specs=row_blk(D),
        out_shape=jax.ShapeDtypeStruct((T, D), F32),
        scratch_shapes=[pltpu.VMEM((tm, D), BF16)],
        compiler_params=pltpu.CompilerParams(
            dimension_semantics=("parallel",),
            vmem_limit_bytes=V7X_VMEM_LIMIT_BYTES),
        name="merge_output",
    )(x2, ya, yb, *([proj2] * (2 * n_z)), bias, wa, wb, wo, final_gain)


def _angles(pos, dim, theta):
    expo = np.arange(0, dim, 2, dtype=np.float64) / dim
    return pos.astype(np.float64)[:, None] / np.power(float(theta), expo)[None, :]


def _lane_tables(ang_blocks):
    cos, sin = [], []
    for blk in ang_blocks:
        if isinstance(blk, int):
            S = cos[0].shape[0]
            cos.append(np.ones((S, blk)))
            sin.append(np.zeros((S, blk)))
            continue
        c, s = np.cos(blk), np.sin(blk)
        cos += [c, c]
        sin += [-s, s]
    return tuple(jnp.asarray(np.concatenate(t, axis=-1), dtype=F32) for t in (cos, sin))


def _axial_tables(S):
    pos = np.arange(S)
    half = HEAD_DIM // 2
    return _lane_tables([_angles(pos // GRID_W, half, AXIAL_THETA),
                         _angles(pos % GRID_W, half, AXIAL_THETA)])


def _partial_tables(S):
    pos = np.arange(S)
    return _lane_tables([_angles(pos, PARTIAL_ROPE_DIM, ROPE_THETA),
                         HEAD_DIM - PARTIAL_ROPE_DIM])


def kernel(x, norm_gain, w_in, q_norm_gain, k_norm_gain, merge_gate_bias, w_branch_a,
           w_branch_b, w_out, final_norm_gain):
    B, S, D = x.shape
    depth = norm_gain.shape[0]
    n_cols = _OFF_ZA + 2 * D
    assert w_in.shape[2] == n_cols and S % GRID_W == 0
    assert all(w // (2 * d) == BAND_HALF for w, d in B_PATTERNS)
    axial = _axial_tables(S)
    partial = _partial_tables(S)
    x2 = x.reshape(B * S, D)
    for l in range(depth):
        proj2 = _input_projection(x2, norm_gain[l][None, :], w_in[l])
        proj3 = proj2.reshape(B, S, n_cols)
        ya = _attention_a(proj3, q_norm_gain[l][None, :], k_norm_gain[l][None, :], axial)
        yb = _band_mix(proj3, partial)
        x2 = _merge(x2, ya.reshape(B * S, A_WIDTH), yb.reshape(B * S, B_WIDTH), proj2,
                    merge_gate_bias[l], w_branch_a[l].astype(BF16),
                    w_branch_b[l].astype(BF16), w_out[l].astype(BF16),
                    final_norm_gain[None, :], final_norm=(l == depth - 1))
    return x2.reshape(B, S, D)
```

```python
import functools
import math

import numpy as np
import jax
import jax.numpy as jnp
from jax import lax
from jax.experimental import pallas as pl
from jax.experimental.pallas import tpu as pltpu

F32 = jnp.float32
BF16 = jnp.bfloat16

HEAD_DIM = 128
GRID_W = 64
NORM_EPS = 1e-6
A_Q_HEADS = 8
A_KV_HEADS = 2
A_GROUP = A_Q_HEADS // A_KV_HEADS
A_WIDTH = A_Q_HEADS * HEAD_DIM
A_KV_WIDTH = A_KV_HEADS * HEAD_DIM
AXIAL_THETA = 10000.0
AXIAL_SHIFT = HEAD_DIM // 4
MAX_SAFE_SCORE_BOUND = 50.0
SCORE_BOUND_SLACK = 1.01
B_PATTERNS = ((128, 1), (512, 4), (2048, 16))
B_GROUPS = len(B_PATTERNS)
B_HEADS = 4
B_WIDTH = B_HEADS * HEAD_DIM
PARTIAL_ROPE_DIM = HEAD_DIM // 4
PARTIAL_SHIFT = PARTIAL_ROPE_DIM // 2
ROPE_THETA = 500000.0
BAND_HALF = 64
BAND_TQ = 256
BAND_WIN = BAND_TQ + 2 * BAND_HALF
SPLIT = 4
MASK_VALUE = -1e30
LOG2_E = math.log2(math.e)
LN_2 = math.log(2.0)
Q_SCALE = HEAD_DIM ** -0.5 * LOG2_E

COL_BLK = 512
_OFF_QA = 0
_OFF_KA = A_WIDTH
_OFF_VA = _OFF_KA + A_KV_WIDTH
_OFF_GA = _OFF_VA + A_KV_WIDTH
_OFF_QB = _OFF_GA + A_WIDTH
_OFF_KB = _OFF_QB + B_GROUPS * B_WIDTH
_OFF_VB = _OFF_KB + B_GROUPS * B_WIDTH
_OFF_GB = _OFF_VB + B_GROUPS * B_WIDTH
_OFF_ZA = _OFF_GB + B_WIDTH

V7X_VMEM_LIMIT_BYTES = 56 * 1024 * 1024


def _rotary(x, cos, sin, shift):
    lane = lax.broadcasted_iota(jnp.int32, x.shape, 1)
    up = pltpu.roll(x, HEAD_DIM - shift, 1)
    dn = pltpu.roll(x, shift, 1)
    partner = jnp.where(lane % (2 * shift) < shift, up, dn)
    return x * cos + partner * sin


def _rms(x, gain):
    ms = jnp.mean(x * x, axis=-1, keepdims=True)
    return x * lax.rsqrt(ms + NORM_EPS) * gain


def _silu(g):
    return g * jax.nn.sigmoid(g)


def _proj_kernel(x_ref, g_ref, w_ref, o_ref, h_ref, wb_ref):
    j, r = pl.program_id(1), pl.program_id(2)

    @pl.when(j == 0)
    def _():
        h_ref[r] = _rms(x_ref[...], g_ref[...]).astype(BF16)

    @pl.when(r == 0)
    def _():
        w = w_ref[...].astype(BF16)
        wb_ref[...] = w
        o_ref[...] = jnp.dot(h_ref[0], w, preferred_element_type=F32).astype(BF16)

    @pl.when(r != 0)
    def _():
        o_ref[...] = jnp.dot(h_ref[r], wb_ref[...], preferred_element_type=F32).astype(BF16)


def _input_projection(x2, gain, w, tm=1024, tn=COL_BLK, rows_per_group=2):
    T, D = x2.shape
    N = w.shape[1]
    R = rows_per_group

    def x_index(g, j, r):
        return g * R + jnp.where(j == 0, r, R - 1), 0

    return pl.pallas_call(
        _proj_kernel,
        grid=(T // (tm * R), N // tn, R),
        in_specs=[
            pl.BlockSpec((tm, D), x_index),
            pl.BlockSpec((1, D), lambda g, j, r: (0, 0)),
            pl.BlockSpec((D, tn), lambda g, j, r: (0, j)),
        ],
        out_specs=pl.BlockSpec((tm, tn), lambda g, j, r: (g * R + r, j)),
        out_shape=jax.ShapeDtypeStruct((T, N), BF16),
        scratch_shapes=[pltpu.VMEM((R, tm, D), BF16), pltpu.VMEM((D, tn), BF16)],
        compiler_params=pltpu.CompilerParams(
            dimension_semantics=("parallel", "arbitrary", "arbitrary"),
            vmem_limit_bytes=V7X_VMEM_LIMIT_BYTES),
        name="input_projection",
    )(x2, gain, w)


def _attn_a_kernel(q_ref, qnext_ref, k_ref, v_ref, ga_ref, gq_ref, gk_ref, cos_ref, sin_ref,
                   o_ref, kn_ref, k2max_ref, qs_ref, q2max_ref, *, tq, n_q):
    qi = pl.program_id(2)
    slot = qi % 2

    def max_sq_norm(x_bf16):
        xf = x_bf16.astype(F32)
        return jnp.max(jnp.sum(xf * xf, axis=-1, keepdims=True), axis=0, keepdims=True)

    def prepare_queries(src_ref, tile, dst):
        rows = pl.ds(pl.multiple_of(tile * tq, tq), tq)
        cos, sin = cos_ref[rows, :], sin_ref[rows, :]
        q2max = None
        for h in range(A_GROUP):
            cols = slice(h * HEAD_DIM, (h + 1) * HEAD_DIM)
            q = _rms(src_ref[0, :, cols].astype(F32), gq_ref[...])
            q = (_rotary(q, cos, sin, AXIAL_SHIFT) * Q_SCALE).astype(BF16)
            qs_ref[dst, :, cols] = q
            q2max = max_sq_norm(q) if q2max is None else jnp.maximum(q2max, max_sq_norm(q))
        q2max_ref[dst] = jnp.broadcast_to(q2max, q2max_ref.shape[1:])

    @pl.when(qi == 0)
    def _():
        k = _rms(k_ref[0].astype(F32), gk_ref[...])
        k = _rotary(k, cos_ref[...], sin_ref[...], AXIAL_SHIFT).astype(BF16)
        kn_ref[...] = k
        k2max_ref[...] = jnp.broadcast_to(max_sq_norm(k), k2max_ref.shape)
        prepare_queries(q_ref, 0, 0)

    bound = jnp.sqrt(q2max_ref[slot, 0:1, 0:1] * k2max_ref[0:1, 0:1]) * SCORE_BOUND_SLACK
    bound_is_safe = bound[0, 0] <= MAX_SAFE_SCORE_BOUND

    def attend(shift_of):
        prepare_queries(qnext_ref, jnp.minimum(qi + 1, n_q - 1), 1 - slot)
        kn = kn_ref[...]
        v = v_ref[0]
        for h in range(A_GROUP):
            cols = slice(h * HEAD_DIM, (h + 1) * HEAD_DIM)
            s = lax.dot_general(qs_ref[slot, :, cols], kn, (((1,), (1,)), ((), ())),
                                preferred_element_type=F32)
            p = jnp.exp2(s - shift_of(s))
            l = jnp.sum(p, axis=-1, keepdims=True)
            o = jnp.dot(p.astype(BF16), v, preferred_element_type=F32)
            g = ga_ref[0, :, cols].astype(F32)
            o_ref[0, :, cols] = (o * (1.0 / l) * _silu(g)).astype(BF16)

    @pl.when(bound_is_safe)
    def _():
        attend(lambda s: bound)

    @pl.when(jnp.logical_not(bound_is_safe))
    def _():
        attend(lambda s: jnp.max(s, axis=-1, keepdims=True))


def _attention_a(proj3, q_gain, k_gain, tables, tq=256):
    B, S, _ = proj3.shape
    n_q = S // tq
    gw = A_GROUP * HEAD_DIM
    ka_blk = _OFF_KA // HEAD_DIM
    va_blk = _OFF_VA // HEAD_DIM
    ga_blk = _OFF_GA // gw
    full = pl.BlockSpec((S, HEAD_DIM), lambda b, g, i: (0, 0))
    vec = pl.BlockSpec((1, HEAD_DIM), lambda b, g, i: (0, 0))
    return pl.pallas_call(
        functools.partial(_attn_a_kernel, tq=tq, n_q=n_q),
        grid=(B, A_KV_HEADS, n_q),
        in_specs=[
            pl.BlockSpec((1, tq, gw), lambda b, g, i: (b, i, g)),
            pl.BlockSpec((1, tq, gw), lambda b, g, i: (b, jnp.minimum(i + 1, n_q - 1), g)),
            pl.BlockSpec((1, S, HEAD_DIM), lambda b, g, i: (b, 0, ka_blk + g)),
            pl.BlockSpec((1, S, HEAD_DIM), lambda b, g, i: (b, 0, va_blk + g)),
            pl.BlockSpec((1, tq, gw), lambda b, g, i: (b, i, ga_blk + g)),
            vec, vec, full, full,
        ],
        out_specs=pl.BlockSpec((1, tq, gw), lambda b, g, i: (b, i, g)),
        out_shape=jax.ShapeDtypeStruct((B, S, A_WIDTH), BF16),
        scratch_shapes=[
            pltpu.VMEM((S, HEAD_DIM), BF16),
            pltpu.VMEM((8, HEAD_DIM), F32),
            pltpu.VMEM((2, tq, gw), BF16),
            pltpu.VMEM((2, 8, HEAD_DIM), F32),
        ],
        compiler_params=pltpu.CompilerParams(
            dimension_semantics=("parallel", "parallel", "arbitrary"),
            vmem_limit_bytes=V7X_VMEM_LIMIT_BYTES),
        name="attention_a",
    )(proj3, proj3, proj3, proj3, proj3, q_gain, k_gain, *tables)


def _band_tile(q, kw, vw, valid):
    s = lax.dot_general(q, kw, (((1,), (1,)), ((), ())), preferred_element_type=F32)
    s = jnp.where(valid, s, MASK_VALUE)
    m = jnp.max(s, axis=-1, keepdims=True)
    p = jnp.exp2(s - m)
    l = jnp.sum(p, axis=-1, keepdims=True)
    o = jnp.dot(p.astype(BF16), vw, preferred_element_type=F32)
    lse = m * LN_2 + jnp.log(l)
    return o * (1.0 / l), jnp.broadcast_to(lse, (BAND_TQ, HEAD_DIM))


def _window_mask(t, n_tiles):
    qi = lax.broadcasted_iota(jnp.int32, (BAND_TQ, BAND_WIN), 0)
    kj = lax.broadcasted_iota(jnp.int32, (BAND_TQ, BAND_WIN), 1)
    kpos = kj + (t * BAND_TQ - BAND_HALF)
    return (kj >= qi) & (kj <= qi + 2 * BAND_HALF) & (kpos >= 0) & (kpos < n_tiles * BAND_TQ)


def _zero_pads(ref, lead, n_rows):
    zeros = jnp.zeros((BAND_HALF, HEAD_DIM), BF16)
    for idx in lead:
        ref[idx + (slice(0, BAND_HALF),)] = zeros
        ref[idx + (slice(BAND_HALF + n_rows, 2 * BAND_HALF + n_rows),)] = zeros


def _band_mix_kernel(q1_ref, k1_ref, v1_ref, q4_ref, k4_ref, v4_ref, q16_ref, k16_ref, v16_ref,
                     gb_ref, cos_ref, sin_ref, y_ref,
                     nat_ref, tmp_ref, qd1_ref, kp1_ref, vp1_ref, qd4_ref, kp4_ref, vp4_ref,
                     qd16_ref, kd16_ref, vd16_ref, od16_ref, og_ref, lg_ref, *, S):
    L4 = S // SPLIT
    L16 = L4 // SPLIT
    n1, n4 = S // BAND_TQ, L4 // BAND_TQ
    per_tile = BAND_TQ // L16

    def rotated(src_ref, scale=None):
        y = _rotary(src_ref[0].astype(F32), cos_ref[...], sin_ref[...], PARTIAL_SHIFT)
        return y if scale is None else y * scale

    _zero_pads(kp1_ref, [()], S)
    _zero_pads(vp1_ref, [()], S)
    qd1_ref[...] = rotated(q1_ref, Q_SCALE).astype(BF16)
    kp1_ref[BAND_HALF:BAND_HALF + S, :] = rotated(k1_ref).astype(BF16)
    vp1_ref[BAND_HALF:BAND_HALF + S, :] = v1_ref[0]

    for t in range(n1):
        rows = slice(t * BAND_TQ, (t + 1) * BAND_TQ)
        win = slice(t * BAND_TQ, t * BAND_TQ + BAND_WIN)
        o, lse = _band_tile(qd1_ref[rows, :], kp1_ref[win, :], vp1_ref[win, :],
                            _window_mask(t, n1))
        og_ref[0, rows, :] = o
        lg_ref[0, rows, :] = lse

    _zero_pads(kp4_ref, [(r,) for r in range(SPLIT)], L4)
    _zero_pads(vp4_ref, [(r,) for r in range(SPLIT)], L4)
    for n, (val, dst_ref, lead) in enumerate(((rotated(q4_ref, Q_SCALE), qd4_ref, 0),
                                              (rotated(k4_ref), kp4_ref, BAND_HALF),
                                              (v4_ref[0].astype(F32), vp4_ref, BAND_HALF))):
        nat_ref[n] = val
        for r in range(SPLIT):
            dst_ref[r, lead:lead + L4, :] = nat_ref[n, pl.ds(r, L4, stride=SPLIT), :].astype(BF16)

    for r in range(SPLIT):
        for t in range(n4):
            rows = pl.ds(r + t * (BAND_TQ * SPLIT), BAND_TQ, stride=SPLIT)
            win = slice(t * BAND_TQ, t * BAND_TQ + BAND_WIN)
            o, lse = _band_tile(qd4_ref[r, t * BAND_TQ:(t + 1) * BAND_TQ, :],
                                kp4_ref[r, win, :], vp4_ref[r, win, :], _window_mask(t, n4))
            og_ref[1, rows, :] = o
            lg_ref[1, rows, :] = lse

    for n, (val, dst_ref) in enumerate(((rotated(q16_ref, Q_SCALE), qd16_ref),
                                        (rotated(k16_ref), kd16_ref),
                                        (v16_ref[0].astype(F32), vd16_ref))):
        nat_ref[3 + n] = val
        for a in range(SPLIT):
            tmp_ref[n, a] = nat_ref[3 + n, pl.ds(a, L4, stride=SPLIT), :]
            for b in range(SPLIT):
                r = a + SPLIT * b
                dst_ref[r // per_tile, (r % per_tile) * L16:(r % per_tile + 1) * L16, :] = \
                    tmp_ref[n, a, pl.ds(b, L16, stride=SPLIT), :].astype(BF16)

    qi = lax.broadcasted_iota(jnp.int32, (BAND_TQ, BAND_TQ), 0)
    kj = lax.broadcasted_iota(jnp.int32, (BAND_TQ, BAND_TQ), 1)
    same_residue = functools.reduce(
        jnp.logical_or, [(qi >= c * L16) & (qi < (c + 1) * L16) & (kj >= c * L16)
                         & (kj < (c + 1) * L16) for c in range(per_tile)])
    stacked_band = same_residue & (kj >= qi - BAND_HALF) & (kj <= qi + BAND_HALF)
    for j in range(SPLIT * SPLIT // per_tile):
        o, lse = _band_tile(qd16_ref[j], kd16_ref[j], vd16_ref[j], stacked_band)
        od16_ref[0, j] = o
        od16_ref[1, j] = lse

    for n, dst_ref in enumerate((og_ref, lg_ref)):
        for a in range(SPLIT):
            for b in range(SPLIT):
                r = a + SPLIT * b
                tmp_ref[n, a, pl.ds(b, L16, stride=SPLIT), :] = \
                    od16_ref[n, r // per_tile, (r % per_tile) * L16:(r % per_tile + 1) * L16, :]
            dst_ref[2, pl.ds(a, L4, stride=SPLIT), :] = tmp_ref[n, a]

    chunk = BAND_TQ

    def mix(i, carry):
        rows = pl.ds(pl.multiple_of(i * chunk, chunk), chunk)
        lses = [lg_ref[g, rows, :] for g in range(B_GROUPS)]
        lmax = functools.reduce(jnp.maximum, lses)
        es = [jnp.exp(l - lmax) for l in lses]
        num = sum(e * og_ref[g, rows, :] for g, e in enumerate(es))
        ob = num * (1.0 / sum(es))
        y_ref[0, rows, :] = (ob * _silu(gb_ref[0, rows, :].astype(F32))).astype(BF16)
        return carry

    lax.fori_loop(0, S // chunk, mix, 0)


def _band_mix(proj3, tables):
    B, S, NC = proj3.shape
    assert tuple(d for _, d in B_PATTERNS) == (1, SPLIT, SPLIT * SPLIT)
    L4, L16 = S // SPLIT, S // (SPLIT * SPLIT)
    assert L4 % BAND_TQ == 0 and BAND_TQ % L16 == 0
    n16 = S // BAND_TQ

    def head_blk(off, g):
        first = off // HEAD_DIM + g * B_HEADS
        return pl.BlockSpec((1, S, HEAD_DIM), lambda b, h: (b, 0, first + h))

    qkv_specs = [head_blk(off, g) for g in range(B_GROUPS)
                 for off in (_OFF_QB, _OFF_KB, _OFF_VB)]
    gb_first = _OFF_GB // HEAD_DIM
    table = pl.BlockSpec((S, HEAD_DIM), lambda b, h: (0, 0))
    return pl.pallas_call(
        functools.partial(_band_mix_kernel, S=S),
        grid=(B, B_HEADS),
        in_specs=qkv_specs
        + [pl.BlockSpec((1, S, HEAD_DIM), lambda b, h: (b, 0, gb_first + h)), table, table],
        out_specs=pl.BlockSpec((1, S, HEAD_DIM), lambda b, h: (b, 0, h)),
        out_shape=jax.ShapeDtypeStruct((B, S, B_WIDTH), BF16),
        scratch_shapes=[
            pltpu.VMEM((6, S, HEAD_DIM), F32),
            pltpu.VMEM((3, SPLIT, L4, HEAD_DIM), F32),
            pltpu.VMEM((S, HEAD_DIM), BF16),
            pltpu.VMEM((S + 2 * BAND_HALF, HEAD_DIM), BF16),
            pltpu.VMEM((S + 2 * BAND_HALF, HEAD_DIM), BF16),
            pltpu.VMEM((SPLIT, L4, HEAD_DIM), BF16),
            pltpu.VMEM((SPLIT, L4 + 2 * BAND_HALF, HEAD_DIM), BF16),
            pltpu.VMEM((SPLIT, L4 + 2 * BAND_HALF, HEAD_DIM), BF16),
            pltpu.VMEM((n16, BAND_TQ, HEAD_DIM), BF16),
            pltpu.VMEM((n16, BAND_TQ, HEAD_DIM), BF16),
            pltpu.VMEM((n16, BAND_TQ, HEAD_DIM), BF16),
            pltpu.VMEM((2, n16, BAND_TQ, HEAD_DIM), F32),
            pltpu.VMEM((B_GROUPS, S, HEAD_DIM), F32),
            pltpu.VMEM((B_GROUPS, S, HEAD_DIM), F32),
        ],
        compiler_params=pltpu.CompilerParams(
            dimension_semantics=("parallel", "parallel"),
            vmem_limit_bytes=V7X_VMEM_LIMIT_BYTES),
        name="band_mix",
    )(*([proj3] * (3 * B_GROUPS + 1)), *tables)


def _merge_kernel(x_ref, ya_ref, yb_ref, za0, za1, za2, za3, zb0, zb1, zb2, zb3, bias_ref,
                  wa_ref, wb_ref, wo_ref, fg_ref, out_ref, merged_ref, *, final_norm):
    pa = jnp.dot(ya_ref[...], wa_ref[...], preferred_element_type=F32)
    pb = jnp.dot(yb_ref[...], wb_ref[...], preferred_element_type=F32)
    for c, (za, zb) in enumerate(((za0, zb0), (za1, zb1), (za2, zb2), (za3, zb3))):
        cols = slice(c * COL_BLK, (c + 1) * COL_BLK)
        gate_a = jax.nn.sigmoid(za[...].astype(F32) + bias_ref[0:1, cols])
        gate_b = jax.nn.sigmoid(zb[...].astype(F32) + bias_ref[1:2, cols])
        merged_ref[:, cols] = (gate_a * pa[:, cols] + gate_b * pb[:, cols]).astype(BF16)

    y = x_ref[...] + jnp.dot(merged_ref[...], wo_ref[...], preferred_element_type=F32)
    out_ref[...] = _rms(y, fg_ref[...]) if final_norm else y


def _merge(x2, ya, yb, proj2, bias, wa, wb, wo, final_gain, final_norm, tm=256):
    T, D = x2.shape
    za_blk = _OFF_ZA // COL_BLK
    n_z = D // COL_BLK

    def row_blk(width, col=0):
        return pl.BlockSpec((tm, width), lambda i: (i, col))

    def whole(a):
        return pl.BlockSpec(a.shape, lambda i: (0, 0))

    z_specs = [row_blk(COL_BLK, za_blk + c) for c in range(2 * n_z)]
    return pl.pallas_call(
        functools.partial(_merge_kernel, final_norm=final_norm),
        grid=(T // tm,),
        in_specs=[row_blk(D), row_blk(A_WIDTH), row_blk(B_WIDTH)] + z_specs
        + [whole(bias), whole(wa), whole(wb), whole(wo), whole(final_gain)],
        out_specs=row_blk(D),
        out_shape=jax.ShapeDtypeStruct((T, D), F32),
        scratch_shapes=[pltpu.VMEM((tm, D), BF16)],
        compiler_params=pltpu.CompilerParams(
            dimension_semantics=("parallel",),
            vmem_limit_bytes=V7X_VMEM_LIMIT_BYTES),
        name="merge_output",
    )(x2, ya, yb, *([proj2] * (2 * n_z)), bias, wa, wb, wo, final_gain)


def _angles(pos, dim, theta):
    expo = np.arange(0, dim, 2, dtype=np.float64) / dim
    return pos.astype(np.float64)[:, None] / np.power(float(theta), expo)[None, :]


def _lane_tables(ang_blocks):
    cos, sin = [], []
    for blk in ang_blocks:
        if isinstance(blk, int):
            S = cos[0].shape[0]
            cos.append(np.ones((S, blk)))
            sin.append(np.zeros((S, blk)))
            continue
        c, s = np.cos(blk), np.sin(blk)
        cos += [c, c]
        sin += [-s, s]
    return tuple(jnp.asarray(np.concatenate(t, axis=-1), dtype=F32) for t in (cos, sin))


def _axial_tables(S):
    pos = np.arange(S)
    half = HEAD_DIM // 2
    return _lane_tables([_angles(pos // GRID_W, half, AXIAL_THETA),
                         _angles(pos % GRID_W, half, AXIAL_THETA)])


def _partial_tables(S):
    pos = np.arange(S)
    return _lane_tables([_angles(pos, PARTIAL_ROPE_DIM, ROPE_THETA),
                         HEAD_DIM - PARTIAL_ROPE_DIM])


def kernel(x, norm_gain, w_in, q_norm_gain, k_norm_gain, merge_gate_bias, w_branch_a,
           w_branch_b, w_out, final_norm_gain):
    B, S, D = x.shape
    depth = norm_gain.shape[0]
    n_cols = _OFF_ZA + 2 * D
    assert w_in.shape[2] == n_cols and S % GRID_W == 0
    assert all(w // (2 * d) == BAND_HALF for w, d in B_PATTERNS)
    axial = _axial_tables(S)
    partial = _partial_tables(S)
    x2 = x.reshape(B * S, D)
    for l in range(depth):
        proj2 = _input_projection(x2, norm_gain[l][None, :], w_in[l])
        proj3 = proj2.reshape(B, S, n_cols)
        ya = _attention_a(proj3, q_norm_gain[l][None, :], k_norm_gain[l][None, :], axial)
        yb = _band_mix(proj3, partial)
        x2 = _merge(x2, ya.reshape(B * S, A_WIDTH), yb.reshape(B * S, B_WIDTH), proj2,
                    merge_gate_bias[l], w_branch_a[l].astype(BF16),
                    w_branch_b[l].astype(BF16), w_out[l].astype(BF16),
                    final_norm_gain[None, :], final_norm=(l == depth - 1))
    return x2.reshape(B, S, D)
```

```python
import functools
import math

import numpy as np
import jax
import jax.numpy as jnp
from jax import lax
from jax.experimental import pallas as pl
from jax.experimental.pallas import tpu as pltpu

F32 = jnp.float32
BF16 = jnp.bfloat16

HEAD_DIM = 128
GRID_W = 64
NORM_EPS = 1e-6
A_Q_HEADS = 8
A_KV_HEADS = 2
A_GROUP = A_Q_HEADS // A_KV_HEADS
A_WIDTH = A_Q_HEADS * HEAD_DIM
A_KV_WIDTH = A_KV_HEADS * HEAD_DIM
AXIAL_THETA = 10000.0
AXIAL_SHIFT = HEAD_DIM // 4
MAX_SAFE_SCORE_BOUND = 50.0
SCORE_BOUND_SLACK = 1.01
B_PATTERNS = ((128, 1), (512, 4), (2048, 16))
B_GROUPS = len(B_PATTERNS)
B_HEADS = 4
B_WIDTH = B_HEADS * HEAD_DIM
PARTIAL_ROPE_DIM = HEAD_DIM // 4
PARTIAL_SHIFT = PARTIAL_ROPE_DIM // 2
ROPE_THETA = 500000.0
BAND_HALF = 64
BAND_TQ = 256
BAND_WIN = BAND_TQ + 2 * BAND_HALF
SPLIT = 4
MASK_VALUE = -1e30
LOG2_E = math.log2(math.e)
LN_2 = math.log(2.0)
Q_SCALE = HEAD_DIM ** -0.5 * LOG2_E

COL_BLK = 512
_OFF_QA = 0
_OFF_KA = A_WIDTH
_OFF_VA = _OFF_KA + A_KV_WIDTH
_OFF_GA = _OFF_VA + A_KV_WIDTH
_OFF_QB = _OFF_GA + A_WIDTH
_OFF_KB = _OFF_QB + B_GROUPS * B_WIDTH
_OFF_VB = _OFF_KB + B_GROUPS * B_WIDTH
_OFF_GB = _OFF_VB + B_GROUPS * B_WIDTH
_OFF_ZA = _OFF_GB + B_WIDTH

V7X_VMEM_LIMIT_BYTES = 56 * 1024 * 1024


def _rotary(x, cos, sin, shift):
    lane = lax.broadcasted_iota(jnp.int32, x.shape, 1)
    up = pltpu.roll(x, HEAD_DIM - shift, 1)
    dn = pltpu.roll(x, shift, 1)
    partner = jnp.where(lane % (2 * shift) < shift, up, dn)
    return x * cos + partner * sin


def _rms(x, gain):
    ms = jnp.mean(x * x, axis=-1, keepdims=True)
    return x * lax.rsqrt(ms + NORM_EPS) * gain


def _silu(g):
    return g * jax.nn.sigmoid(g)


def _proj_kernel(x_ref, g_ref, w_ref, o_ref, h_ref, wb_ref, *, n_full, tail):
    j, r = pl.program_id(1), pl.program_id(2)

    @pl.when(j == 0)
    def _():
        h_ref[r] = _rms(x_ref[...], g_ref[...]).astype(BF16)

    def project(width):
        cols = slice(0, width)

        @pl.when(r == 0)
        def _():
            w = w_ref[:, cols].astype(BF16)
            wb_ref[:, cols] = w
            o_ref[:, cols] = jnp.dot(h_ref[0], w, preferred_element_type=F32).astype(BF16)

        @pl.when(r != 0)
        def _():
            o_ref[:, cols] = jnp.dot(h_ref[r], wb_ref[:, cols],
                                     preferred_element_type=F32).astype(BF16)

    @pl.when(j < n_full)
    def _():
        project(w_ref.shape[1])

    if tail:
        @pl.when(j == n_full)
        def _():
            project(tail)


def _input_projection(x2, gain, w, tm=1024, tn=2 * COL_BLK, rows_per_group=2):
    T, D = x2.shape
    N = w.shape[1]
    R = rows_per_group
    n_full, tail = divmod(N, tn)

    def x_index(g, j, r):
        return g * R + jnp.where(j == 0, r, R - 1), 0

    return pl.pallas_call(
        functools.partial(_proj_kernel, n_full=n_full, tail=tail),
        grid=(T // (tm * R), pl.cdiv(N, tn), R),
        in_specs=[
            pl.BlockSpec((tm, D), x_index),
            pl.BlockSpec((1, D), lambda g, j, r: (0, 0)),
            pl.BlockSpec((D, tn), lambda g, j, r: (0, j)),
        ],
        out_specs=pl.BlockSpec((tm, tn), lambda g, j, r: (g * R + r, j)),
        out_shape=jax.ShapeDtypeStruct((T, N), BF16),
        scratch_shapes=[pltpu.VMEM((R, tm, D), BF16), pltpu.VMEM((D, tn), BF16)],
        compiler_params=pltpu.CompilerParams(
            dimension_semantics=("parallel", "arbitrary", "arbitrary"),
            vmem_limit_bytes=V7X_VMEM_LIMIT_BYTES),
        name="input_projection",
    )(x2, gain, w)


def _attn_a_kernel(q_ref, qnext_ref, k_ref, v_ref, ga_ref, gq_ref, gk_ref, cos_ref, sin_ref,
                   o_ref, kn_ref, k2max_ref, qs_ref, q2max_ref, *, tq, n_q):
    qi = pl.program_id(2)
    slot = qi % 2

    def max_sq_norm(x_bf16):
        xf = x_bf16.astype(F32)
        return jnp.max(jnp.sum(xf * xf, axis=-1, keepdims=True), axis=0, keepdims=True)

    def prepare_queries(src_ref, tile, dst):
        rows = pl.ds(pl.multiple_of(tile * tq, tq), tq)
        cos, sin = cos_ref[rows, :], sin_ref[rows, :]
        q2max = None
        for h in range(A_GROUP):
            cols = slice(h * HEAD_DIM, (h + 1) * HEAD_DIM)
            q = _rms(src_ref[0, :, cols].astype(F32), gq_ref[...])
            q = (_rotary(q, cos, sin, AXIAL_SHIFT) * Q_SCALE).astype(BF16)
            qs_ref[dst, :, cols] = q
            q2max = max_sq_norm(q) if q2max is None else jnp.maximum(q2max, max_sq_norm(q))
        q2max_ref[dst] = jnp.broadcast_to(q2max, q2max_ref.shape[1:])

    @pl.when(qi == 0)
    def _():
        k = _rms(k_ref[0].astype(F32), gk_ref[...])
        k = _rotary(k, cos_ref[...], sin_ref[...], AXIAL_SHIFT).astype(BF16)
        kn_ref[...] = k
        k2max_ref[...] = jnp.broadcast_to(max_sq_norm(k), k2max_ref.shape)
        prepare_queries(q_ref, 0, 0)

    bound = jnp.sqrt(q2max_ref[slot, 0:1, 0:1] * k2max_ref[0:1, 0:1]) * SCORE_BOUND_SLACK
    bound_is_safe = bound[0, 0] <= MAX_SAFE_SCORE_BOUND

    def attend(shift_of):
        prepare_queries(qnext_ref, jnp.minimum(qi + 1, n_q - 1), 1 - slot)
        kn = kn_ref[...]
        v = v_ref[0]
        for h in range(A_GROUP):
            cols = slice(h * HEAD_DIM, (h + 1) * HEAD_DIM)
            s = lax.dot_general(qs_ref[slot, :, cols], kn, (((1,), (1,)), ((), ())),
                                preferred_element_type=F32)
            p = jnp.exp2(s - shift_of(s))
            l = jnp.sum(p, axis=-1, keepdims=True)
            o = jnp.dot(p.astype(BF16), v, preferred_element_type=F32)
            g = ga_ref[0, :, cols].astype(F32)
            o_ref[0, :, cols] = (o * (1.0 / l) * _silu(g)).astype(BF16)

    @pl.when(bound_is_safe)
    def _():
        attend(lambda s: bound)

    @pl.when(jnp.logical_not(bound_is_safe))
    def _():
        attend(lambda s: jnp.max(s, axis=-1, keepdims=True))


def _attention_a(proj3, q_gain, k_gain, tables, tq=256):
    B, S, _ = proj3.shape
    n_q = S // tq
    gw = A_GROUP * HEAD_DIM
    ka_blk = _OFF_KA // HEAD_DIM
    va_blk = _OFF_VA // HEAD_DIM
    ga_blk = _OFF_GA // gw
    full = pl.BlockSpec((S, HEAD_DIM), lambda b, g, i: (0, 0))
    vec = pl.BlockSpec((1, HEAD_DIM), lambda b, g, i: (0, 0))
    return pl.pallas_call(
        functools.partial(_attn_a_kernel, tq=tq, n_q=n_q),
        grid=(B, A_KV_HEADS, n_q),
        in_specs=[
            pl.BlockSpec((1, tq, gw), lambda b, g, i: (b, i, g)),
            pl.BlockSpec((1, tq, gw), lambda b, g, i: (b, jnp.minimum(i + 1, n_q - 1), g)),
            pl.BlockSpec((1, S, HEAD_DIM), lambda b, g, i: (b, 0, ka_blk + g)),
            pl.BlockSpec((1, S, HEAD_DIM), lambda b, g, i: (b, 0, va_blk + g)),
            pl.BlockSpec((1, tq, gw), lambda b, g, i: (b, i, ga_blk + g)),
            vec, vec, full, full,
        ],
        out_specs=pl.BlockSpec((1, tq, gw), lambda b, g, i: (b, i, g)),
        out_shape=jax.ShapeDtypeStruct((B, S, A_WIDTH), BF16),
        scratch_shapes=[
            pltpu.VMEM((S, HEAD_DIM), BF16),
            pltpu.VMEM((8, HEAD_DIM), F32),
            pltpu.VMEM((2, tq, gw), BF16),
            pltpu.VMEM((2, 8, HEAD_DIM), F32),
        ],
        compiler_params=pltpu.CompilerParams(
            dimension_semantics=("parallel", "parallel", "arbitrary"),
            vmem_limit_bytes=V7X_VMEM_LIMIT_BYTES),
        name="attention_a",
    )(proj3, proj3, proj3, proj3, proj3, q_gain, k_gain, *tables)


def _band_tile(q, kw, vw, valid):
    s = lax.dot_general(q, kw, (((1,), (1,)), ((), ())), preferred_element_type=F32)
    s = jnp.where(valid, s, MASK_VALUE)
    m = jnp.max(s, axis=-1, keepdims=True)
    p = jnp.exp2(s - m)
    l = jnp.sum(p, axis=-1, keepdims=True)
    o = jnp.dot(p.astype(BF16), vw, preferred_element_type=F32)
    lse = m * LN_2 + jnp.log(l)
    return o * (1.0 / l), jnp.broadcast_to(lse, (BAND_TQ, HEAD_DIM))


def _window_mask(t, n_tiles):
    qi = lax.broadcasted_iota(jnp.int32, (BAND_TQ, BAND_WIN), 0)
    kj = lax.broadcasted_iota(jnp.int32, (BAND_TQ, BAND_WIN), 1)
    kpos = kj + (t * BAND_TQ - BAND_HALF)
    return (kj >= qi) & (kj <= qi + 2 * BAND_HALF) & (kpos >= 0) & (kpos < n_tiles * BAND_TQ)


def _zero_pads(ref, lead, n_rows):
    zeros = jnp.zeros((BAND_HALF, HEAD_DIM), BF16)
    for idx in lead:
        ref[idx + (slice(0, BAND_HALF),)] = zeros
        ref[idx + (slice(BAND_HALF + n_rows, 2 * BAND_HALF + n_rows),)] = zeros


def _band_mix_kernel(q1_ref, k1_ref, v1_ref, q4_ref, k4_ref, v4_ref, q16_ref, k16_ref, v16_ref,
                     gb_ref, cos_ref, sin_ref, y_ref,
                     nat_ref, tmp_ref, qd1_ref, kp1_ref, vp1_ref, qd4_ref, kp4_ref, vp4_ref,
                     qd16_ref, kd16_ref, vd16_ref, od16_ref, og_ref, lg_ref, *, S):
    L4 = S // SPLIT
    L16 = L4 // SPLIT
    n1, n4 = S // BAND_TQ, L4 // BAND_TQ
    per_tile = BAND_TQ // L16

    def rotated(src_ref, scale=None):
        y = _rotary(src_ref[0].astype(F32), cos_ref[...], sin_ref[...], PARTIAL_SHIFT)
        return y if scale is None else y * scale

    _zero_pads(kp1_ref, [()], S)
    _zero_pads(vp1_ref, [()], S)
    qd1_ref[...] = rotated(q1_ref, Q_SCALE).astype(BF16)
    kp1_ref[BAND_HALF:BAND_HALF + S, :] = rotated(k1_ref).astype(BF16)
    vp1_ref[BAND_HALF:BAND_HALF + S, :] = v1_ref[0]

    for t in range(n1):
        rows = slice(t * BAND_TQ, (t + 1) * BAND_TQ)
        win = slice(t * BAND_TQ, t * BAND_TQ + BAND_WIN)
        o, lse = _band_tile(qd1_ref[rows, :], kp1_ref[win, :], vp1_ref[win, :],
                            _window_mask(t, n1))
        og_ref[0, rows, :] = o
        lg_ref[0, rows, :] = lse

    _zero_pads(kp4_ref, [(r,) for r in range(SPLIT)], L4)
    _zero_pads(vp4_ref, [(r,) for r in range(SPLIT)], L4)
    for n, (val, dst_ref, lead) in enumerate(((rotated(q4_ref, Q_SCALE), qd4_ref, 0),
                                              (rotated(k4_ref), kp4_ref, BAND_HALF),
                                              (v4_ref[0].astype(F32), vp4_ref, BAND_HALF))):
        nat_ref[n] = val
        for r in range(SPLIT):
            dst_ref[r, lead:lead + L4, :] = nat_ref[n, pl.ds(r, L4, stride=SPLIT), :].astype(BF16)

    for r in range(SPLIT):
        for t in range(n4):
            rows = pl.ds(r + t * (BAND_TQ * SPLIT), BAND_TQ, stride=SPLIT)
            win = slice(t * BAND_TQ, t * BAND_TQ + BAND_WIN)
            o, lse = _band_tile(qd4_ref[r, t * BAND_TQ:(t + 1) * BAND_TQ, :],
                                kp4_ref[r, win, :], vp4_ref[r, win, :], _window_mask(t, n4))
            og_ref[1, rows, :] = o
            lg_ref[1, rows, :] = lse

    for n, (val, dst_ref) in enumerate(((rotated(q16_ref, Q_SCALE), qd16_ref),
                                        (rotated(k16_ref), kd16_ref),
                                        (v16_ref[0].astype(F32), vd16_ref))):
        nat_ref[3 + n] = val
        for a in range(SPLIT):
            tmp_ref[n, a] = nat_ref[3 + n, pl.ds(a, L4, stride=SPLIT), :]
            for b in range(SPLIT):
                r = a + SPLIT * b
                dst_ref[r // per_tile, (r % per_tile) * L16:(r % per_tile + 1) * L16, :] = \
                    tmp_ref[n, a, pl.ds(b, L16, stride=SPLIT), :].astype(BF16)

    qi = lax.broadcasted_iota(jnp.int32, (BAND_TQ, BAND_TQ), 0)
    kj = lax.broadcasted_iota(jnp.int32, (BAND_TQ, BAND_TQ), 1)
    same_residue = functools.reduce(
        jnp.logical_or, [(qi >= c * L16) & (qi < (c + 1) * L16) & (kj >= c * L16)
                         & (kj < (c + 1) * L16) for c in range(per_tile)])
    stacked_band = same_residue & (kj >= qi - BAND_HALF) & (kj <= qi + BAND_HALF)
    for j in range(SPLIT * SPLIT // per_tile):
        o, lse = _band_tile(qd16_ref[j], kd16_ref[j], vd16_ref[j], stacked_band)
        od16_ref[0, j] = o
        od16_ref[1, j] = lse

    for n, dst_ref in enumerate((og_ref, lg_ref)):
        for a in range(SPLIT):
            for b in range(SPLIT):
                r = a + SPLIT * b
                tmp_ref[n, a, pl.ds(b, L16, stride=SPLIT), :] = \
                    od16_ref[n, r // per_tile, (r % per_tile) * L16:(r % per_tile + 1) * L16, :]
            dst_ref[2, pl.ds(a, L4, stride=SPLIT), :] = tmp_ref[n, a]

    chunk = BAND_TQ

    def mix(i, carry):
        rows = pl.ds(pl.multiple_of(i * chunk, chunk), chunk)
        lses = [lg_ref[g, rows, :] for g in range(B_GROUPS)]
        lmax = functools.reduce(jnp.maximum, lses)
        es = [jnp.exp(l - lmax) for l in lses]
        num = sum(e * og_ref[g, rows, :] for g, e in enumerate(es))
        ob = num * (1.0 / sum(es))
        y_ref[0, rows, :] = (ob * _silu(gb_ref[0, rows, :].astype(F32))).astype(BF16)
        return carry

    lax.fori_loop(0, S // chunk, mix, 0)


def _band_mix(proj3, tables):
    B, S, NC = proj3.shape
    assert tuple(d for _, d in B_PATTERNS) == (1, SPLIT, SPLIT * SPLIT)
    L4, L16 = S // SPLIT, S // (SPLIT * SPLIT)
    assert L4 % BAND_TQ == 0 and BAND_TQ % L16 == 0
    n16 = S // BAND_TQ

    def head_blk(off, g):
        first = off // HEAD_DIM + g * B_HEADS
        return pl.BlockSpec((1, S, HEAD_DIM), lambda b, h: (b, 0, first + h))

    qkv_specs = [head_blk(off, g) for g in range(B_GROUPS)
                 for off in (_OFF_QB, _OFF_KB, _OFF_VB)]
    gb_first = _OFF_GB // HEAD_DIM
    table = pl.BlockSpec((S, HEAD_DIM), lambda b, h: (0, 0))
    return pl.pallas_call(
        functools.partial(_band_mix_kernel, S=S),
        grid=(B, B_HEADS),
        in_specs=qkv_specs
        + [pl.BlockSpec((1, S, HEAD_DIM), lambda b, h: (b, 0, gb_first + h)), table, table],
        out_specs=pl.BlockSpec((1, S, HEAD_DIM), lambda b, h: (b, 0, h)),
        out_shape=jax.ShapeDtypeStruct((B, S, B_WIDTH), BF16),
        scratch_shapes=[
            pltpu.VMEM((6, S, HEAD_DIM), F32),
            pltpu.VMEM((3, SPLIT, L4, HEAD_DIM), F32),
            pltpu.VMEM((S, HEAD_DIM), BF16),
            pltpu.VMEM((S + 2 * BAND_HALF, HEAD_DIM), BF16),
            pltpu.VMEM((S + 2 * BAND_HALF, HEAD_DIM), BF16),
            pltpu.VMEM((SPLIT, L4, HEAD_DIM), BF16),
            pltpu.VMEM((SPLIT, L4 + 2 * BAND_HALF, HEAD_DIM), BF16),
            pltpu.VMEM((SPLIT, L4 + 2 * BAND_HALF, HEAD_DIM), BF16),
            pltpu.VMEM((n16, BAND_TQ, HEAD_DIM), BF16),
            pltpu.VMEM((n16, BAND_TQ, HEAD_DIM), BF16),
            pltpu.VMEM((n16, BAND_TQ, HEAD_DIM), BF16),
            pltpu.VMEM((2, n16, BAND_TQ, HEAD_DIM), F32),
            pltpu.VMEM((B_GROUPS, S, HEAD_DIM), F32),
            pltpu.VMEM((B_GROUPS, S, HEAD_DIM), F32),
        ],
        compiler_params=pltpu.CompilerParams(
            dimension_semantics=("parallel", "parallel"),
            vmem_limit_bytes=V7X_VMEM_LIMIT_BYTES),
        name="band_mix",
    )(*([proj3] * (3 * B_GROUPS + 1)), *tables)


def _merge_kernel(x_ref, ya_ref, yb_ref, za0, za1, za2, za3, zb0, zb1, zb2, zb3, bias_ref,
                  wa_ref, wb_ref, wo_ref, fg_ref, out_ref, merged_ref, *, final_norm):
    pa = jnp.dot(ya_ref[...], wa_ref[...], preferred_element_type=F32)
    pb = jnp.dot(yb_ref[...], wb_ref[...], preferred_element_type=F32)
    for c, (za, zb) in enumerate(((za0, zb0), (za1, zb1), (za2, zb2), (za3, zb3))):
        cols = slice(c * COL_BLK, (c + 1) * COL_BLK)
        gate_a = jax.nn.sigmoid(za[...].astype(F32) + bias_ref[0:1, cols])
        gate_b = jax.nn.sigmoid(zb[...].astype(F32) + bias_ref[1:2, cols])
        merged_ref[:, cols] = (gate_a * pa[:, cols] + gate_b * pb[:, cols]).astype(BF16)

    y = x_ref[...] + jnp.dot(merged_ref[...], wo_ref[...], preferred_element_type=F32)
    out_ref[...] = _rms(y, fg_ref[...]) if final_norm else y


def _merge(x2, ya, yb, proj2, bias, wa, wb, wo, final_gain, final_norm, tm=256):
    T, D = x2.shape
    za_blk = _OFF_ZA // COL_BLK
    n_z = D // COL_BLK

    def row_blk(width, col=0):
        return pl.BlockSpec((tm, width), lambda i: (i, col))

    def whole(a):
        return pl.BlockSpec(a.shape, lambda i: (0, 0))

    z_specs = [row_blk(COL_BLK, za_blk + c) for c in range(2 * n_z)]
    return pl.pallas_call(
        functools.partial(_merge_kernel, final_norm=final_norm),
        grid=(T // tm,),
        in_specs=[row_blk(D), row_blk(A_WIDTH), row_blk(B_WIDTH)] + z_specs
        + [whole(bias), whole(wa), whole(wb), whole(wo), whole(final_gain)],
        out_specs=row_blk(D),
        out_shape=jax.ShapeDtypeStruct((T, D), F32),
        scratch_shapes=[pltpu.VMEM((tm, D), BF16)],
        compiler_params=pltpu.CompilerParams(
            dimension_semantics=("parallel",),
            vmem_limit_bytes=V7X_VMEM_LIMIT_BYTES),
        name="merge_output",
    )(x2, ya, yb, *([proj2] * (2 * n_z)), bias, wa, wb, wo, final_gain)


def _angles(pos, dim, theta):
    expo = np.arange(0, dim, 2, dtype=np.float64) / dim
    return pos.astype(np.float64)[:, None] / np.power(float(theta), expo)[None, :]


def _lane_tables(ang_blocks):
    cos, sin = [], []
    for blk in ang_blocks:
        if isinstance(blk, int):
            S = cos[0].shape[0]
            cos.append(np.ones((S, blk)))
            sin.append(np.zeros((S, blk)))
            continue
        c, s = np.cos(blk), np.sin(blk)
        cos += [c, c]
        sin += [-s, s]
    return tuple(jnp.asarray(np.concatenate(t, axis=-1), dtype=F32) for t in (cos, sin))


def _axial_tables(S):
    pos = np.arange(S)
    half = HEAD_DIM // 2
    return _lane_tables([_angles(pos // GRID_W, half, AXIAL_THETA),
                         _angles(pos % GRID_W, half, AXIAL_THETA)])


def _partial_tables(S):
    pos = np.arange(S)
    return _lane_tables([_angles(pos, PARTIAL_ROPE_DIM, ROPE_THETA),
                         HEAD_DIM - PARTIAL_ROPE_DIM])


def kernel(x, norm_gain, w_in, q_norm_gain, k_norm_gain, merge_gate_bias, w_branch_a,
           w_branch_b, w_out, final_norm_gain):
    B, S, D = x.shape
    depth = norm_gain.shape[0]
    n_cols = _OFF_ZA + 2 * D
    assert w_in.shape[2] == n_cols and S % GRID_W == 0
    assert all(w // (2 * d) == BAND_HALF for w, d in B_PATTERNS)
    axial = _axial_tables(S)
    partial = _partial_tables(S)
    x2 = x.reshape(B * S, D)
    for l in range(depth):
        proj2 = _input_projection(x2, norm_gain[l][None, :], w_in[l])
        proj3 = proj2.reshape(B, S, n_cols)
        ya = _attention_a(proj3, q_norm_gain[l][None, :], k_norm_gain[l][None, :], axial)
        yb = _band_mix(proj3, partial)
        x2 = _merge(x2, ya.reshape(B * S, A_WIDTH), yb.reshape(B * S, B_WIDTH), proj2,
                    merge_gate_bias[l], w_branch_a[l].astype(BF16),
                    w_branch_b[l].astype(BF16), w_out[l].astype(BF16),
                    final_norm_gain[None, :], final_norm=(l == depth - 1))
    return x2.reshape(B, S, D)
```

```python
import functools
import math

import numpy as np
import jax
import jax.numpy as jnp
from jax import lax
from jax.experimental import pallas as pl
from jax.experimental.pallas import tpu as pltpu

F32 = jnp.float32
BF16 = jnp.bfloat16

HEAD_DIM = 128
GRID_W = 64
NORM_EPS = 1e-6
A_Q_HEADS = 8
A_KV_HEADS = 2
A_GROUP = A_Q_HEADS // A_KV_HEADS
A_WIDTH = A_Q_HEADS * HEAD_DIM
A_KV_WIDTH = A_KV_HEADS * HEAD_DIM
AXIAL_THETA = 10000.0
AXIAL_SHIFT = HEAD_DIM // 4
MAX_SAFE_SCORE_BOUND = 50.0
SCORE_BOUND_SLACK = 1.01
B_PATTERNS = ((128, 1), (512, 4), (2048, 16))
B_GROUPS = len(B_PATTERNS)
B_HEADS = 4
B_WIDTH = B_HEADS * HEAD_DIM
PARTIAL_ROPE_DIM = HEAD_DIM // 4
PARTIAL_SHIFT = PARTIAL_ROPE_DIM // 2
ROPE_THETA = 500000.0
BAND_HALF = 64
BAND_TQ = 256
BAND_WIN = BAND_TQ + 2 * BAND_HALF
SPLIT = 4
MASK_VALUE = -1e30
LOG2_E = math.log2(math.e)
LN_2 = math.log(2.0)
Q_SCALE = HEAD_DIM ** -0.5 * LOG2_E

COL_BLK = 512
_OFF_QA = 0
_OFF_KA = A_WIDTH
_OFF_VA = _OFF_KA + A_KV_WIDTH
_OFF_GA = _OFF_VA + A_KV_WIDTH
_OFF_QB = _OFF_GA + A_WIDTH
_OFF_KB = _OFF_QB + B_GROUPS * B_WIDTH
_OFF_VB = _OFF_KB + B_GROUPS * B_WIDTH
_OFF_GB = _OFF_VB + B_GROUPS * B_WIDTH
_OFF_ZA = _OFF_GB + B_WIDTH

V7X_VMEM_LIMIT_BYTES = 56 * 1024 * 1024


def _rotary(x, cos, sin, shift):
    lane = lax.broadcasted_iota(jnp.int32, x.shape, 1)
    up = pltpu.roll(x, HEAD_DIM - shift, 1)
    dn = pltpu.roll(x, shift, 1)
    partner = jnp.where(lane % (2 * shift) < shift, up, dn)
    return x * cos + partner * sin


def _rms(x, gain):
    ms = jnp.mean(x * x, axis=-1, keepdims=True)
    return x * lax.rsqrt(ms + NORM_EPS) * gain


def _silu(g):
    return g * jax.nn.sigmoid(g)


def _proj_kernel(x_ref, g_ref, w_ref, o_ref, h_ref, wb_ref, *, n_full, tail):
    j, r = pl.program_id(1), pl.program_id(2)

    @pl.when(j == 0)
    def _():
        h_ref[r] = _rms(x_ref[...], g_ref[...]).astype(BF16)

    def project(width):
        cols = slice(0, width)

        @pl.when(r == 0)
        def _():
            w = w_ref[:, cols].astype(BF16)
            wb_ref[:, cols] = w
            o_ref[:, cols] = jnp.dot(h_ref[0], w, preferred_element_type=F32).astype(BF16)

        @pl.when(r != 0)
        def _():
            o_ref[:, cols] = jnp.dot(h_ref[r], wb_ref[:, cols],
                                     preferred_element_type=F32).astype(BF16)

    @pl.when(j < n_full)
    def _():
        project(w_ref.shape[1])

    if tail:
        @pl.when(j == n_full)
        def _():
            project(tail)


def _input_projection(x2, gain, w, tm=1024, tn=2 * COL_BLK, rows_per_group=2):
    T, D = x2.shape
    N = w.shape[1]
    R = rows_per_group
    n_full, tail = divmod(N, tn)

    def x_index(g, j, r):
        return g * R + jnp.where(j == 0, r, R - 1), 0

    return pl.pallas_call(
        functools.partial(_proj_kernel, n_full=n_full, tail=tail),
        grid=(T // (tm * R), pl.cdiv(N, tn), R),
        in_specs=[
            pl.BlockSpec((tm, D), x_index),
            pl.BlockSpec((1, D), lambda g, j, r: (0, 0)),
            pl.BlockSpec((D, tn), lambda g, j, r: (0, j)),
        ],
        out_specs=pl.BlockSpec((tm, tn), lambda g, j, r: (g * R + r, j)),
        out_shape=jax.ShapeDtypeStruct((T, N), BF16),
        scratch_shapes=[pltpu.VMEM((R, tm, D), BF16), pltpu.VMEM((D, tn), BF16)],
        compiler_params=pltpu.CompilerParams(
            dimension_semantics=("parallel", "arbitrary", "arbitrary"),
            vmem_limit_bytes=V7X_VMEM_LIMIT_BYTES),
        name="input_projection",
    )(x2, gain, w)


def _attn_a_kernel(q_ref, qnext_ref, k_ref, v_ref, ga_ref, gq_ref, gk_ref, cos_ref, sin_ref,
                   o_ref, kn_ref, k2max_ref, qs_ref, q2max_ref, *, tq, n_q):
    qi = pl.program_id(2)
    slot = qi % 2

    def max_sq_norm(x_bf16):
        xf = x_bf16.astype(F32)
        return jnp.max(jnp.sum(xf * xf, axis=-1, keepdims=True), axis=0, keepdims=True)

    def prepare_queries(src_ref, tile, dst):
        rows = pl.ds(pl.multiple_of(tile * tq, tq), tq)
        cos, sin = cos_ref[rows, :], sin_ref[rows, :]
        q2max = None
        for h in range(A_GROUP):
            cols = slice(h * HEAD_DIM, (h + 1) * HEAD_DIM)
            q = _rms(src_ref[0, :, cols].astype(F32), gq_ref[...])
            q = (_rotary(q, cos, sin, AXIAL_SHIFT) * Q_SCALE).astype(BF16)
            qs_ref[dst, :, cols] = q
            q2max = max_sq_norm(q) if q2max is None else jnp.maximum(q2max, max_sq_norm(q))
        q2max_ref[dst] = jnp.broadcast_to(q2max, q2max_ref.shape[1:])

    @pl.when(qi == 0)
    def _():
        k = _rms(k_ref[0].astype(F32), gk_ref[...])
        k = _rotary(k, cos_ref[...], sin_ref[...], AXIAL_SHIFT).astype(BF16)
        kn_ref[...] = k
        k2max_ref[...] = jnp.broadcast_to(max_sq_norm(k), k2max_ref.shape)
        prepare_queries(q_ref, 0, 0)

    bound = jnp.sqrt(q2max_ref[slot, 0:1, 0:1] * k2max_ref[0:1, 0:1]) * SCORE_BOUND_SLACK
    bound_is_safe = bound[0, 0] <= MAX_SAFE_SCORE_BOUND

    def attend(shift_of):
        prepare_queries(qnext_ref, jnp.minimum(qi + 1, n_q - 1), 1 - slot)
        kn = kn_ref[...]
        v = v_ref[0]
        for h in range(A_GROUP):
            cols = slice(h * HEAD_DIM, (h + 1) * HEAD_DIM)
            s = lax.dot_general(qs_ref[slot, :, cols], kn, (((1,), (1,)), ((), ())),
                                preferred_element_type=F32)
            p = jnp.exp2(s - shift_of(s))
            l = jnp.sum(p, axis=-1, keepdims=True)
            o = jnp.dot(p.astype(BF16), v, preferred_element_type=F32)
            g = ga_ref[0, :, cols].astype(F32)
            o_ref[0, :, cols] = (o * (1.0 / l) * _silu(g)).astype(BF16)

    @pl.when(bound_is_safe)
    def _():
        attend(lambda s: bound)

    @pl.when(jnp.logical_not(bound_is_safe))
    def _():
        attend(lambda s: jnp.max(s, axis=-1, keepdims=True))


def _attention_a(proj3, q_gain, k_gain, tables, tq=512):
    B, S, _ = proj3.shape
    n_q = S // tq
    gw = A_GROUP * HEAD_DIM
    ka_blk = _OFF_KA // HEAD_DIM
    va_blk = _OFF_VA // HEAD_DIM
    ga_blk = _OFF_GA // gw
    full = pl.BlockSpec((S, HEAD_DIM), lambda b, g, i: (0, 0))
    vec = pl.BlockSpec((1, HEAD_DIM), lambda b, g, i: (0, 0))
    return pl.pallas_call(
        functools.partial(_attn_a_kernel, tq=tq, n_q=n_q),
        grid=(B, A_KV_HEADS, n_q),
        in_specs=[
            pl.BlockSpec((1, tq, gw), lambda b, g, i: (b, i, g)),
            pl.BlockSpec((1, tq, gw), lambda b, g, i: (b, jnp.minimum(i + 1, n_q - 1), g)),
            pl.BlockSpec((1, S, HEAD_DIM), lambda b, g, i: (b, 0, ka_blk + g)),
            pl.BlockSpec((1, S, HEAD_DIM), lambda b, g, i: (b, 0, va_blk + g)),
            pl.BlockSpec((1, tq, gw), lambda b, g, i: (b, i, ga_blk + g)),
            vec, vec, full, full,
        ],
        out_specs=pl.BlockSpec((1, tq, gw), lambda b, g, i: (b, i, g)),
        out_shape=jax.ShapeDtypeStruct((B, S, A_WIDTH), BF16),
        scratch_shapes=[
            pltpu.VMEM((S, HEAD_DIM), BF16),
            pltpu.VMEM((8, HEAD_DIM), F32),
            pltpu.VMEM((2, tq, gw), BF16),
            pltpu.VMEM((2, 8, HEAD_DIM), F32),
        ],
        compiler_params=pltpu.CompilerParams(
            dimension_semantics=("parallel", "parallel", "arbitrary"),
            vmem_limit_bytes=V7X_VMEM_LIMIT_BYTES),
        name="attention_a",
    )(proj3, proj3, proj3, proj3, proj3, q_gain, k_gain, *tables)


def _band_tile(q, kw, vw, valid):
    s = lax.dot_general(q, kw, (((1,), (1,)), ((), ())), preferred_element_type=F32)
    s = jnp.where(valid, s, MASK_VALUE)
    m = jnp.max(s, axis=-1, keepdims=True)
    p = jnp.exp2(s - m)
    l = jnp.sum(p, axis=-1, keepdims=True)
    o = jnp.dot(p.astype(BF16), vw, preferred_element_type=F32)
    lse = m * LN_2 + jnp.log(l)
    return o * (1.0 / l), jnp.broadcast_to(lse, (BAND_TQ, HEAD_DIM))


def _window_mask(t, n_tiles):
    qi = lax.broadcasted_iota(jnp.int32, (BAND_TQ, BAND_WIN), 0)
    kj = lax.broadcasted_iota(jnp.int32, (BAND_TQ, BAND_WIN), 1)
    kpos = kj + (t * BAND_TQ - BAND_HALF)
    return (kj >= qi) & (kj <= qi + 2 * BAND_HALF) & (kpos >= 0) & (kpos < n_tiles * BAND_TQ)


def _zero_pads(ref, lead, n_rows):
    zeros = jnp.zeros((BAND_HALF, HEAD_DIM), BF16)
    for idx in lead:
        ref[idx + (slice(0, BAND_HALF),)] = zeros
        ref[idx + (slice(BAND_HALF + n_rows, 2 * BAND_HALF + n_rows),)] = zeros


def _band_mix_kernel(q1_ref, k1_ref, v1_ref, q4_ref, k4_ref, v4_ref, q16_ref, k16_ref, v16_ref,
                     gb_ref, cos_ref, sin_ref, y_ref,
                     nat_ref, tmp_ref, qd1_ref, kp1_ref, vp1_ref, qd4_ref, kp4_ref, vp4_ref,
                     qd16_ref, kd16_ref, vd16_ref, od16_ref, og_ref, lg_ref, *, S):
    L4 = S // SPLIT
    L16 = L4 // SPLIT
    n1, n4 = S // BAND_TQ, L4 // BAND_TQ
    per_tile = BAND_TQ // L16

    def rotated(src_ref, scale=None):
        y = _rotary(src_ref[0].astype(F32), cos_ref[...], sin_ref[...], PARTIAL_SHIFT)
        return y if scale is None else y * scale

    _zero_pads(kp1_ref, [()], S)
    _zero_pads(vp1_ref, [()], S)
    qd1_ref[...] = rotated(q1_ref, Q_SCALE).astype(BF16)
    kp1_ref[BAND_HALF:BAND_HALF + S, :] = rotated(k1_ref).astype(BF16)
    vp1_ref[BAND_HALF:BAND_HALF + S, :] = v1_ref[0]

    for t in range(n1):
        rows = slice(t * BAND_TQ, (t + 1) * BAND_TQ)
        win = slice(t * BAND_TQ, t * BAND_TQ + BAND_WIN)
        o, lse = _band_tile(qd1_ref[rows, :], kp1_ref[win, :], vp1_ref[win, :],
                            _window_mask(t, n1))
        og_ref[0, rows, :] = o
        lg_ref[0, rows, :] = lse

    _zero_pads(kp4_ref, [(r,) for r in range(SPLIT)], L4)
    _zero_pads(vp4_ref, [(r,) for r in range(SPLIT)], L4)
    for n, (val, dst_ref, lead) in enumerate(((rotated(q4_ref, Q_SCALE), qd4_ref, 0),
                                              (rotated(k4_ref), kp4_ref, BAND_HALF),
                                              (v4_ref[0].astype(F32), vp4_ref, BAND_HALF))):
        nat_ref[n] = val
        for r in range(SPLIT):
            dst_ref[r, lead:lead + L4, :] = nat_ref[n, pl.ds(r, L4, stride=SPLIT), :].astype(BF16)

    for r in range(SPLIT):
        for t in range(n4):
            rows = pl.ds(r + t * (BAND_TQ * SPLIT), BAND_TQ, stride=SPLIT)
            win = slice(t * BAND_TQ, t * BAND_TQ + BAND_WIN)
            o, lse = _band_tile(qd4_ref[r, t * BAND_TQ:(t + 1) * BAND_TQ, :],
                                kp4_ref[r, win, :], vp4_ref[r, win, :], _window_mask(t, n4))
            og_ref[1, rows, :] = o
            lg_ref[1, rows, :] = lse

    for n, (val, dst_ref) in enumerate(((rotated(q16_ref, Q_SCALE), qd16_ref),
                                        (rotated(k16_ref), kd16_ref),
                                        (v16_ref[0].astype(F32), vd16_ref))):
        nat_ref[3 + n] = val
        for a in range(SPLIT):
            tmp_ref[n, a] = nat_ref[3 + n, pl.ds(a, L4, stride=SPLIT), :]
            for b in range(SPLIT):
                r = a + SPLIT * b
                dst_ref[r // per_tile, (r % per_tile) * L16:(r % per_tile + 1) * L16, :] = \
                    tmp_ref[n, a, pl.ds(b, L16, stride=SPLIT), :].astype(BF16)

    qi = lax.broadcasted_iota(jnp.int32, (BAND_TQ, BAND_TQ), 0)
    kj = lax.broadcasted_iota(jnp.int32, (BAND_TQ, BAND_TQ), 1)
    same_residue = functools.reduce(
        jnp.logical_or, [(qi >= c * L16) & (qi < (c + 1) * L16) & (kj >= c * L16)
                         & (kj < (c + 1) * L16) for c in range(per_tile)])
    stacked_band = same_residue & (kj >= qi - BAND_HALF) & (kj <= qi + BAND_HALF)
    for j in range(SPLIT * SPLIT // per_tile):
        o, lse = _band_tile(qd16_ref[j], kd16_ref[j], vd16_ref[j], stacked_band)
        od16_ref[0, j] = o
        od16_ref[1, j] = lse

    for n, dst_ref in enumerate((og_ref, lg_ref)):
        for a in range(SPLIT):
            for b in range(SPLIT):
                r = a + SPLIT * b
                tmp_ref[n, a, pl.ds(b, L16, stride=SPLIT), :] = \
                    od16_ref[n, r // per_tile, (r % per_tile) * L16:(r % per_tile + 1) * L16, :]
            dst_ref[2, pl.ds(a, L4, stride=SPLIT), :] = tmp_ref[n, a]

    chunk = BAND_TQ

    def mix(i, carry):
        rows = pl.ds(pl.multiple_of(i * chunk, chunk), chunk)
        lses = [lg_ref[g, rows, :] for g in range(B_GROUPS)]
        lmax = functools.reduce(jnp.maximum, lses)
        es = [jnp.exp(l - lmax) for l in lses]
        num = sum(e * og_ref[g, rows, :] for g, e in enumerate(es))
        ob = num * (1.0 / sum(es))
        y_ref[0, rows, :] = (ob * _silu(gb_ref[0, rows, :].astype(F32))).astype(BF16)
        return carry

    lax.fori_loop(0, S // chunk, mix, 0)


def _band_mix(proj3, tables):
    B, S, NC = proj3.shape
    assert tuple(d for _, d in B_PATTERNS) == (1, SPLIT, SPLIT * SPLIT)
    L4, L16 = S // SPLIT, S // (SPLIT * SPLIT)
    assert L4 % BAND_TQ == 0 and BAND_TQ % L16 == 0
    n16 = S // BAND_TQ

    def head_blk(off, g):
        first = off // HEAD_DIM + g * B_HEADS
        return pl.BlockSpec((1, S, HEAD_DIM), lambda b, h: (b, 0, first + h))

    qkv_specs = [head_blk(off, g) for g in range(B_GROUPS)
                 for off in (_OFF_QB, _OFF_KB, _OFF_VB)]
    gb_first = _OFF_GB // HEAD_DIM
    table = pl.BlockSpec((S, HEAD_DIM), lambda b, h: (0, 0))
    return pl.pallas_call(
        functools.partial(_band_mix_kernel, S=S),
        grid=(B, B_HEADS),
        in_specs=qkv_specs
        + [pl.BlockSpec((1, S, HEAD_DIM), lambda b, h: (b, 0, gb_first + h)), table, table],
        out_specs=pl.BlockSpec((1, S, HEAD_DIM), lambda b, h: (b, 0, h)),
        out_shape=jax.ShapeDtypeStruct((B, S, B_WIDTH), BF16),
        scratch_shapes=[
            pltpu.VMEM((6, S, HEAD_DIM), F32),
            pltpu.VMEM((3, SPLIT, L4, HEAD_DIM), F32),
            pltpu.VMEM((S, HEAD_DIM), BF16),
            pltpu.VMEM((S + 2 * BAND_HALF, HEAD_DIM), BF16),
            pltpu.VMEM((S + 2 * BAND_HALF, HEAD_DIM), BF16),
            pltpu.VMEM((SPLIT, L4, HEAD_DIM), BF16),
            pltpu.VMEM((SPLIT, L4 + 2 * BAND_HALF, HEAD_DIM), BF16),
            pltpu.VMEM((SPLIT, L4 + 2 * BAND_HALF, HEAD_DIM), BF16),
            pltpu.VMEM((n16, BAND_TQ, HEAD_DIM), BF16),
            pltpu.VMEM((n16, BAND_TQ, HEAD_DIM), BF16),
            pltpu.VMEM((n16, BAND_TQ, HEAD_DIM), BF16),
            pltpu.VMEM((2, n16, BAND_TQ, HEAD_DIM), F32),
            pltpu.VMEM((B_GROUPS, S, HEAD_DIM), F32),
            pltpu.VMEM((B_GROUPS, S, HEAD_DIM), F32),
        ],
        compiler_params=pltpu.CompilerParams(
            dimension_semantics=("parallel", "parallel"),
            vmem_limit_bytes=V7X_VMEM_LIMIT_BYTES),
        name="band_mix",
    )(*([proj3] * (3 * B_GROUPS + 1)), *tables)


def _merge_kernel(x_ref, ya_ref, yb_ref, za0, za1, za2, za3, zb0, zb1, zb2, zb3, bias_ref,
                  wa_ref, wb_ref, wo_ref, fg_ref, out_ref, merged_ref, *, final_norm):
    pa = jnp.dot(ya_ref[...], wa_ref[...], preferred_element_type=F32)
    pb = jnp.dot(yb_ref[...], wb_ref[...], preferred_element_type=F32)
    for c, (za, zb) in enumerate(((za0, zb0), (za1, zb1), (za2, zb2), (za3, zb3))):
        cols = slice(c * COL_BLK, (c + 1) * COL_BLK)
        gate_a = jax.nn.sigmoid(za[...].astype(F32) + bias_ref[0:1, cols])
        gate_b = jax.nn.sigmoid(zb[...].astype(F32) + bias_ref[1:2, cols])
        merged_ref[:, cols] = (gate_a * pa[:, cols] + gate_b * pb[:, cols]).astype(BF16)

    y = x_ref[...] + jnp.dot(merged_ref[...], wo_ref[...], preferred_element_type=F32)
    out_ref[...] = _rms(y, fg_ref[...]) if final_norm else y


def _merge(x2, ya, yb, proj2, bias, wa, wb, wo, final_gain, final_norm, tm=512):
    T, D = x2.shape
    za_blk = _OFF_ZA // COL_BLK
    n_z = D // COL_BLK

    def row_blk(width, col=0):
        return pl.BlockSpec((tm, width), lambda i: (i, col))

    def whole(a):
        return pl.BlockSpec(a.shape, lambda i: (0, 0))

    z_specs = [row_blk(COL_BLK, za_blk + c) for c in range(2 * n_z)]
    return pl.pallas_call(
        functools.partial(_merge_kernel, final_norm=final_norm),
        grid=(T // tm,),
        in_specs=[row_blk(D), row_blk(A_WIDTH), row_blk(B_WIDTH)] + z_specs
        + [whole(bias), whole(wa), whole(wb), whole(wo), whole(final_gain)],
        out_specs=row_blk(D),
        out_shape=jax.ShapeDtypeStruct((T, D), F32),
        scratch_shapes=[pltpu.VMEM((tm, D), BF16)],
        compiler_params=pltpu.CompilerParams(
            dimension_semantics=("parallel",),
            vmem_limit_bytes=V7X_VMEM_LIMIT_BYTES),
        name="merge_output",
    )(x2, ya, yb, *([proj2] * (2 * n_z)), bias, wa, wb, wo, final_gain)


def _angles(pos, dim, theta):
    expo = np.arange(0, dim, 2, dtype=np.float64) / dim
    return pos.astype(np.float64)[:, None] / np.power(float(theta), expo)[None, :]


def _lane_tables(ang_blocks):
    cos, sin = [], []
    for blk in ang_blocks:
        if isinstance(blk, int):
            S = cos[0].shape[0]
            cos.append(np.ones((S, blk)))
            sin.append(np.zeros((S, blk)))
            continue
        c, s = np.cos(blk), np.sin(blk)
        cos += [c, c]
        sin += [-s, s]
    return tuple(jnp.asarray(np.concatenate(t, axis=-1), dtype=F32) for t in (cos, sin))


def _axial_tables(S):
    pos = np.arange(S)
    half = HEAD_DIM // 2
    return _lane_tables([_angles(pos // GRID_W, half, AXIAL_THETA),
                         _angles(pos % GRID_W, half, AXIAL_THETA)])


def _partial_tables(S):
    pos = np.arange(S)
    return _lane_tables([_angles(pos, PARTIAL_ROPE_DIM, ROPE_THETA),
                         HEAD_DIM - PARTIAL_ROPE_DIM])


def kernel(x, norm_gain, w_in, q_norm_gain, k_norm_gain, merge_gate_bias, w_branch_a,
           w_branch_b, w_out, final_norm_gain):
    B, S, D = x.shape
    depth = norm_gain.shape[0]
    n_cols = _OFF_ZA + 2 * D
    assert w_in.shape[2] == n_cols and S % GRID_W == 0
    assert all(w // (2 * d) == BAND_HALF for w, d in B_PATTERNS)
    axial = _axial_tables(S)
    partial = _partial_tables(S)
    x2 = x.reshape(B * S, D)
    for l in range(depth):
        proj2 = _input_projection(x2, norm_gain[l][None, :], w_in[l])
        proj3 = proj2.reshape(B, S, n_cols)
        ya = _attention_a(proj3, q_norm_gain[l][None, :], k_norm_gain[l][None, :], axial)
        yb = _band_mix(proj3, partial)
        x2 = _merge(x2, ya.reshape(B * S, A_WIDTH), yb.reshape(B * S, B_WIDTH), proj2,
                    merge_gate_bias[l], w_branch_a[l].astype(BF16),
                    w_branch_b[l].astype(BF16), w_out[l].astype(BF16),
                    final_norm_gain[None, :], final_norm=(l == depth - 1))
    return x2.reshape(B, S, D)
```

```python
import functools
import math

import numpy as np
import jax
import jax.numpy as jnp
from jax import lax
from jax.experimental import pallas as pl
from jax.experimental.pallas import tpu as pltpu

F32 = jnp.float32
BF16 = jnp.bfloat16

HEAD_DIM = 128
GRID_W = 64
NORM_EPS = 1e-6
A_Q_HEADS = 8
A_KV_HEADS = 2
A_GROUP = A_Q_HEADS // A_KV_HEADS
A_WIDTH = A_Q_HEADS * HEAD_DIM
A_KV_WIDTH = A_KV_HEADS * HEAD_DIM
AXIAL_THETA = 10000.0
AXIAL_SHIFT = HEAD_DIM // 4
MAX_SAFE_SCORE_BOUND = 50.0
SCORE_BOUND_SLACK = 1.01
B_PATTERNS = ((128, 1), (512, 4), (2048, 16))
B_GROUPS = len(B_PATTERNS)
B_HEADS = 4
B_WIDTH = B_HEADS * HEAD_DIM
PARTIAL_ROPE_DIM = HEAD_DIM // 4
PARTIAL_SHIFT = PARTIAL_ROPE_DIM // 2
ROPE_THETA = 500000.0
BAND_HALF = 64
BAND_TQ = 256
BAND_WIN = BAND_TQ + 2 * BAND_HALF
SPLIT = 4
MERGE_ROW_SPLIT = 2
MASK_VALUE = -1e30
LOG2_E = math.log2(math.e)
LN_2 = math.log(2.0)
Q_SCALE = HEAD_DIM ** -0.5 * LOG2_E

COL_BLK = 512
_OFF_QA = 0
_OFF_KA = A_WIDTH
_OFF_VA = _OFF_KA + A_KV_WIDTH
_OFF_GA = _OFF_VA + A_KV_WIDTH
_OFF_QB = _OFF_GA + A_WIDTH
_OFF_KB = _OFF_QB + B_GROUPS * B_WIDTH
_OFF_VB = _OFF_KB + B_GROUPS * B_WIDTH
_OFF_GB = _OFF_VB + B_GROUPS * B_WIDTH
_OFF_ZA = _OFF_GB + B_WIDTH

V7X_VMEM_LIMIT_BYTES = 56 * 1024 * 1024


def _rotary(x, cos, sin, shift):
    lane = lax.broadcasted_iota(jnp.int32, x.shape, 1)
    up = pltpu.roll(x, HEAD_DIM - shift, 1)
    dn = pltpu.roll(x, shift, 1)
    partner = jnp.where(lane % (2 * shift) < shift, up, dn)
    return x * cos + partner * sin


def _rms(x, gain):
    ms = jnp.mean(x * x, axis=-1, keepdims=True)
    return x * lax.rsqrt(ms + NORM_EPS) * gain


def _silu(g):
    return g * jax.nn.sigmoid(g)


def _proj_kernel(x_ref, g_ref, w_ref, o_ref, h_ref, wb_ref, *, n_full, tail):
    j, r = pl.program_id(1), pl.program_id(2)

    def project(width, first_tile):
        cols = slice(0, width)

        def normed_rows():
            if not first_tile:
                return h_ref[r]
            h = _rms(x_ref[...], g_ref[...]).astype(BF16)
            h_ref[r] = h
            return h

        @pl.when(r == 0)
        def _():
            w = w_ref[:, cols].astype(BF16)
            wb_ref[:, cols] = w
            o_ref[:, cols] = jnp.dot(normed_rows(), w, preferred_element_type=F32).astype(BF16)

        @pl.when(r != 0)
        def _():
            o_ref[:, cols] = jnp.dot(normed_rows(), wb_ref[:, cols],
                                     preferred_element_type=F32).astype(BF16)

    assert n_full >= 1

    @pl.when(j == 0)
    def _():
        project(w_ref.shape[1], True)

    @pl.when((j > 0) & (j < n_full))
    def _():
        project(w_ref.shape[1], False)

    if tail:
        @pl.when(j == n_full)
        def _():
            project(tail, False)


def _input_projection(x2, gain, w, tm=1024, tn=2 * COL_BLK, rows_per_group=2):
    T, D = x2.shape
    N = w.shape[1]
    R = rows_per_group
    n_full, tail = divmod(N, tn)

    def x_index(g, j, r):
        return g * R + jnp.where(j == 0, r, R - 1), 0

    return pl.pallas_call(
        functools.partial(_proj_kernel, n_full=n_full, tail=tail),
        grid=(T // (tm * R), pl.cdiv(N, tn), R),
        in_specs=[
            pl.BlockSpec((tm, D), x_index),
            pl.BlockSpec((1, D), lambda g, j, r: (0, 0)),
            pl.BlockSpec((D, tn), lambda g, j, r: (0, j)),
        ],
        out_specs=pl.BlockSpec((tm, tn), lambda g, j, r: (g * R + r, j)),
        out_shape=jax.ShapeDtypeStruct((T, N), BF16),
        scratch_shapes=[pltpu.VMEM((R, tm, D), BF16), pltpu.VMEM((D, tn), BF16)],
        compiler_params=pltpu.CompilerParams(
            dimension_semantics=("parallel", "arbitrary", "arbitrary"),
            vmem_limit_bytes=V7X_VMEM_LIMIT_BYTES),
        name="input_projection",
    )(x2, gain, w)


def _attn_a_kernel(q_ref, qnext_ref, k_ref, v_ref, ga_ref, gq_ref, gk_ref, cos_ref, sin_ref,
                   o_ref, kn_ref, k2max_ref, qs_ref, q2max_ref, *, tq, n_q):
    qi = pl.program_id(2)
    slot = qi % 2

    def max_sq_norm(x_bf16):
        xf = x_bf16.astype(F32)
        return jnp.max(jnp.sum(xf * xf, axis=-1, keepdims=True), axis=0, keepdims=True)

    def prepare_queries(src_ref, tile, dst):
        rows = pl.ds(pl.multiple_of(tile * tq, tq), tq)
        cos, sin = cos_ref[rows, :], sin_ref[rows, :]
        q2max = None
        for h in range(A_GROUP):
            cols = slice(h * HEAD_DIM, (h + 1) * HEAD_DIM)
            q = _rms(src_ref[0, :, cols].astype(F32), gq_ref[...])
            q = (_rotary(q, cos, sin, AXIAL_SHIFT) * Q_SCALE).astype(BF16)
            qs_ref[dst, :, cols] = q
            q2max = max_sq_norm(q) if q2max is None else jnp.maximum(q2max, max_sq_norm(q))
        q2max_ref[dst] = jnp.broadcast_to(q2max, q2max_ref.shape[1:])

    @pl.when(qi == 0)
    def _():
        k = _rms(k_ref[0].astype(F32), gk_ref[...])
        k = _rotary(k, cos_ref[...], sin_ref[...], AXIAL_SHIFT).astype(BF16)
        kn_ref[...] = k
        k2max_ref[...] = jnp.broadcast_to(max_sq_norm(k), k2max_ref.shape)
        prepare_queries(q_ref, 0, 0)

    bound = jnp.sqrt(q2max_ref[slot, 0:1, 0:1] * k2max_ref[0:1, 0:1]) * SCORE_BOUND_SLACK
    bound_is_safe = bound[0, 0] <= MAX_SAFE_SCORE_BOUND

    def attend(shift_of):
        prepare_queries(qnext_ref, jnp.minimum(qi + 1, n_q - 1), 1 - slot)
        kn = kn_ref[...]
        v = v_ref[0]
        for h in range(A_GROUP):
            cols = slice(h * HEAD_DIM, (h + 1) * HEAD_DIM)
            s = lax.dot_general(qs_ref[slot, :, cols], kn, (((1,), (1,)), ((), ())),
                                preferred_element_type=F32)
            p = jnp.exp2(s - shift_of(s))
            l = jnp.sum(p, axis=-1, keepdims=True)
            o = jnp.dot(p.astype(BF16), v, preferred_element_type=F32)
            g = ga_ref[0, :, cols].astype(F32)
            o_ref[0, :, cols] = (o * (1.0 / l) * _silu(g)).astype(BF16)

    @pl.when(bound_is_safe)
    def _():
        attend(lambda s: bound)

    @pl.when(jnp.logical_not(bound_is_safe))
    def _():
        attend(lambda s: jnp.max(s, axis=-1, keepdims=True))


def _attention_a(proj3, q_gain, k_gain, tables, tq=512):
    B, S, _ = proj3.shape
    n_q = S // tq
    gw = A_GROUP * HEAD_DIM
    ka_blk = _OFF_KA // HEAD_DIM
    va_blk = _OFF_VA // HEAD_DIM
    ga_blk = _OFF_GA // gw
    full = pl.BlockSpec((S, HEAD_DIM), lambda b, g, i: (0, 0))
    vec = pl.BlockSpec((1, HEAD_DIM), lambda b, g, i: (0, 0))
    return pl.pallas_call(
        functools.partial(_attn_a_kernel, tq=tq, n_q=n_q),
        grid=(B, A_KV_HEADS, n_q),
        in_specs=[
            pl.BlockSpec((1, tq, gw), lambda b, g, i: (b, i, g)),
            pl.BlockSpec((1, tq, gw), lambda b, g, i: (b, jnp.minimum(i + 1, n_q - 1), g)),
            pl.BlockSpec((1, S, HEAD_DIM), lambda b, g, i: (b, 0, ka_blk + g)),
            pl.BlockSpec((1, S, HEAD_DIM), lambda b, g, i: (b, 0, va_blk + g)),
            pl.BlockSpec((1, tq, gw), lambda b, g, i: (b, i, ga_blk + g)),
            vec, vec, full, full,
        ],
        out_specs=pl.BlockSpec((1, tq, gw), lambda b, g, i: (b, i, g)),
        out_shape=jax.ShapeDtypeStruct((B, S, A_WIDTH), BF16),
        scratch_shapes=[
            pltpu.VMEM((S, HEAD_DIM), BF16),
            pltpu.VMEM((8, HEAD_DIM), F32),
            pltpu.VMEM((2, tq, gw), BF16),
            pltpu.VMEM((2, 8, HEAD_DIM), F32),
        ],
        compiler_params=pltpu.CompilerParams(
            dimension_semantics=("parallel", "parallel", "arbitrary"),
            vmem_limit_bytes=V7X_VMEM_LIMIT_BYTES),
        name="attention_a",
    )(proj3, proj3, proj3, proj3, proj3, q_gain, k_gain, *tables)


def _band_tile(q, kw, vw, valid):
    s = lax.dot_general(q, kw, (((1,), (1,)), ((), ())), preferred_element_type=F32)
    s = jnp.where(valid, s, MASK_VALUE)
    m = jnp.max(s, axis=-1, keepdims=True)
    p = jnp.exp2(s - m)
    l = jnp.sum(p, axis=-1, keepdims=True)
    o = jnp.dot(p.astype(BF16), vw, preferred_element_type=F32)
    lse = m * LN_2 + jnp.log(l)
    return o * (1.0 / l), jnp.broadcast_to(lse, (BAND_TQ, HEAD_DIM))


def _window_mask(t, n_tiles):
    qi = lax.broadcasted_iota(jnp.int32, (BAND_TQ, BAND_WIN), 0)
    kj = lax.broadcasted_iota(jnp.int32, (BAND_TQ, BAND_WIN), 1)
    kpos = kj + (t * BAND_TQ - BAND_HALF)
    return (kj >= qi) & (kj <= qi + 2 * BAND_HALF) & (kpos >= 0) & (kpos < n_tiles * BAND_TQ)


def _zero_pads(ref, lead, n_rows):
    zeros = jnp.zeros((BAND_HALF, HEAD_DIM), BF16)
    for idx in lead:
        ref[idx + (slice(0, BAND_HALF),)] = zeros
        ref[idx + (slice(BAND_HALF + n_rows, 2 * BAND_HALF + n_rows),)] = zeros


def _band_mix_kernel(q1_ref, k1_ref, v1_ref, q4_ref, k4_ref, v4_ref, q16_ref, k16_ref, v16_ref,
                     gb_ref, cos_ref, sin_ref, perm_ref, y_ref,
                     nat_ref, tmp_ref, qd1_ref, kp1_ref, vp1_ref, qd4_ref, kp4_ref, vp4_ref,
                     qd16_ref, kd16_ref, vd16_ref, od16_ref, og_ref, lg_ref, *, S):
    L4 = S // SPLIT
    L16 = L4 // SPLIT
    n1, n4 = S // BAND_TQ, L4 // BAND_TQ
    per_tile = BAND_TQ // L16

    def rotated(src_ref, scale=None):
        x = src_ref[0]
        partner = jnp.dot(x, perm_ref[...], preferred_element_type=F32)
        y = x.astype(F32) * cos_ref[...] + partner * sin_ref[...]
        return y if scale is None else y * scale

    _zero_pads(kp1_ref, [()], S)
    _zero_pads(vp1_ref, [()], S)
    qd1_ref[...] = rotated(q1_ref, Q_SCALE).astype(BF16)
    kp1_ref[BAND_HALF:BAND_HALF + S, :] = rotated(k1_ref).astype(BF16)
    vp1_ref[BAND_HALF:BAND_HALF + S, :] = v1_ref[0]

    for t in range(n1):
        rows = slice(t * BAND_TQ, (t + 1) * BAND_TQ)
        win = slice(t * BAND_TQ, t * BAND_TQ + BAND_WIN)
        o, lse = _band_tile(qd1_ref[rows, :], kp1_ref[win, :], vp1_ref[win, :],
                            _window_mask(t, n1))
        og_ref[0, rows, :] = o
        lg_ref[0, rows, :] = lse

    _zero_pads(kp4_ref, [(r,) for r in range(SPLIT)], L4)
    _zero_pads(vp4_ref, [(r,) for r in range(SPLIT)], L4)
    for n, (val, dst_ref, lead) in enumerate(((rotated(q4_ref, Q_SCALE), qd4_ref, 0),
                                              (rotated(k4_ref), kp4_ref, BAND_HALF),
                                              (v4_ref[0].astype(F32), vp4_ref, BAND_HALF))):
        nat_ref[n] = val
        for r in range(SPLIT):
            dst_ref[r, lead:lead + L4, :] = nat_ref[n, pl.ds(r, L4, stride=SPLIT), :].astype(BF16)

    for r in range(SPLIT):
        for t in range(n4):
            rows = pl.ds(r + t * (BAND_TQ * SPLIT), BAND_TQ, stride=SPLIT)
            win = slice(t * BAND_TQ, t * BAND_TQ + BAND_WIN)
            o, lse = _band_tile(qd4_ref[r, t * BAND_TQ:(t + 1) * BAND_TQ, :],
                                kp4_ref[r, win, :], vp4_ref[r, win, :], _window_mask(t, n4))
            og_ref[1, rows, :] = o
            lg_ref[1, rows, :] = lse

    for n, (val, dst_ref) in enumerate(((rotated(q16_ref, Q_SCALE), qd16_ref),
                                        (rotated(k16_ref), kd16_ref),
                                        (v16_ref[0].astype(F32), vd16_ref))):
        nat_ref[3 + n] = val
        for a in range(SPLIT):
            tmp_ref[n, a] = nat_ref[3 + n, pl.ds(a, L4, stride=SPLIT), :]
            for b in range(SPLIT):
                r = a + SPLIT * b
                dst_ref[r // per_tile, (r % per_tile) * L16:(r % per_tile + 1) * L16, :] = \
                    tmp_ref[n, a, pl.ds(b, L16, stride=SPLIT), :].astype(BF16)

    qi = lax.broadcasted_iota(jnp.int32, (BAND_TQ, BAND_TQ), 0)
    kj = lax.broadcasted_iota(jnp.int32, (BAND_TQ, BAND_TQ), 1)
    same_residue = functools.reduce(
        jnp.logical_or, [(qi >= c * L16) & (qi < (c + 1) * L16) & (kj >= c * L16)
                         & (kj < (c + 1) * L16) for c in range(per_tile)])
    stacked_band = same_residue & (kj >= qi - BAND_HALF) & (kj <= qi + BAND_HALF)
    for j in range(SPLIT * SPLIT // per_tile):
        o, lse = _band_tile(qd16_ref[j], kd16_ref[j], vd16_ref[j], stacked_band)
        od16_ref[0, j] = o
        od16_ref[1, j] = lse

    for n, dst_ref in enumerate((og_ref, lg_ref)):
        for a in range(SPLIT):
            for b in range(SPLIT):
                r = a + SPLIT * b
                tmp_ref[n, a, pl.ds(b, L16, stride=SPLIT), :] = \
                    od16_ref[n, r // per_tile, (r % per_tile) * L16:(r % per_tile + 1) * L16, :]
            dst_ref[2, pl.ds(a, L4, stride=SPLIT), :] = tmp_ref[n, a]

    chunk = BAND_TQ

    def mix(i, carry):
        rows = pl.ds(pl.multiple_of(i * chunk, chunk), chunk)
        lses = [lg_ref[g, rows, :] for g in range(B_GROUPS)]
        lmax = functools.reduce(jnp.maximum, lses)
        es = [jnp.exp(l - lmax) for l in lses]
        num = sum(e * og_ref[g, rows, :] for g, e in enumerate(es))
        ob = num * (1.0 / sum(es))
        y_ref[0, rows, :] = (ob * _silu(gb_ref[0, rows, :].astype(F32))).astype(BF16)
        return carry

    lax.fori_loop(0, S // chunk, mix, 0)


def _band_mix(proj3, tables):
    B, S, NC = proj3.shape
    assert tuple(d for _, d in B_PATTERNS) == (1, SPLIT, SPLIT * SPLIT)
    L4, L16 = S // SPLIT, S // (SPLIT * SPLIT)
    assert L4 % BAND_TQ == 0 and BAND_TQ % L16 == 0
    n16 = S // BAND_TQ

    def head_blk(off, g):
        first = off // HEAD_DIM + g * B_HEADS
        return pl.BlockSpec((1, S, HEAD_DIM), lambda b, h: (b, 0, first + h))

    qkv_specs = [head_blk(off, g) for g in range(B_GROUPS)
                 for off in (_OFF_QB, _OFF_KB, _OFF_VB)]
    gb_first = _OFF_GB // HEAD_DIM
    table = pl.BlockSpec((S, HEAD_DIM), lambda b, h: (0, 0))
    return pl.pallas_call(
        functools.partial(_band_mix_kernel, S=S),
        grid=(B, B_HEADS),
        in_specs=qkv_specs
        + [pl.BlockSpec((1, S, HEAD_DIM), lambda b, h: (b, 0, gb_first + h)), table, table,
           pl.BlockSpec((HEAD_DIM, HEAD_DIM), lambda b, h: (0, 0))],
        out_specs=pl.BlockSpec((1, S, HEAD_DIM), lambda b, h: (b, 0, h)),
        out_shape=jax.ShapeDtypeStruct((B, S, B_WIDTH), BF16),
        scratch_shapes=[
            pltpu.VMEM((6, S, HEAD_DIM), F32),
            pltpu.VMEM((3, SPLIT, L4, HEAD_DIM), F32),
            pltpu.VMEM((S, HEAD_DIM), BF16),
            pltpu.VMEM((S + 2 * BAND_HALF, HEAD_DIM), BF16),
            pltpu.VMEM((S + 2 * BAND_HALF, HEAD_DIM), BF16),
            pltpu.VMEM((SPLIT, L4, HEAD_DIM), BF16),
            pltpu.VMEM((SPLIT, L4 + 2 * BAND_HALF, HEAD_DIM), BF16),
            pltpu.VMEM((SPLIT, L4 + 2 * BAND_HALF, HEAD_DIM), BF16),
            pltpu.VMEM((n16, BAND_TQ, HEAD_DIM), BF16),
            pltpu.VMEM((n16, BAND_TQ, HEAD_DIM), BF16),
            pltpu.VMEM((n16, BAND_TQ, HEAD_DIM), BF16),
            pltpu.VMEM((2, n16, BAND_TQ, HEAD_DIM), F32),
            pltpu.VMEM((B_GROUPS, S, HEAD_DIM), F32),
            pltpu.VMEM((B_GROUPS, S, HEAD_DIM), F32),
        ],
        compiler_params=pltpu.CompilerParams(
            dimension_semantics=("parallel", "parallel"),
            vmem_limit_bytes=V7X_VMEM_LIMIT_BYTES),
        name="band_mix",
    )(*([proj3] * (3 * B_GROUPS + 1)), *tables, _partner_permutation(PARTIAL_ROPE_DIM))


def _merge_kernel(x_ref, ya_ref, yb_ref, za0, za1, za2, za3, zb0, zb1, zb2, zb3, bias_ref,
                  wa_ref, wb_ref, wo_ref, fg_ref, out_ref, merged_ref, *, final_norm):
    half = x_ref.shape[0] // MERGE_ROW_SPLIT
    for i in range(MERGE_ROW_SPLIT):
        rows = slice(i * half, (i + 1) * half)
        pa = jnp.dot(ya_ref[rows, :], wa_ref[...], preferred_element_type=F32)
        pb = jnp.dot(yb_ref[rows, :], wb_ref[...], preferred_element_type=F32)
        for c, (za, zb) in enumerate(((za0, zb0), (za1, zb1), (za2, zb2), (za3, zb3))):
            cols = slice(c * COL_BLK, (c + 1) * COL_BLK)
            gate_a = jax.nn.sigmoid(za[rows, :].astype(F32) + bias_ref[0:1, cols])
            gate_b = jax.nn.sigmoid(zb[rows, :].astype(F32) + bias_ref[1:2, cols])
            merged_ref[rows, cols] = (gate_a * pa[:, cols] + gate_b * pb[:, cols]).astype(BF16)

        y = x_ref[rows, :] + jnp.dot(merged_ref[rows, :], wo_ref[...],
                                     preferred_element_type=F32)
        out_ref[rows, :] = _rms(y, fg_ref[...]) if final_norm else y


def _merge(x2, ya, yb, proj2, bias, wa, wb, wo, final_gain, final_norm, tm=512):
    T, D = x2.shape
    za_blk = _OFF_ZA // COL_BLK
    n_z = D // COL_BLK

    def row_blk(width, col=0):
        return pl.BlockSpec((tm, width), lambda i: (i, col))

    def whole(a):
        return pl.BlockSpec(a.shape, lambda i: (0, 0))

    z_specs = [row_blk(COL_BLK, za_blk + c) for c in range(2 * n_z)]
    return pl.pallas_call(
        functools.partial(_merge_kernel, final_norm=final_norm),
        grid=(T // tm,),
        in_specs=[row_blk(D), row_blk(A_WIDTH), row_blk(B_WIDTH)] + z_specs
        + [whole(bias), whole(wa), whole(wb), whole(wo), whole(final_gain)],
        out_specs=row_blk(D),
        out_shape=jax.ShapeDtypeStruct((T, D), F32),
        scratch_shapes=[pltpu.VMEM((tm, D), BF16)],
        compiler_params=pltpu.CompilerParams(
            dimension_semantics=("parallel",),
            vmem_limit_bytes=V7X_VMEM_LIMIT_BYTES),
        name="merge_output",
    )(x2, ya, yb, *([proj2] * (2 * n_z)), bias, wa, wb, wo, final_gain)


def _angles(pos, dim, theta):
    expo = np.arange(0, dim, 2, dtype=np.float64) / dim
    return pos.astype(np.float64)[:, None] / np.power(float(theta), expo)[None, :]


def _lane_tables(ang_blocks):
    cos, sin = [], []
    for blk in ang_blocks:
        if isinstance(blk, int):
            S = cos[0].shape[0]
            cos.append(np.ones((S, blk)))
            sin.append(np.zeros((S, blk)))
            continue
        c, s = np.cos(blk), np.sin(blk)
        cos += [c, c]
        sin += [-s, s]
    return tuple(jnp.asarray(np.concatenate(t, axis=-1), dtype=F32) for t in (cos, sin))


def _partner_permutation(rotary_dim):
    half = rotary_dim // 2
    p = np.zeros((HEAD_DIM, HEAD_DIM))
    for i in range(rotary_dim):
        p[i + half if i < half else i - half, i] = 1.0
    return jnp.asarray(p, dtype=BF16)


def _axial_tables(S):
    pos = np.arange(S)
    half = HEAD_DIM // 2
    return _lane_tables([_angles(pos // GRID_W, half, AXIAL_THETA),
                         _angles(pos % GRID_W, half, AXIAL_THETA)])


def _partial_tables(S):
    pos = np.arange(S)
    return _lane_tables([_angles(pos, PARTIAL_ROPE_DIM, ROPE_THETA),
                         HEAD_DIM - PARTIAL_ROPE_DIM])


def kernel(x, norm_gain, w_in, q_norm_gain, k_norm_gain, merge_gate_bias, w_branch_a,
           w_branch_b, w_out, final_norm_gain):
    B, S, D = x.shape
    depth = norm_gain.shape[0]
    n_cols = _OFF_ZA + 2 * D
    assert w_in.shape[2] == n_cols and S % GRID_W == 0
    assert all(w // (2 * d) == BAND_HALF for w, d in B_PATTERNS)
    axial = _axial_tables(S)
    partial = _partial_tables(S)
    x2 = x.reshape(B * S, D)
    for l in range(depth):
        proj2 = _input_projection(x2, norm_gain[l][None, :], w_in[l])
        proj3 = proj2.reshape(B, S, n_cols)
        ya = _attention_a(proj3, q_norm_gain[l][None, :], k_norm_gain[l][None, :], axial)
        yb = _band_mix(proj3, partial)
        x2 = _merge(x2, ya.reshape(B * S, A_WIDTH), yb.reshape(B * S, B_WIDTH), proj2,
                    merge_gate_bias[l], w_branch_a[l].astype(BF16),
                    w_branch_b[l].astype(BF16), w_out[l].astype(BF16),
                    final_norm_gain[None, :], final_norm=(l == depth - 1))
    return x2.reshape(B, S, D)
```

```python
import functools
import math

import numpy as np
import jax
import jax.numpy as jnp
from jax import lax
from jax.experimental import pallas as pl
from jax.experimental.pallas import tpu as pltpu

F32 = jnp.float32
BF16 = jnp.bfloat16

HEAD_DIM = 128
GRID_W = 64
NORM_EPS = 1e-6
A_Q_HEADS = 8
A_KV_HEADS = 2
A_GROUP = A_Q_HEADS // A_KV_HEADS
A_WIDTH = A_Q_HEADS * HEAD_DIM
A_KV_WIDTH = A_KV_HEADS * HEAD_DIM
AXIAL_THETA = 10000.0
AXIAL_SHIFT = HEAD_DIM // 4
MAX_SAFE_SCORE_BOUND = 50.0
SCORE_BOUND_SLACK = 1.01
B_PATTERNS = ((128, 1), (512, 4), (2048, 16))
B_GROUPS = len(B_PATTERNS)
B_HEADS = 4
B_WIDTH = B_HEADS * HEAD_DIM
PARTIAL_ROPE_DIM = HEAD_DIM // 4
PARTIAL_SHIFT = PARTIAL_ROPE_DIM // 2
ROPE_THETA = 500000.0
BAND_HALF = 64
BAND_TQ = 256
BAND_WIN = BAND_TQ + 2 * BAND_HALF
SPLIT = 4
MASK_VALUE = -1e30
LOG2_E = math.log2(math.e)
LN_2 = math.log(2.0)
Q_SCALE = HEAD_DIM ** -0.5 * LOG2_E

COL_BLK = 512
_OFF_QA = 0
_OFF_KA = A_WIDTH
_OFF_VA = _OFF_KA + A_KV_WIDTH
_OFF_GA = _OFF_VA + A_KV_WIDTH
_OFF_QB = _OFF_GA + A_WIDTH
_OFF_KB = _OFF_QB + B_GROUPS * B_WIDTH
_OFF_VB = _OFF_KB + B_GROUPS * B_WIDTH
_OFF_GB = _OFF_VB + B_GROUPS * B_WIDTH
_OFF_ZA = _OFF_GB + B_WIDTH

V7X_VMEM_LIMIT_BYTES = 56 * 1024 * 1024


def _rotary(x, cos, sin, shift):
    lane = lax.broadcasted_iota(jnp.int32, x.shape, 1)
    up = pltpu.roll(x, HEAD_DIM - shift, 1)
    dn = pltpu.roll(x, shift, 1)
    partner = jnp.where(lane % (2 * shift) < shift, up, dn)
    return x * cos + partner * sin


def _rms(x, gain):
    ms = jnp.mean(x * x, axis=-1, keepdims=True)
    return x * lax.rsqrt(ms + NORM_EPS) * gain


def _silu(g):
    return g * jax.nn.sigmoid(g)


def _proj_kernel(x_ref, g_ref, w_ref, o_ref, h_ref, wb_ref, *, n_full, tail):
    j, r = pl.program_id(1), pl.program_id(2)

    def project(width, first_tile):
        cols = slice(0, width)

        def normed_rows():
            if not first_tile:
                return h_ref[r]
            h = _rms(x_ref[...], g_ref[...]).astype(BF16)
            h_ref[r] = h
            return h

        @pl.when(r == 0)
        def _():
            w = w_ref[:, cols].astype(BF16)
            wb_ref[:, cols] = w
            o_ref[:, cols] = jnp.dot(normed_rows(), w, preferred_element_type=F32).astype(BF16)

        @pl.when(r != 0)
        def _():
            o_ref[:, cols] = jnp.dot(normed_rows(), wb_ref[:, cols],
                                     preferred_element_type=F32).astype(BF16)

    assert n_full >= 1

    @pl.when(j == 0)
    def _():
        project(w_ref.shape[1], True)

    @pl.when((j > 0) & (j < n_full))
    def _():
        project(w_ref.shape[1], False)

    if tail:
        @pl.when(j == n_full)
        def _():
            project(tail, False)


def _input_projection(x2, gain, w, tm=1024, tn=2 * COL_BLK, rows_per_group=2):
    T, D = x2.shape
    N = w.shape[1]
    R = rows_per_group
    n_full, tail = divmod(N, tn)

    def x_index(g, j, r):
        return g * R + jnp.where(j == 0, r, R - 1), 0

    return pl.pallas_call(
        functools.partial(_proj_kernel, n_full=n_full, tail=tail),
        grid=(T // (tm * R), pl.cdiv(N, tn), R),
        in_specs=[
            pl.BlockSpec((tm, D), x_index),
            pl.BlockSpec((1, D), lambda g, j, r: (0, 0)),
            pl.BlockSpec((D, tn), lambda g, j, r: (0, j)),
        ],
        out_specs=pl.BlockSpec((tm, tn), lambda g, j, r: (g * R + r, j)),
        out_shape=jax.ShapeDtypeStruct((T, N), BF16),
        scratch_shapes=[pltpu.VMEM((R, tm, D), BF16), pltpu.VMEM((D, tn), BF16)],
        compiler_params=pltpu.CompilerParams(
            dimension_semantics=("parallel", "arbitrary", "arbitrary"),
            vmem_limit_bytes=V7X_VMEM_LIMIT_BYTES),
        name="input_projection",
    )(x2, gain, w)


def _attn_a_kernel(q_ref, qnext_ref, k_ref, v_ref, ga_ref, gq_ref, gk_ref, cos_ref, sin_ref,
                   o_ref, kn_ref, v1_ref, k2max_ref, qs_ref, q2max_ref, *, tq, n_q):
    qi = pl.program_id(2)
    slot = qi % 2

    def max_sq_norm(x_bf16):
        xf = x_bf16.astype(F32)
        return jnp.max(jnp.sum(xf * xf, axis=-1, keepdims=True), axis=0, keepdims=True)

    def prepare_queries(src_ref, tile, dst):
        rows = pl.ds(pl.multiple_of(tile * tq, tq), tq)
        cos, sin = cos_ref[rows, :], sin_ref[rows, :]
        q2max = None
        for h in range(A_GROUP):
            cols = slice(h * HEAD_DIM, (h + 1) * HEAD_DIM)
            q = _rms(src_ref[0, :, cols].astype(F32), gq_ref[...])
            q = (_rotary(q, cos, sin, AXIAL_SHIFT) * Q_SCALE).astype(BF16)
            qs_ref[dst, :, cols] = q
            q2max = max_sq_norm(q) if q2max is None else jnp.maximum(q2max, max_sq_norm(q))
        q2max_ref[dst] = jnp.broadcast_to(q2max, q2max_ref.shape[1:])

    @pl.when(qi == 0)
    def _():
        k = _rms(k_ref[0].astype(F32), gk_ref[...])
        k = _rotary(k, cos_ref[...], sin_ref[...], AXIAL_SHIFT).astype(BF16)
        kn_ref[...] = k
        k2max_ref[...] = jnp.broadcast_to(max_sq_norm(k), k2max_ref.shape)
        v1_ref[:, 0:HEAD_DIM] = v_ref[0]
        v1_ref[:, HEAD_DIM:] = jnp.ones((v1_ref.shape[0], HEAD_DIM), BF16)
        prepare_queries(q_ref, 0, 0)

    bound = jnp.sqrt(q2max_ref[slot, 0:1, 0:1] * k2max_ref[0:1, 0:1]) * SCORE_BOUND_SLACK
    bound_is_safe = bound[0, 0] <= MAX_SAFE_SCORE_BOUND

    def attend(shift_of):
        prepare_queries(qnext_ref, jnp.minimum(qi + 1, n_q - 1), 1 - slot)
        kn = kn_ref[...]
        v1 = v1_ref[...]
        for h in range(A_GROUP):
            cols = slice(h * HEAD_DIM, (h + 1) * HEAD_DIM)
            s = lax.dot_general(qs_ref[slot, :, cols], kn, (((1,), (1,)), ((), ())),
                                preferred_element_type=F32)
            p = jnp.exp2(s - shift_of(s))
            pv = jnp.dot(p.astype(BF16), v1, preferred_element_type=F32)
            o, l = pv[:, :HEAD_DIM], pv[:, HEAD_DIM:]
            g = ga_ref[0, :, cols].astype(F32)
            o_ref[0, :, cols] = (o * (1.0 / l) * _silu(g)).astype(BF16)

    @pl.when(bound_is_safe)
    def _():
        attend(lambda s: bound)

    @pl.when(jnp.logical_not(bound_is_safe))
    def _():
        attend(lambda s: jnp.max(s, axis=-1, keepdims=True))


def _attention_a(proj3, q_gain, k_gain, tables, tq=512):
    B, S, _ = proj3.shape
    n_q = S // tq
    gw = A_GROUP * HEAD_DIM
    ka_blk = _OFF_KA // HEAD_DIM
    va_blk = _OFF_VA // HEAD_DIM
    ga_blk = _OFF_GA // gw
    full = pl.BlockSpec((S, HEAD_DIM), lambda b, g, i: (0, 0))
    vec = pl.BlockSpec((1, HEAD_DIM), lambda b, g, i: (0, 0))
    return pl.pallas_call(
        functools.partial(_attn_a_kernel, tq=tq, n_q=n_q),
        grid=(B, A_KV_HEADS, n_q),
        in_specs=[
            pl.BlockSpec((1, tq, gw), lambda b, g, i: (b, i, g)),
            pl.BlockSpec((1, tq, gw), lambda b, g, i: (b, jnp.minimum(i + 1, n_q - 1), g)),
            pl.BlockSpec((1, S, HEAD_DIM), lambda b, g, i: (b, 0, ka_blk + g)),
            pl.BlockSpec((1, S, HEAD_DIM), lambda b, g, i: (b, 0, va_blk + g)),
            pl.BlockSpec((1, tq, gw), lambda b, g, i: (b, i, ga_blk + g)),
            vec, vec, full, full,
        ],
        out_specs=pl.BlockSpec((1, tq, gw), lambda b, g, i: (b, i, g)),
        out_shape=jax.ShapeDtypeStruct((B, S, A_WIDTH), BF16),
        scratch_shapes=[
            pltpu.VMEM((S, HEAD_DIM), BF16),
            pltpu.VMEM((S, 2 * HEAD_DIM), BF16),
            pltpu.VMEM((8, HEAD_DIM), F32),
            pltpu.VMEM((2, tq, gw), BF16),
            pltpu.VMEM((2, 8, HEAD_DIM), F32),
        ],
        compiler_params=pltpu.CompilerParams(
            dimension_semantics=("parallel", "parallel", "arbitrary"),
            vmem_limit_bytes=V7X_VMEM_LIMIT_BYTES),
        name="attention_a",
    )(proj3, proj3, proj3, proj3, proj3, q_gain, k_gain, *tables)


def _band_tile(q, kw, vw, valid):
    s = lax.dot_general(q, kw, (((1,), (1,)), ((), ())), preferred_element_type=F32)
    s = jnp.where(valid, s, MASK_VALUE)
    m = jnp.max(s, axis=-1, keepdims=True)
    p = jnp.exp2(s - m)
    pv = jnp.dot(p.astype(BF16), vw, preferred_element_type=F32)
    o, l = pv[:, :HEAD_DIM], pv[:, HEAD_DIM:]
    return o * (1.0 / l), m * LN_2 + jnp.log(l)


def _window_mask(t, n_tiles):
    qi = lax.broadcasted_iota(jnp.int32, (BAND_TQ, BAND_WIN), 0)
    kj = lax.broadcasted_iota(jnp.int32, (BAND_TQ, BAND_WIN), 1)
    kpos = kj + (t * BAND_TQ - BAND_HALF)
    return (kj >= qi) & (kj <= qi + 2 * BAND_HALF) & (kpos >= 0) & (kpos < n_tiles * BAND_TQ)


def _zero_pads(ref, lead, n_rows):
    zeros = jnp.zeros((BAND_HALF, HEAD_DIM), BF16)
    lanes = slice(0, HEAD_DIM)
    for idx in lead:
        ref[idx + (slice(0, BAND_HALF), lanes)] = zeros
        ref[idx + (slice(BAND_HALF + n_rows, 2 * BAND_HALF + n_rows), lanes)] = zeros


def _band_mix_kernel(q1_ref, k1_ref, v1_ref, q4_ref, k4_ref, v4_ref, q16_ref, k16_ref, v16_ref,
                     gb_ref, cos_ref, sin_ref, perm_ref, y_ref,
                     nat_ref, tmp_ref, qd1_ref, kp1_ref, vp1_ref, qd4_ref, kp4_ref, vp4_ref,
                     qd16_ref, kd16_ref, vd16_ref, od16_ref, og_ref, lg_ref, *, S):
    L4 = S // SPLIT
    L16 = L4 // SPLIT
    n1, n4 = S // BAND_TQ, L4 // BAND_TQ
    per_tile = BAND_TQ // L16

    def rotated(src_ref, scale=None):
        x = src_ref[0]
        partner = jnp.dot(x, perm_ref[...], preferred_element_type=F32)
        y = x.astype(F32) * cos_ref[...] + partner * sin_ref[...]
        return y if scale is None else y * scale

    lanes = slice(0, HEAD_DIM)
    vp1_ref[:, HEAD_DIM:] = jnp.ones((vp1_ref.shape[0], HEAD_DIM), BF16)
    vp4_ref[:, :, HEAD_DIM:] = jnp.ones(vp4_ref.shape[:2] + (HEAD_DIM,), BF16)
    vd16_ref[:, :, HEAD_DIM:] = jnp.ones(vd16_ref.shape[:2] + (HEAD_DIM,), BF16)

    _zero_pads(kp1_ref, [()], S)
    _zero_pads(vp1_ref, [()], S)
    qd1_ref[...] = rotated(q1_ref, Q_SCALE).astype(BF16)
    kp1_ref[BAND_HALF:BAND_HALF + S, :] = rotated(k1_ref).astype(BF16)
    vp1_ref[BAND_HALF:BAND_HALF + S, lanes] = v1_ref[0]

    for t in range(n1):
        rows = slice(t * BAND_TQ, (t + 1) * BAND_TQ)
        win = slice(t * BAND_TQ, t * BAND_TQ + BAND_WIN)
        o, lse = _band_tile(qd1_ref[rows, :], kp1_ref[win, :], vp1_ref[win, :],
                            _window_mask(t, n1))
        og_ref[0, rows, :] = o
        lg_ref[0, rows, :] = lse

    _zero_pads(kp4_ref, [(r,) for r in range(SPLIT)], L4)
    _zero_pads(vp4_ref, [(r,) for r in range(SPLIT)], L4)
    for n, (val, dst_ref, lead) in enumerate(((rotated(q4_ref, Q_SCALE), qd4_ref, 0),
                                              (rotated(k4_ref), kp4_ref, BAND_HALF),
                                              (v4_ref[0].astype(F32), vp4_ref, BAND_HALF))):
        nat_ref[n] = val
        for r in range(SPLIT):
            dst_ref[r, lead:lead + L4, lanes] = \
                nat_ref[n, pl.ds(r, L4, stride=SPLIT), :].astype(BF16)

    for r in range(SPLIT):
        for t in range(n4):
            rows = pl.ds(r + t * (BAND_TQ * SPLIT), BAND_TQ, stride=SPLIT)
            win = slice(t * BAND_TQ, t * BAND_TQ + BAND_WIN)
            o, lse = _band_tile(qd4_ref[r, t * BAND_TQ:(t + 1) * BAND_TQ, :],
                                kp4_ref[r, win, :], vp4_ref[r, win, :], _window_mask(t, n4))
            og_ref[1, rows, :] = o
            lg_ref[1, rows, :] = lse

    for n, (val, dst_ref) in enumerate(((rotated(q16_ref, Q_SCALE), qd16_ref),
                                        (rotated(k16_ref), kd16_ref),
                                        (v16_ref[0].astype(F32), vd16_ref))):
        nat_ref[3 + n] = val
        for a in range(SPLIT):
            tmp_ref[n, a] = nat_ref[3 + n, pl.ds(a, L4, stride=SPLIT), :]
            for b in range(SPLIT):
                r = a + SPLIT * b
                dst_ref[r // per_tile, (r % per_tile) * L16:(r % per_tile + 1) * L16, lanes] = \
                    tmp_ref[n, a, pl.ds(b, L16, stride=SPLIT), :].astype(BF16)

    qi = lax.broadcasted_iota(jnp.int32, (BAND_TQ, BAND_TQ), 0)
    kj = lax.broadcasted_iota(jnp.int32, (BAND_TQ, BAND_TQ), 1)
    same_residue = functools.reduce(
        jnp.logical_or, [(qi >= c * L16) & (qi < (c + 1) * L16) & (kj >= c * L16)
                         & (kj < (c + 1) * L16) for c in range(per_tile)])
    stacked_band = same_residue & (kj >= qi - BAND_HALF) & (kj <= qi + BAND_HALF)
    for j in range(SPLIT * SPLIT // per_tile):
        o, lse = _band_tile(qd16_ref[j], kd16_ref[j], vd16_ref[j], stacked_band)
        od16_ref[0, j] = o
        od16_ref[1, j] = lse

    for n, dst_ref in enumerate((og_ref, lg_ref)):
        for a in range(SPLIT):
            for b in range(SPLIT):
                r = a + SPLIT * b
                tmp_ref[n, a, pl.ds(b, L16, stride=SPLIT), :] = \
                    od16_ref[n, r // per_tile, (r % per_tile) * L16:(r % per_tile + 1) * L16, :]
            dst_ref[2, pl.ds(a, L4, stride=SPLIT), :] = tmp_ref[n, a]

    chunk = BAND_TQ

    def mix(i, carry):
        rows = pl.ds(pl.multiple_of(i * chunk, chunk), chunk)
        lses = [lg_ref[g, rows, :] for g in range(B_GROUPS)]
        lmax = functools.reduce(jnp.maximum, lses)
        es = [jnp.exp(l - lmax) for l in lses]
        num = sum(e * og_ref[g, rows, :] for g, e in enumerate(es))
        ob = num * (1.0 / sum(es))
        y_ref[0, rows, :] = (ob * _silu(gb_ref[0, rows, :].astype(F32))).astype(BF16)
        return carry

    lax.fori_loop(0, S // chunk, mix, 0)


def _band_mix(proj3, tables):
    B, S, NC = proj3.shape
    assert tuple(d for _, d in B_PATTERNS) == (1, SPLIT, SPLIT * SPLIT)
    L4, L16 = S // SPLIT, S // (SPLIT * SPLIT)
    assert L4 % BAND_TQ == 0 and BAND_TQ % L16 == 0
    n16 = S // BAND_TQ

    def head_blk(off, g):
        first = off // HEAD_DIM + g * B_HEADS
        return pl.BlockSpec((1, S, HEAD_DIM), lambda b, h: (b, 0, first + h))

    qkv_specs = [head_blk(off, g) for g in range(B_GROUPS)
                 for off in (_OFF_QB, _OFF_KB, _OFF_VB)]
    gb_first = _OFF_GB // HEAD_DIM
    table = pl.BlockSpec((S, HEAD_DIM), lambda b, h: (0, 0))
    return pl.pallas_call(
        functools.partial(_band_mix_kernel, S=S),
        grid=(B, B_HEADS),
        in_specs=qkv_specs
        + [pl.BlockSpec((1, S, HEAD_DIM), lambda b, h: (b, 0, gb_first + h)), table, table,
           pl.BlockSpec((HEAD_DIM, HEAD_DIM), lambda b, h: (0, 0))],
        out_specs=pl.BlockSpec((1, S, HEAD_DIM), lambda b, h: (b, 0, h)),
        out_shape=jax.ShapeDtypeStruct((B, S, B_WIDTH), BF16),
        scratch_shapes=[
            pltpu.VMEM((6, S, HEAD_DIM), F32),
            pltpu.VMEM((3, SPLIT, L4, HEAD_DIM), F32),
            pltpu.VMEM((S, HEAD_DIM), BF16),
            pltpu.VMEM((S + 2 * BAND_HALF, HEAD_DIM), BF16),
            pltpu.VMEM((S + 2 * BAND_HALF, 2 * HEAD_DIM), BF16),
            pltpu.VMEM((SPLIT, L4, HEAD_DIM), BF16),
            pltpu.VMEM((SPLIT, L4 + 2 * BAND_HALF, HEAD_DIM), BF16),
            pltpu.VMEM((SPLIT, L4 + 2 * BAND_HALF, 2 * HEAD_DIM), BF16),
            pltpu.VMEM((n16, BAND_TQ, HEAD_DIM), BF16),
            pltpu.VMEM((n16, BAND_TQ, HEAD_DIM), BF16),
            pltpu.VMEM((n16, BAND_TQ, 2 * HEAD_DIM), BF16),
            pltpu.VMEM((2, n16, BAND_TQ, HEAD_DIM), F32),
            pltpu.VMEM((B_GROUPS, S, HEAD_DIM), F32),
            pltpu.VMEM((B_GROUPS, S, HEAD_DIM), F32),
        ],
        compiler_params=pltpu.CompilerParams(
            dimension_semantics=("parallel", "parallel"),
            vmem_limit_bytes=V7X_VMEM_LIMIT_BYTES),
        name="band_mix",
    )(*([proj3] * (3 * B_GROUPS + 1)), *tables, _partner_permutation(PARTIAL_ROPE_DIM))


def _merge_kernel(x_ref, ya_ref, yb_ref, za0, za1, za2, za3, zb0, zb1, zb2, zb3, bias_ref,
                  wa_ref, wb_ref, wo_ref, fg_ref, out_ref, merged_ref, *, final_norm):
    pa = jnp.dot(ya_ref[...], wa_ref[...], preferred_element_type=F32)
    pb = jnp.dot(yb_ref[...], wb_ref[...], preferred_element_type=F32)
    for c, (za, zb) in enumerate(((za0, zb0), (za1, zb1), (za2, zb2), (za3, zb3))):
        cols = slice(c * COL_BLK, (c + 1) * COL_BLK)
        gate_a = jax.nn.sigmoid(za[...].astype(F32) + bias_ref[0:1, cols])
        gate_b = jax.nn.sigmoid(zb[...].astype(F32) + bias_ref[1:2, cols])
        merged_ref[:, cols] = (gate_a * pa[:, cols] + gate_b * pb[:, cols]).astype(BF16)

    y = x_ref[...] + jnp.dot(merged_ref[...], wo_ref[...], preferred_element_type=F32)
    out_ref[...] = _rms(y, fg_ref[...]) if final_norm else y


def _merge(x2, ya, yb, proj2, bias, wa, wb, wo, final_gain, final_norm, tm=512):
    T, D = x2.shape
    za_blk = _OFF_ZA // COL_BLK
    n_z = D // COL_BLK

    def row_blk(width, col=0):
        return pl.BlockSpec((tm, width), lambda i: (i, col))

    def whole(a):
        return pl.BlockSpec(a.shape, lambda i: (0, 0))

    z_specs = [row_blk(COL_BLK, za_blk + c) for c in range(2 * n_z)]
    return pl.pallas_call(
        functools.partial(_merge_kernel, final_norm=final_norm),
        grid=(T // tm,),
        in_specs=[row_blk(D), row_blk(A_WIDTH), row_blk(B_WIDTH)] + z_specs
        + [whole(bias), whole(wa), whole(wb), whole(wo), whole(final_gain)],
        out_specs=row_blk(D),
        out_shape=jax.ShapeDtypeStruct((T, D), F32),
        scratch_shapes=[pltpu.VMEM((tm, D), BF16)],
        compiler_params=pltpu.CompilerParams(
            dimension_semantics=("parallel",),
            vmem_limit_bytes=V7X_VMEM_LIMIT_BYTES),
        name="merge_output",
    )(x2, ya, yb, *([proj2] * (2 * n_z)), bias, wa, wb, wo, final_gain)


def _angles(pos, dim, theta):
    expo = np.arange(0, dim, 2, dtype=np.float64) / dim
    return pos.astype(np.float64)[:, None] / np.power(float(theta), expo)[None, :]


def _lane_tables(ang_blocks):
    cos, sin = [], []
    for blk in ang_blocks:
        if isinstance(blk, int):
            S = cos[0].shape[0]
            cos.append(np.ones((S, blk)))
            sin.append(np.zeros((S, blk)))
            continue
        c, s = np.cos(blk), np.sin(blk)
        cos += [c, c]
        sin += [-s, s]
    return tuple(jnp.asarray(np.concatenate(t, axis=-1), dtype=F32) for t in (cos, sin))


def _partner_permutation(rotary_dim):
    half = rotary_dim // 2
    p = np.zeros((HEAD_DIM, HEAD_DIM))
    for i in range(rotary_dim):
        p[i + half if i < half else i - half, i] = 1.0
    return jnp.asarray(p, dtype=BF16)


def _axial_tables(S):
    pos = np.arange(S)
    half = HEAD_DIM // 2
    return _lane_tables([_angles(pos // GRID_W, half, AXIAL_THETA),
                         _angles(pos % GRID_W, half, AXIAL_THETA)])


def _partial_tables(S):
    pos = np.arange(S)
    return _lane_tables([_angles(pos, PARTIAL_ROPE_DIM, ROPE_THETA),
                         HEAD_DIM - PARTIAL_ROPE_DIM])


def kernel(x, norm_gain, w_in, q_norm_gain, k_norm_gain, merge_gate_bias, w_branch_a,
           w_branch_b, w_out, final_norm_gain):
    B, S, D = x.shape
    depth = norm_gain.shape[0]
    n_cols = _OFF_ZA + 2 * D
    assert w_in.shape[2] == n_cols and S % GRID_W == 0
    assert all(w // (2 * d) == BAND_HALF for w, d in B_PATTERNS)
    axial = _axial_tables(S)
    partial = _partial_tables(S)
    x2 = x.reshape(B * S, D)
    for l in range(depth):
        proj2 = _input_projection(x2, norm_gain[l][None, :], w_in[l])
        proj3 = proj2.reshape(B, S, n_cols)
        ya = _attention_a(proj3, q_norm_gain[l][None, :], k_norm_gain[l][None, :], axial)
        yb = _band_mix(proj3, partial)
        x2 = _merge(x2, ya.reshape(B * S, A_WIDTH), yb.reshape(B * S, B_WIDTH), proj2,
                    merge_gate_bias[l], w_branch_a[l].astype(BF16),
                    w_branch_b[l].astype(BF16), w_out[l].astype(BF16),
                    final_norm_gain[None, :], final_norm=(l == depth - 1))
    return x2.reshape(B, S, D)
```

```python
import functools
import math

import numpy as np
import jax
import jax.numpy as jnp
from jax import lax
from jax.experimental import pallas as pl
from jax.experimental.pallas import tpu as pltpu

F32 = jnp.float32
BF16 = jnp.bfloat16

HEAD_DIM = 128
GRID_W = 64
NORM_EPS = 1e-6
A_Q_HEADS = 8
A_KV_HEADS = 2
A_GROUP = A_Q_HEADS // A_KV_HEADS
A_WIDTH = A_Q_HEADS * HEAD_DIM
A_KV_WIDTH = A_KV_HEADS * HEAD_DIM
AXIAL_THETA = 10000.0
AXIAL_SHIFT = HEAD_DIM // 4
MAX_SAFE_SCORE_BOUND = 50.0
SCORE_BOUND_SLACK = 1.01
B_PATTERNS = ((128, 1), (512, 4), (2048, 16))
B_GROUPS = len(B_PATTERNS)
B_HEADS = 4
B_WIDTH = B_HEADS * HEAD_DIM
PARTIAL_ROPE_DIM = HEAD_DIM // 4
PARTIAL_SHIFT = PARTIAL_ROPE_DIM // 2
ROPE_THETA = 500000.0
BAND_HALF = 64
BAND_TQ = 256
BAND_WIN = BAND_TQ + 2 * BAND_HALF
SPLIT = 4
MASK_VALUE = -1e30
LOG2_E = math.log2(math.e)
LN_2 = math.log(2.0)
Q_SCALE = HEAD_DIM ** -0.5 * LOG2_E

COL_BLK = 512
_OFF_QA = 0
_OFF_KA = A_WIDTH
_OFF_VA = _OFF_KA + A_KV_WIDTH
_OFF_GA = _OFF_VA + A_KV_WIDTH
_OFF_QB = _OFF_GA + A_WIDTH
_OFF_KB = _OFF_QB + B_GROUPS * B_WIDTH
_OFF_VB = _OFF_KB + B_GROUPS * B_WIDTH
_OFF_GB = _OFF_VB + B_GROUPS * B_WIDTH
_OFF_ZA = _OFF_GB + B_WIDTH

V7X_VMEM_LIMIT_BYTES = 56 * 1024 * 1024
PROJ_ROW_TILE = 1024
PROJ_COL_TILE = 2 * COL_BLK
PROJ_ROW_TILES_PER_WEIGHT_TILE = 2
CAST_CHUNK_ROWS = 128
ATTN_Q_TILE = 512
MERGE_ROW_TILE = 512


def _rotary(x, cos, sin, shift):
    lane = lax.broadcasted_iota(jnp.int32, x.shape, 1)
    up = pltpu.roll(x, HEAD_DIM - shift, 1)
    dn = pltpu.roll(x, shift, 1)
    partner = jnp.where(lane % (2 * shift) < shift, up, dn)
    return x * cos + partner * sin


def _rms(x, gain):
    ms = jnp.mean(x * x, axis=-1, keepdims=True)
    return x * lax.rsqrt(ms + NORM_EPS) * gain


def _silu(g):
    return g * jax.nn.sigmoid(g)


def _proj_kernel(x_ref, g_ref, w_ref, o_ref, h_ref, wb_ref, *, n_full, tail):
    j, r = pl.program_id(1), pl.program_id(2)

    def project(width, first_tile):
        cols = slice(0, width)

        def normed_rows():
            if not first_tile:
                return h_ref[r]
            h = _rms(x_ref[...], g_ref[...]).astype(BF16)
            h_ref[r] = h
            return h

        @pl.when(r == 0)
        def _():
            w = w_ref[:, cols].astype(BF16)
            wb_ref[:, cols] = w
            o_ref[:, cols] = jnp.dot(normed_rows(), w, preferred_element_type=F32).astype(BF16)

        @pl.when(r != 0)
        def _():
            o_ref[:, cols] = jnp.dot(normed_rows(), wb_ref[:, cols],
                                     preferred_element_type=F32).astype(BF16)

    assert n_full >= 1

    @pl.when(j == 0)
    def _():
        project(w_ref.shape[1], True)

    @pl.when((j > 0) & (j < n_full))
    def _():
        project(w_ref.shape[1], False)

    if tail:
        @pl.when(j == n_full)
        def _():
            project(tail, False)


def _input_projection(x2, gain, w):
    T, D = x2.shape
    N = w.shape[1]
    tm, tn, R = PROJ_ROW_TILE, PROJ_COL_TILE, PROJ_ROW_TILES_PER_WEIGHT_TILE
    n_full, tail = divmod(N, tn)

    def x_index(g, j, r):
        return g * R + jnp.where(j == 0, r, R - 1), 0

    return pl.pallas_call(
        functools.partial(_proj_kernel, n_full=n_full, tail=tail),
        grid=(T // (tm * R), pl.cdiv(N, tn), R),
        in_specs=[
            pl.BlockSpec((tm, D), x_index),
            pl.BlockSpec((1, D), lambda g, j, r: (0, 0)),
            pl.BlockSpec((D, tn), lambda g, j, r: (0, j)),
        ],
        out_specs=pl.BlockSpec((tm, tn), lambda g, j, r: (g * R + r, j)),
        out_shape=jax.ShapeDtypeStruct((T, N), BF16),
        scratch_shapes=[pltpu.VMEM((R, tm, D), BF16), pltpu.VMEM((D, tn), BF16)],
        compiler_params=pltpu.CompilerParams(
            dimension_semantics=("parallel", "arbitrary", "arbitrary"),
            vmem_limit_bytes=V7X_VMEM_LIMIT_BYTES),
        name="input_projection",
    )(x2, gain, w)


def _attn_a_kernel(q_ref, qnext_ref, k_ref, v_ref, ga_ref, gq_ref, gk_ref, cos_ref, sin_ref,
                   *refs, tq, n_q, cast_plan):
    n_cast = len(cast_plan)
    cast_in, o_ref, cast_out = refs[:n_cast], refs[n_cast], refs[n_cast + 1:2 * n_cast + 1]
    kn_ref, v1_ref, k2max_ref, qs_ref, q2max_ref = refs[2 * n_cast + 1:]
    qi = pl.program_id(2)
    slot = qi % 2
    step = (pl.program_id(0) * pl.num_programs(1) + pl.program_id(1)) * n_q + qi
    for src_ref, dst_ref, (first, count) in zip(cast_in, cast_out, cast_plan):
        @pl.when((step >= first) & (step < first + count))
        def _():
            dst_ref[...] = src_ref[...].astype(BF16)

    def max_sq_norm(x_bf16):
        xf = x_bf16.astype(F32)
        return jnp.max(jnp.sum(xf * xf, axis=-1, keepdims=True), axis=0, keepdims=True)

    def prepare_queries(src_ref, tile, dst):
        rows = pl.ds(pl.multiple_of(tile * tq, tq), tq)
        cos, sin = cos_ref[rows, :], sin_ref[rows, :]
        q2max = None
        for h in range(A_GROUP):
            cols = slice(h * HEAD_DIM, (h + 1) * HEAD_DIM)
            q = _rms(src_ref[0, :, cols].astype(F32), gq_ref[...])
            q = (_rotary(q, cos, sin, AXIAL_SHIFT) * Q_SCALE).astype(BF16)
            qs_ref[dst, :, cols] = q
            q2max = max_sq_norm(q) if q2max is None else jnp.maximum(q2max, max_sq_norm(q))
        q2max_ref[dst] = jnp.broadcast_to(q2max, q2max_ref.shape[1:])

    @pl.when(qi == 0)
    def _():
        k = _rms(k_ref[0].astype(F32), gk_ref[...])
        k = _rotary(k, cos_ref[...], sin_ref[...], AXIAL_SHIFT).astype(BF16)
        kn_ref[...] = k
        k2max_ref[...] = jnp.broadcast_to(max_sq_norm(k), k2max_ref.shape)
        v1_ref[:, 0:HEAD_DIM] = v_ref[0]
        v1_ref[:, HEAD_DIM:] = jnp.ones((v1_ref.shape[0], HEAD_DIM), BF16)
        prepare_queries(q_ref, 0, 0)

    bound = jnp.sqrt(q2max_ref[slot, 0:1, 0:1] * k2max_ref[0:1, 0:1]) * SCORE_BOUND_SLACK
    bound_is_safe = bound[0, 0] <= MAX_SAFE_SCORE_BOUND

    def attend(shift_of):
        prepare_queries(qnext_ref, jnp.minimum(qi + 1, n_q - 1), 1 - slot)
        kn = kn_ref[...]
        v1 = v1_ref[...]
        for h in range(A_GROUP):
            cols = slice(h * HEAD_DIM, (h + 1) * HEAD_DIM)
            s = lax.dot_general(qs_ref[slot, :, cols], kn, (((1,), (1,)), ((), ())),
                                preferred_element_type=F32)
            p = jnp.exp2(s - shift_of(s))
            pv = jnp.dot(p.astype(BF16), v1, preferred_element_type=F32)
            o, l = pv[:, :HEAD_DIM], pv[:, HEAD_DIM:]
            g = ga_ref[0, :, cols].astype(F32)
            o_ref[0, :, cols] = (o * (1.0 / l) * _silu(g)).astype(BF16)

    @pl.when(bound_is_safe)
    def _():
        attend(lambda s: bound)

    @pl.when(jnp.logical_not(bound_is_safe))
    def _():
        attend(lambda s: jnp.max(s, axis=-1, keepdims=True))


def _attention_a(proj3, q_gain, k_gain, tables, later_weights):
    B, S, _ = proj3.shape
    tq = ATTN_Q_TILE
    n_q = S // tq
    grid = (B, A_KV_HEADS, n_q)
    gw = A_GROUP * HEAD_DIM
    ka_blk = _OFF_KA // HEAD_DIM
    va_blk = _OFF_VA // HEAD_DIM
    ga_blk = _OFF_GA // gw
    full = pl.BlockSpec((S, HEAD_DIM), lambda b, g, i: (0, 0))
    vec = pl.BlockSpec((1, HEAD_DIM), lambda b, g, i: (0, 0))

    cast_plan, cast_specs, first = [], [], 0
    for lw in later_weights:
        count = lw.shape[0] // CAST_CHUNK_ROWS
        assert count * CAST_CHUNK_ROWS == lw.shape[0]

        def chunk_index(b, g, i, first=first, count=count):
            step = (b * grid[1] + g) * grid[2] + i
            return jnp.clip(step - first, 0, count - 1), 0

        cast_plan.append((first, count))
        cast_specs.append(pl.BlockSpec((CAST_CHUNK_ROWS, lw.shape[1]), chunk_index))
        first += count
    assert first <= grid[0] * grid[1] * grid[2]

    outs = pl.pallas_call(
        functools.partial(_attn_a_kernel, tq=tq, n_q=n_q, cast_plan=tuple(cast_plan)),
        grid=grid,
        in_specs=[
            pl.BlockSpec((1, tq, gw), lambda b, g, i: (b, i, g)),
            pl.BlockSpec((1, tq, gw), lambda b, g, i: (b, jnp.minimum(i + 1, n_q - 1), g)),
            pl.BlockSpec((1, S, HEAD_DIM), lambda b, g, i: (b, 0, ka_blk + g)),
            pl.BlockSpec((1, S, HEAD_DIM), lambda b, g, i: (b, 0, va_blk + g)),
            pl.BlockSpec((1, tq, gw), lambda b, g, i: (b, i, ga_blk + g)),
            vec, vec, full, full,
        ] + cast_specs,
        out_specs=[pl.BlockSpec((1, tq, gw), lambda b, g, i: (b, i, g))] + cast_specs,
        out_shape=[jax.ShapeDtypeStruct((B, S, A_WIDTH), BF16)]
        + [jax.ShapeDtypeStruct(lw.shape, BF16) for lw in later_weights],
        scratch_shapes=[
            pltpu.VMEM((S, HEAD_DIM), BF16),
            pltpu.VMEM((S, 2 * HEAD_DIM), BF16),
            pltpu.VMEM((8, HEAD_DIM), F32),
            pltpu.VMEM((2, tq, gw), BF16),
            pltpu.VMEM((2, 8, HEAD_DIM), F32),
        ],
        compiler_params=pltpu.CompilerParams(
            dimension_semantics=("arbitrary", "arbitrary", "arbitrary"),
            vmem_limit_bytes=V7X_VMEM_LIMIT_BYTES),
        name="attention_a",
    )(proj3, proj3, proj3, proj3, proj3, q_gain, k_gain, *tables, *later_weights)
    return outs[0], outs[1:]


def _band_tile(q, kw, vw, valid):
    s = lax.dot_general(q, kw, (((1,), (1,)), ((), ())), preferred_element_type=F32)
    s = jnp.where(valid, s, MASK_VALUE)
    m = jnp.max(s, axis=-1, keepdims=True)
    p = jnp.exp2(s - m)
    pv = jnp.dot(p.astype(BF16), vw, preferred_element_type=F32)
    o, l = pv[:, :HEAD_DIM], pv[:, HEAD_DIM:]
    return o * (1.0 / l), m * LN_2 + jnp.log(l)


def _window_mask(t, n_tiles):
    qi = lax.broadcasted_iota(jnp.int32, (BAND_TQ, BAND_WIN), 0)
    kj = lax.broadcasted_iota(jnp.int32, (BAND_TQ, BAND_WIN), 1)
    kpos = kj + (t * BAND_TQ - BAND_HALF)
    return (kj >= qi) & (kj <= qi + 2 * BAND_HALF) & (kpos >= 0) & (kpos < n_tiles * BAND_TQ)


def _zero_pads(ref, lead, n_rows):
    zeros = jnp.zeros((BAND_HALF, HEAD_DIM), BF16)
    lanes = slice(0, HEAD_DIM)
    for idx in lead:
        ref[idx + (slice(0, BAND_HALF), lanes)] = zeros
        ref[idx + (slice(BAND_HALF + n_rows, 2 * BAND_HALF + n_rows), lanes)] = zeros


def _band_mix_kernel(q1_ref, k1_ref, v1_ref, q4_ref, k4_ref, v4_ref, q16_ref, k16_ref, v16_ref,
                     gb_ref, cos_ref, sin_ref, perm_ref, y_ref,
                     nat_ref, tmp_ref, qd1_ref, kp1_ref, vp1_ref, qd4_ref, kp4_ref, vp4_ref,
                     qd16_ref, kd16_ref, vd16_ref, od16_ref, og_ref, lg_ref, *, S):
    L4 = S // SPLIT
    L16 = L4 // SPLIT
    n1, n4 = S // BAND_TQ, L4 // BAND_TQ
    per_tile = BAND_TQ // L16

    def rotated(src_ref, scale=None):
        x = src_ref[0]
        partner = jnp.dot(x, perm_ref[...], preferred_element_type=F32)
        y = x.astype(F32) * cos_ref[...] + partner * sin_ref[...]
        return y if scale is None else y * scale

    lanes = slice(0, HEAD_DIM)
    vp1_ref[:, HEAD_DIM:] = jnp.ones((vp1_ref.shape[0], HEAD_DIM), BF16)
    vp4_ref[:, :, HEAD_DIM:] = jnp.ones(vp4_ref.shape[:2] + (HEAD_DIM,), BF16)
    vd16_ref[:, :, HEAD_DIM:] = jnp.ones(vd16_ref.shape[:2] + (HEAD_DIM,), BF16)

    _zero_pads(kp1_ref, [()], S)
    _zero_pads(vp1_ref, [()], S)
    qd1_ref[...] = rotated(q1_ref, Q_SCALE).astype(BF16)
    kp1_ref[BAND_HALF:BAND_HALF + S, :] = rotated(k1_ref).astype(BF16)
    vp1_ref[BAND_HALF:BAND_HALF + S, lanes] = v1_ref[0]

    for t in range(n1):
        rows = slice(t * BAND_TQ, (t + 1) * BAND_TQ)
        win = slice(t * BAND_TQ, t * BAND_TQ + BAND_WIN)
        o, lse = _band_tile(qd1_ref[rows, :], kp1_ref[win, :], vp1_ref[win, :],
                            _window_mask(t, n1))
        og_ref[0, rows, :] = o
        lg_ref[0, rows, :] = lse

    _zero_pads(kp4_ref, [(r,) for r in range(SPLIT)], L4)
    _zero_pads(vp4_ref, [(r,) for r in range(SPLIT)], L4)
    for n, (val, dst_ref, lead) in enumerate(((rotated(q4_ref, Q_SCALE), qd4_ref, 0),
                                              (rotated(k4_ref), kp4_ref, BAND_HALF),
                                              (v4_ref[0].astype(F32), vp4_ref, BAND_HALF))):
        nat_ref[n] = val
        for r in range(SPLIT):
            dst_ref[r, lead:lead + L4, lanes] = \
                nat_ref[n, pl.ds(r, L4, stride=SPLIT), :].astype(BF16)

    for r in range(SPLIT):
        for t in range(n4):
            rows = pl.ds(r + t * (BAND_TQ * SPLIT), BAND_TQ, stride=SPLIT)
            win = slice(t * BAND_TQ, t * BAND_TQ + BAND_WIN)
            o, lse = _band_tile(qd4_ref[r, t * BAND_TQ:(t + 1) * BAND_TQ, :],
                                kp4_ref[r, win, :], vp4_ref[r, win, :], _window_mask(t, n4))
            og_ref[1, rows, :] = o
            lg_ref[1, rows, :] = lse

    for n, (val, dst_ref) in enumerate(((rotated(q16_ref, Q_SCALE), qd16_ref),
                                        (rotated(k16_ref), kd16_ref),
                                        (v16_ref[0].astype(F32), vd16_ref))):
        nat_ref[3 + n] = val
        for a in range(SPLIT):
            tmp_ref[n, a] = nat_ref[3 + n, pl.ds(a, L4, stride=SPLIT), :]
            for b in range(SPLIT):
                r = a + SPLIT * b
                dst_ref[r // per_tile, (r % per_tile) * L16:(r % per_tile + 1) * L16, lanes] = \
                    tmp_ref[n, a, pl.ds(b, L16, stride=SPLIT), :].astype(BF16)

    qi = lax.broadcasted_iota(jnp.int32, (BAND_TQ, BAND_TQ), 0)
    kj = lax.broadcasted_iota(jnp.int32, (BAND_TQ, BAND_TQ), 1)
    same_residue = functools.reduce(
        jnp.logical_or, [(qi >= c * L16) & (qi < (c + 1) * L16) & (kj >= c * L16)
                         & (kj < (c + 1) * L16) for c in range(per_tile)])
    stacked_band = same_residue & (kj >= qi - BAND_HALF) & (kj <= qi + BAND_HALF)
    for j in range(SPLIT * SPLIT // per_tile):
        o, lse = _band_tile(qd16_ref[j], kd16_ref[j], vd16_ref[j], stacked_band)
        od16_ref[0, j] = o
        od16_ref[1, j] = lse

    for n, dst_ref in enumerate((og_ref, lg_ref)):
        for a in range(SPLIT):
            for b in range(SPLIT):
                r = a + SPLIT * b
                tmp_ref[n, a, pl.ds(b, L16, stride=SPLIT), :] = \
                    od16_ref[n, r // per_tile, (r % per_tile) * L16:(r % per_tile + 1) * L16, :]
            dst_ref[2, pl.ds(a, L4, stride=SPLIT), :] = tmp_ref[n, a]

    chunk = BAND_TQ

    def mix(i, carry):
        rows = pl.ds(pl.multiple_of(i * chunk, chunk), chunk)
        lses = [lg_ref[g, rows, :] for g in range(B_GROUPS)]
        lmax = functools.reduce(jnp.maximum, lses)
        es = [jnp.exp(l - lmax) for l in lses]
        num = sum(e * og_ref[g, rows, :] for g, e in enumerate(es))
        ob = num * (1.0 / sum(es))
        y_ref[0, rows, :] = (ob * _silu(gb_ref[0, rows, :].astype(F32))).astype(BF16)
        return carry

    lax.fori_loop(0, S // chunk, mix, 0)


def _band_mix(proj3, tables):
    B, S, NC = proj3.shape
    assert tuple(d for _, d in B_PATTERNS) == (1, SPLIT, SPLIT * SPLIT)
    L4, L16 = S // SPLIT, S // (SPLIT * SPLIT)
    assert L4 % BAND_TQ == 0 and BAND_TQ % L16 == 0
    n16 = S // BAND_TQ

    def head_blk(off, g):
        first = off // HEAD_DIM + g * B_HEADS
        return pl.BlockSpec((1, S, HEAD_DIM), lambda b, h: (b, 0, first + h))

    qkv_specs = [head_blk(off, g) for g in range(B_GROUPS)
                 for off in (_OFF_QB, _OFF_KB, _OFF_VB)]
    gb_first = _OFF_GB // HEAD_DIM
    table = pl.BlockSpec((S, HEAD_DIM), lambda b, h: (0, 0))
    return pl.pallas_call(
        functools.partial(_band_mix_kernel, S=S),
        grid=(B, B_HEADS),
        in_specs=qkv_specs
        + [pl.BlockSpec((1, S, HEAD_DIM), lambda b, h: (b, 0, gb_first + h)), table, table,
           pl.BlockSpec((HEAD_DIM, HEAD_DIM), lambda b, h: (0, 0))],
        out_specs=pl.BlockSpec((1, S, HEAD_DIM), lambda b, h: (b, 0, h)),
        out_shape=jax.ShapeDtypeStruct((B, S, B_WIDTH), BF16),
        scratch_shapes=[
            pltpu.VMEM((6, S, HEAD_DIM), F32),
            pltpu.VMEM((3, SPLIT, L4, HEAD_DIM), F32),
            pltpu.VMEM((S, HEAD_DIM), BF16),
            pltpu.VMEM((S + 2 * BAND_HALF, HEAD_DIM), BF16),
            pltpu.VMEM((S + 2 * BAND_HALF, 2 * HEAD_DIM), BF16),
            pltpu.VMEM((SPLIT, L4, HEAD_DIM), BF16),
            pltpu.VMEM((SPLIT, L4 + 2 * BAND_HALF, HEAD_DIM), BF16),
            pltpu.VMEM((SPLIT, L4 + 2 * BAND_HALF, 2 * HEAD_DIM), BF16),
            pltpu.VMEM((n16, BAND_TQ, HEAD_DIM), BF16),
            pltpu.VMEM((n16, BAND_TQ, HEAD_DIM), BF16),
            pltpu.VMEM((n16, BAND_TQ, 2 * HEAD_DIM), BF16),
            pltpu.VMEM((2, n16, BAND_TQ, HEAD_DIM), F32),
            pltpu.VMEM((B_GROUPS, S, HEAD_DIM), F32),
            pltpu.VMEM((B_GROUPS, S, HEAD_DIM), F32),
        ],
        compiler_params=pltpu.CompilerParams(
            dimension_semantics=("parallel", "parallel"),
            vmem_limit_bytes=V7X_VMEM_LIMIT_BYTES),
        name="band_mix",
    )(*([proj3] * (3 * B_GROUPS + 1)), *tables, _partner_permutation(PARTIAL_ROPE_DIM))


def _merge_kernel(x_ref, ya_ref, yb_ref, za0, za1, za2, za3, zb0, zb1, zb2, zb3, bias_ref,
                  wa_ref, wb_ref, wo_ref, fg_ref, out_ref, merged_ref, *, final_norm):
    pa = jnp.dot(ya_ref[...], wa_ref[...], preferred_element_type=F32)
    pb = jnp.dot(yb_ref[...], wb_ref[...], preferred_element_type=F32)
    for c, (za, zb) in enumerate(((za0, zb0), (za1, zb1), (za2, zb2), (za3, zb3))):
        cols = slice(c * COL_BLK, (c + 1) * COL_BLK)
        gate_a = jax.nn.sigmoid(za[...].astype(F32) + bias_ref[0:1, cols])
        gate_b = jax.nn.sigmoid(zb[...].astype(F32) + bias_ref[1:2, cols])
        merged_ref[:, cols] = (gate_a * pa[:, cols] + gate_b * pb[:, cols]).astype(BF16)

    y = x_ref[...] + jnp.dot(merged_ref[...], wo_ref[...], preferred_element_type=F32)
    out_ref[...] = _rms(y, fg_ref[...]) if final_norm else y


def _merge(x2, ya, yb, proj2, bias, wa, wb, wo, final_gain, final_norm):
    T, D = x2.shape
    tm = MERGE_ROW_TILE
    za_blk = _OFF_ZA // COL_BLK
    n_z = D // COL_BLK

    def row_blk(width, col=0):
        return pl.BlockSpec((tm, width), lambda i: (i, col))

    def whole(a):
        return pl.BlockSpec(a.shape, lambda i: (0, 0))

    z_specs = [row_blk(COL_BLK, za_blk + c) for c in range(2 * n_z)]
    return pl.pallas_call(
        functools.partial(_merge_kernel, final_norm=final_norm),
        grid=(T // tm,),
        in_specs=[row_blk(D), row_blk(A_WIDTH), row_blk(B_WIDTH)] + z_specs
        + [whole(bias), whole(wa), whole(wb), whole(wo), whole(final_gain)],
        out_specs=row_blk(D),
        out_shape=jax.ShapeDtypeStruct((T, D), F32),
        scratch_shapes=[pltpu.VMEM((tm, D), BF16)],
        compiler_params=pltpu.CompilerParams(
            dimension_semantics=("parallel",),
            vmem_limit_bytes=V7X_VMEM_LIMIT_BYTES),
        name="merge_output",
    )(x2, ya, yb, *([proj2] * (2 * n_z)), bias, wa, wb, wo, final_gain)


def _angles(pos, dim, theta):
    expo = np.arange(0, dim, 2, dtype=np.float64) / dim
    return pos.astype(np.float64)[:, None] / np.power(float(theta), expo)[None, :]


def _lane_tables(ang_blocks):
    cos, sin = [], []
    for blk in ang_blocks:
        if isinstance(blk, int):
            S = cos[0].shape[0]
            cos.append(np.ones((S, blk)))
            sin.append(np.zeros((S, blk)))
            continue
        c, s = np.cos(blk), np.sin(blk)
        cos += [c, c]
        sin += [-s, s]
    return tuple(jnp.asarray(np.concatenate(t, axis=-1), dtype=F32) for t in (cos, sin))


def _partner_permutation(rotary_dim):
    half = rotary_dim // 2
    p = np.zeros((HEAD_DIM, HEAD_DIM))
    for i in range(rotary_dim):
        p[i + half if i < half else i - half, i] = 1.0
    return jnp.asarray(p, dtype=BF16)


def _axial_tables(S):
    pos = np.arange(S)
    half = HEAD_DIM // 2
    return _lane_tables([_angles(pos // GRID_W, half, AXIAL_THETA),
                         _angles(pos % GRID_W, half, AXIAL_THETA)])


def _partial_tables(S):
    pos = np.arange(S)
    return _lane_tables([_angles(pos, PARTIAL_ROPE_DIM, ROPE_THETA),
                         HEAD_DIM - PARTIAL_ROPE_DIM])


def kernel(x, norm_gain, w_in, q_norm_gain, k_norm_gain, merge_gate_bias, w_branch_a,
           w_branch_b, w_out, final_norm_gain):
    B, S, D = x.shape
    depth = norm_gain.shape[0]
    n_cols = _OFF_ZA + 2 * D
    assert w_in.shape[2] == n_cols and S % GRID_W == 0
    assert all(w // (2 * d) == BAND_HALF for w, d in B_PATTERNS)
    axial = _axial_tables(S)
    partial = _partial_tables(S)
    x2 = x.reshape(B * S, D)
    for l in range(depth):
        proj2 = _input_projection(x2, norm_gain[l][None, :], w_in[l])
        proj3 = proj2.reshape(B, S, n_cols)
        ya, (wa, wb, wo) = _attention_a(
            proj3, q_norm_gain[l][None, :], k_norm_gain[l][None, :], axial,
            (w_branch_a[l], w_branch_b[l], w_out[l]))
        yb = _band_mix(proj3, partial)
        x2 = _merge(x2, ya.reshape(B * S, A_WIDTH), yb.reshape(B * S, B_WIDTH), proj2,
                    merge_gate_bias[l], wa, wb, wo,
                    final_norm_gain[None, :], final_norm=(l == depth - 1))
    return x2.reshape(B, S, D)
```

```python
import functools
import math

import numpy as np
import jax
import jax.numpy as jnp
from jax import lax
from jax.experimental import pallas as pl
from jax.experimental.pallas import tpu as pltpu

F32 = jnp.float32
BF16 = jnp.bfloat16

HEAD_DIM = 128
GRID_W = 64
NORM_EPS = 1e-6
A_Q_HEADS = 8
A_KV_HEADS = 2
A_GROUP = A_Q_HEADS // A_KV_HEADS
A_WIDTH = A_Q_HEADS * HEAD_DIM
A_KV_WIDTH = A_KV_HEADS * HEAD_DIM
AXIAL_THETA = 10000.0
AXIAL_SHIFT = HEAD_DIM // 4
MAX_SAFE_SCORE_BOUND = 50.0
SCORE_BOUND_SLACK = 1.01
B_PATTERNS = ((128, 1), (512, 4), (2048, 16))
B_GROUPS = len(B_PATTERNS)
B_HEADS = 4
B_WIDTH = B_HEADS * HEAD_DIM
PARTIAL_ROPE_DIM = HEAD_DIM // 4
PARTIAL_SHIFT = PARTIAL_ROPE_DIM // 2
ROPE_THETA = 500000.0
BAND_HALF = 64
BAND_TQ = 256
BAND_WIN = BAND_TQ + 2 * BAND_HALF
SPLIT = 4
MASK_VALUE = -1e30
LOG2_E = math.log2(math.e)
LN_2 = math.log(2.0)
Q_SCALE = HEAD_DIM ** -0.5 * LOG2_E

COL_BLK = 512
_OFF_QA = 0
_OFF_KA = A_WIDTH
_OFF_VA = _OFF_KA + A_KV_WIDTH
_OFF_GA = _OFF_VA + A_KV_WIDTH
_OFF_QB = _OFF_GA + A_WIDTH
_OFF_KB = _OFF_QB + B_GROUPS * B_WIDTH
_OFF_VB = _OFF_KB + B_GROUPS * B_WIDTH
_OFF_GB = _OFF_VB + B_GROUPS * B_WIDTH
_OFF_ZA = _OFF_GB + B_WIDTH

V7X_VMEM_LIMIT_BYTES = 56 * 1024 * 1024
PROJ_ROW_TILE = 1024
PROJ_COL_TILE = 2 * COL_BLK
PROJ_ROW_TILES_PER_WEIGHT_TILE = 2
ATTN_Q_TILE = 512
MERGE_ROW_TILE = 512


def _rotary(x, cos, sin, shift):
    lane = lax.broadcasted_iota(jnp.int32, x.shape, 1)
    up = pltpu.roll(x, HEAD_DIM - shift, 1)
    dn = pltpu.roll(x, shift, 1)
    partner = jnp.where(lane % (2 * shift) < shift, up, dn)
    return x * cos + partner * sin


def _rms(x, gain):
    ms = jnp.mean(x * x, axis=-1, keepdims=True)
    return x * lax.rsqrt(ms + NORM_EPS) * gain


def _silu(g):
    return g * jax.nn.sigmoid(g)


def _proj_kernel(x_ref, g_ref, w_ref, o_ref, h_ref, wb_ref, *, n_full, tail):
    j, r = pl.program_id(1), pl.program_id(2)

    def project(width, first_tile):
        cols = slice(0, width)

        def normed_rows():
            if not first_tile:
                return h_ref[r]
            h = _rms(x_ref[...], g_ref[...]).astype(BF16)
            h_ref[r] = h
            return h

        @pl.when(r == 0)
        def _():
            w = w_ref[:, cols].astype(BF16)
            wb_ref[:, cols] = w
            o_ref[:, cols] = jnp.dot(normed_rows(), w, preferred_element_type=F32).astype(BF16)

        @pl.when(r != 0)
        def _():
            o_ref[:, cols] = jnp.dot(normed_rows(), wb_ref[:, cols],
                                     preferred_element_type=F32).astype(BF16)

    assert n_full >= 1

    @pl.when(j == 0)
    def _():
        project(w_ref.shape[1], True)

    @pl.when((j > 0) & (j < n_full))
    def _():
        project(w_ref.shape[1], False)

    if tail:
        @pl.when(j == n_full)
        def _():
            project(tail, False)


def _input_projection(x2, gain, w):
    T, D = x2.shape
    N = w.shape[1]
    tm, tn, R = PROJ_ROW_TILE, PROJ_COL_TILE, PROJ_ROW_TILES_PER_WEIGHT_TILE
    n_full, tail = divmod(N, tn)

    def x_index(g, j, r):
        return g * R + jnp.where(j == 0, r, R - 1), 0

    return pl.pallas_call(
        functools.partial(_proj_kernel, n_full=n_full, tail=tail),
        grid=(T // (tm * R), pl.cdiv(N, tn), R),
        in_specs=[
            pl.BlockSpec((tm, D), x_index),
            pl.BlockSpec((1, D), lambda g, j, r: (0, 0)),
            pl.BlockSpec((D, tn), lambda g, j, r: (0, j)),
        ],
        out_specs=pl.BlockSpec((tm, tn), lambda g, j, r: (g * R + r, j)),
        out_shape=jax.ShapeDtypeStruct((T, N), BF16),
        scratch_shapes=[pltpu.VMEM((R, tm, D), BF16), pltpu.VMEM((D, tn), BF16)],
        compiler_params=pltpu.CompilerParams(
            dimension_semantics=("parallel", "arbitrary", "arbitrary"),
            vmem_limit_bytes=V7X_VMEM_LIMIT_BYTES),
        name="input_projection",
    )(x2, gain, w)


def _attn_a_kernel(q_ref, qnext_ref, k_ref, v_ref, ga_ref, gq_ref, gk_ref, cos_ref, sin_ref,
                   o_ref, kn_ref, v1_ref, k2max_ref, qs_ref, q2max_ref, *, tq, n_q):
    qi = pl.program_id(2)
    slot = qi % 2

    def max_sq_norm(x_bf16):
        xf = x_bf16.astype(F32)
        return jnp.max(jnp.sum(xf * xf, axis=-1, keepdims=True), axis=0, keepdims=True)

    def prepare_queries(src_ref, tile, dst):
        rows = pl.ds(pl.multiple_of(tile * tq, tq), tq)
        cos, sin = cos_ref[rows, :], sin_ref[rows, :]
        q2max = None
        for h in range(A_GROUP):
            cols = slice(h * HEAD_DIM, (h + 1) * HEAD_DIM)
            q = _rms(src_ref[0, :, cols].astype(F32), gq_ref[...])
            q = (_rotary(q, cos, sin, AXIAL_SHIFT) * Q_SCALE).astype(BF16)
            qs_ref[dst, :, cols] = q
            q2max = max_sq_norm(q) if q2max is None else jnp.maximum(q2max, max_sq_norm(q))
        q2max_ref[dst] = jnp.broadcast_to(q2max, q2max_ref.shape[1:])

    @pl.when(qi == 0)
    def _():
        k = _rms(k_ref[0].astype(F32), gk_ref[...])
        k = _rotary(k, cos_ref[...], sin_ref[...], AXIAL_SHIFT).astype(BF16)
        kn_ref[...] = k
        k2max_ref[...] = jnp.broadcast_to(max_sq_norm(k), k2max_ref.shape)
        v1_ref[:, 0:HEAD_DIM] = v_ref[0]
        v1_ref[:, HEAD_DIM:] = jnp.ones((v1_ref.shape[0], HEAD_DIM), BF16)

    @pl.when((pl.program_id(0) == 0) & (pl.program_id(1) == 0) & (qi == 0))
    def _():
        prepare_queries(q_ref, 0, 0)

    bound = jnp.sqrt(q2max_ref[slot, 0:1, 0:1] * k2max_ref[0:1, 0:1]) * SCORE_BOUND_SLACK
    bound_is_safe = bound[0, 0] <= MAX_SAFE_SCORE_BOUND

    def attend(shift_of):
        prepare_queries(qnext_ref, (qi + 1) % n_q, 1 - slot)
        kn = kn_ref[...]
        v1 = v1_ref[...]
        for h in range(A_GROUP):
            cols = slice(h * HEAD_DIM, (h + 1) * HEAD_DIM)
            s = lax.dot_general(qs_ref[slot, :, cols], kn, (((1,), (1,)), ((), ())),
                                preferred_element_type=F32)
            p = jnp.exp2(s - shift_of(s))
            pv = jnp.dot(p.astype(BF16), v1, preferred_element_type=F32)
            o, l = pv[:, :HEAD_DIM], pv[:, HEAD_DIM:]
            g = ga_ref[0, :, cols].astype(F32)
            o_ref[0, :, cols] = (o * (1.0 / l) * _silu(g)).astype(BF16)

    @pl.when(bound_is_safe)
    def _():
        attend(lambda s: bound)

    @pl.when(jnp.logical_not(bound_is_safe))
    def _():
        attend(lambda s: jnp.max(s, axis=-1, keepdims=True))


def _attention_a(proj3, q_gain, k_gain, tables):
    B, S, _ = proj3.shape
    tq = ATTN_Q_TILE
    n_q = S // tq
    gw = A_GROUP * HEAD_DIM
    ka_blk = _OFF_KA // HEAD_DIM
    va_blk = _OFF_VA // HEAD_DIM
    ga_blk = _OFF_GA // gw
    full = pl.BlockSpec((S, HEAD_DIM), lambda b, g, i: (0, 0))
    vec = pl.BlockSpec((1, HEAD_DIM), lambda b, g, i: (0, 0))
    assert n_q % 2 == 0

    def next_step_q(b, g, i):
        n = jnp.minimum((b * A_KV_HEADS + g) * n_q + i + 1, B * A_KV_HEADS * n_q - 1)
        return n // (A_KV_HEADS * n_q), n % n_q, (n // n_q) % A_KV_HEADS

    return pl.pallas_call(
        functools.partial(_attn_a_kernel, tq=tq, n_q=n_q),
        grid=(B, A_KV_HEADS, n_q),
        in_specs=[
            pl.BlockSpec((1, tq, gw), lambda b, g, i: (b, i, g)),
            pl.BlockSpec((1, tq, gw), next_step_q),
            pl.BlockSpec((1, S, HEAD_DIM), lambda b, g, i: (b, 0, ka_blk + g)),
            pl.BlockSpec((1, S, HEAD_DIM), lambda b, g, i: (b, 0, va_blk + g)),
            pl.BlockSpec((1, tq, gw), lambda b, g, i: (b, i, ga_blk + g)),
            vec, vec, full, full,
        ],
        out_specs=pl.BlockSpec((1, tq, gw), lambda b, g, i: (b, i, g)),
        out_shape=jax.ShapeDtypeStruct((B, S, A_WIDTH), BF16),
        scratch_shapes=[
            pltpu.VMEM((S, HEAD_DIM), BF16),
            pltpu.VMEM((S, 2 * HEAD_DIM), BF16),
            pltpu.VMEM((8, HEAD_DIM), F32),
            pltpu.VMEM((2, tq, gw), BF16),
            pltpu.VMEM((2, 8, HEAD_DIM), F32),
        ],
        compiler_params=pltpu.CompilerParams(
            dimension_semantics=("arbitrary", "arbitrary", "arbitrary"),
            vmem_limit_bytes=V7X_VMEM_LIMIT_BYTES),
        name="attention_a",
    )(proj3, proj3, proj3, proj3, proj3, q_gain, k_gain, *tables)


def _band_tile(q, kw, vw, valid):
    s = lax.dot_general(q, kw, (((1,), (1,)), ((), ())), preferred_element_type=F32)
    s = jnp.where(valid, s, MASK_VALUE)
    m = jnp.max(s, axis=-1, keepdims=True)
    p = jnp.exp2(s - m)
    pv = jnp.dot(p.astype(BF16), vw, preferred_element_type=F32)
    o, l = pv[:, :HEAD_DIM], pv[:, HEAD_DIM:]
    return o * (1.0 / l), m * LN_2 + jnp.log(l)


def _window_mask(t, n_tiles):
    qi = lax.broadcasted_iota(jnp.int32, (BAND_TQ, BAND_WIN), 0)
    kj = lax.broadcasted_iota(jnp.int32, (BAND_TQ, BAND_WIN), 1)
    kpos = kj + (t * BAND_TQ - BAND_HALF)
    return (kj >= qi) & (kj <= qi + 2 * BAND_HALF) & (kpos >= 0) & (kpos < n_tiles * BAND_TQ)


def _zero_pads(ref, lead, n_rows):
    zeros = jnp.zeros((BAND_HALF, HEAD_DIM), BF16)
    lanes = slice(0, HEAD_DIM)
    for idx in lead:
        ref[idx + (slice(0, BAND_HALF), lanes)] = zeros
        ref[idx + (slice(BAND_HALF + n_rows, 2 * BAND_HALF + n_rows), lanes)] = zeros


def _band_mix_kernel(q1_ref, k1_ref, v1_ref, q4_ref, k4_ref, v4_ref, q16_ref, k16_ref, v16_ref,
                     gb_ref, cos_ref, sin_ref, perm_ref, y_ref,
                     nat_ref, tmp_ref, qd1_ref, kp1_ref, vp1_ref, qd4_ref, kp4_ref, vp4_ref,
                     qd16_ref, kd16_ref, vd16_ref, od16_ref, og_ref, lg_ref, *, S):
    L4 = S // SPLIT
    L16 = L4 // SPLIT
    n1, n4 = S // BAND_TQ, L4 // BAND_TQ
    per_tile = BAND_TQ // L16

    def rotated(src_ref, scale=None):
        x = src_ref[0]
        partner = jnp.dot(x, perm_ref[...], preferred_element_type=F32)
        y = x.astype(F32) * cos_ref[...] + partner * sin_ref[...]
        return y if scale is None else y * scale

    lanes = slice(0, HEAD_DIM)
    vp1_ref[:, HEAD_DIM:] = jnp.ones((vp1_ref.shape[0], HEAD_DIM), BF16)
    vp4_ref[:, :, HEAD_DIM:] = jnp.ones(vp4_ref.shape[:2] + (HEAD_DIM,), BF16)
    vd16_ref[:, :, HEAD_DIM:] = jnp.ones(vd16_ref.shape[:2] + (HEAD_DIM,), BF16)

    _zero_pads(kp1_ref, [()], S)
    _zero_pads(vp1_ref, [()], S)
    qd1_ref[...] = rotated(q1_ref, Q_SCALE).astype(BF16)
    kp1_ref[BAND_HALF:BAND_HALF + S, :] = rotated(k1_ref).astype(BF16)
    vp1_ref[BAND_HALF:BAND_HALF + S, lanes] = v1_ref[0]

    for t in range(n1):
        rows = slice(t * BAND_TQ, (t + 1) * BAND_TQ)
        win = slice(t * BAND_TQ, t * BAND_TQ + BAND_WIN)
        o, lse = _band_tile(qd1_ref[rows, :], kp1_ref[win, :], vp1_ref[win, :],
                            _window_mask(t, n1))
        og_ref[0, rows, :] = o
        lg_ref[0, rows, :] = lse

    _zero_pads(kp4_ref, [(r,) for r in range(SPLIT)], L4)
    _zero_pads(vp4_ref, [(r,) for r in range(SPLIT)], L4)
    for n, (val, dst_ref, lead) in enumerate(((rotated(q4_ref, Q_SCALE), qd4_ref, 0),
                                              (rotated(k4_ref), kp4_ref, BAND_HALF),
                                              (v4_ref[0].astype(F32), vp4_ref, BAND_HALF))):
        nat_ref[n] = val
        for r in range(SPLIT):
            dst_ref[r, lead:lead + L4, lanes] = \
                nat_ref[n, pl.ds(r, L4, stride=SPLIT), :].astype(BF16)

    for r in range(SPLIT):
        for t in range(n4):
            rows = pl.ds(r + t * (BAND_TQ * SPLIT), BAND_TQ, stride=SPLIT)
            win = slice(t * BAND_TQ, t * BAND_TQ + BAND_WIN)
            o, lse = _band_tile(qd4_ref[r, t * BAND_TQ:(t + 1) * BAND_TQ, :],
                                kp4_ref[r, win, :], vp4_ref[r, win, :], _window_mask(t, n4))
            og_ref[1, rows, :] = o
            lg_ref[1, rows, :] = lse

    for n, (val, dst_ref) in enumerate(((rotated(q16_ref, Q_SCALE), qd16_ref),
                                        (rotated(k16_ref), kd16_ref),
                                        (v16_ref[0].astype(F32), vd16_ref))):
        nat_ref[3 + n] = val
        for a in range(SPLIT):
            tmp_ref[n, a] = nat_ref[3 + n, pl.ds(a, L4, stride=SPLIT), :]
            for b in range(SPLIT):
                r = a + SPLIT * b
                dst_ref[r // per_tile, (r % per_tile) * L16:(r % per_tile + 1) * L16, lanes] = \
                    tmp_ref[n, a, pl.ds(b, L16, stride=SPLIT), :].astype(BF16)

    qi = lax.broadcasted_iota(jnp.int32, (BAND_TQ, BAND_TQ), 0)
    kj = lax.broadcasted_iota(jnp.int32, (BAND_TQ, BAND_TQ), 1)
    same_residue = functools.reduce(
        jnp.logical_or, [(qi >= c * L16) & (qi < (c + 1) * L16) & (kj >= c * L16)
                         & (kj < (c + 1) * L16) for c in range(per_tile)])
    stacked_band = same_residue & (kj >= qi - BAND_HALF) & (kj <= qi + BAND_HALF)
    for j in range(SPLIT * SPLIT // per_tile):
        o, lse = _band_tile(qd16_ref[j], kd16_ref[j], vd16_ref[j], stacked_band)
        od16_ref[0, j] = o
        od16_ref[1, j] = lse

    for n, dst_ref in enumerate((og_ref, lg_ref)):
        for a in range(SPLIT):
            for b in range(SPLIT):
                r = a + SPLIT * b
                tmp_ref[n, a, pl.ds(b, L16, stride=SPLIT), :] = \
                    od16_ref[n, r // per_tile, (r % per_tile) * L16:(r % per_tile + 1) * L16, :]
            dst_ref[2, pl.ds(a, L4, stride=SPLIT), :] = tmp_ref[n, a]

    chunk = BAND_TQ

    def mix(i, carry):
        rows = pl.ds(pl.multiple_of(i * chunk, chunk), chunk)
        lses = [lg_ref[g, rows, :] for g in range(B_GROUPS)]
        lmax = functools.reduce(jnp.maximum, lses)
        es = [jnp.exp(l - lmax) for l in lses]
        num = sum(e * og_ref[g, rows, :] for g, e in enumerate(es))
        ob = num * (1.0 / sum(es))
        y_ref[0, rows, :] = (ob * _silu(gb_ref[0, rows, :].astype(F32))).astype(BF16)
        return carry

    lax.fori_loop(0, S // chunk, mix, 0)


def _band_mix(proj3, tables):
    B, S, NC = proj3.shape
    assert tuple(d for _, d in B_PATTERNS) == (1, SPLIT, SPLIT * SPLIT)
    L4, L16 = S // SPLIT, S // (SPLIT * SPLIT)
    assert L4 % BAND_TQ == 0 and BAND_TQ % L16 == 0
    n16 = S // BAND_TQ

    def head_blk(off, g):
        first = off // HEAD_DIM + g * B_HEADS
        return pl.BlockSpec((1, S, HEAD_DIM), lambda b, h: (b, 0, first + h))

    qkv_specs = [head_blk(off, g) for g in range(B_GROUPS)
                 for off in (_OFF_QB, _OFF_KB, _OFF_VB)]
    gb_first = _OFF_GB // HEAD_DIM
    table = pl.BlockSpec((S, HEAD_DIM), lambda b, h: (0, 0))
    return pl.pallas_call(
        functools.partial(_band_mix_kernel, S=S),
        grid=(B, B_HEADS),
        in_specs=qkv_specs
        + [pl.BlockSpec((1, S, HEAD_DIM), lambda b, h: (b, 0, gb_first + h)), table, table,
           pl.BlockSpec((HEAD_DIM, HEAD_DIM), lambda b, h: (0, 0))],
        out_specs=pl.BlockSpec((1, S, HEAD_DIM), lambda b, h: (b, 0, h)),
        out_shape=jax.ShapeDtypeStruct((B, S, B_WIDTH), BF16),
        scratch_shapes=[
            pltpu.VMEM((6, S, HEAD_DIM), F32),
            pltpu.VMEM((3, SPLIT, L4, HEAD_DIM), F32),
            pltpu.VMEM((S, HEAD_DIM), BF16),
            pltpu.VMEM((S + 2 * BAND_HALF, HEAD_DIM), BF16),
            pltpu.VMEM((S + 2 * BAND_HALF, 2 * HEAD_DIM), BF16),
            pltpu.VMEM((SPLIT, L4, HEAD_DIM), BF16),
            pltpu.VMEM((SPLIT, L4 + 2 * BAND_HALF, HEAD_DIM), BF16),
            pltpu.VMEM((SPLIT, L4 + 2 * BAND_HALF, 2 * HEAD_DIM), BF16),
            pltpu.VMEM((n16, BAND_TQ, HEAD_DIM), BF16),
            pltpu.VMEM((n16, BAND_TQ, HEAD_DIM), BF16),
            pltpu.VMEM((n16, BAND_TQ, 2 * HEAD_DIM), BF16),
            pltpu.VMEM((2, n16, BAND_TQ, HEAD_DIM), F32),
            pltpu.VMEM((B_GROUPS, S, HEAD_DIM), F32),
            pltpu.VMEM((B_GROUPS, S, HEAD_DIM), F32),
        ],
        compiler_params=pltpu.CompilerParams(
            dimension_semantics=("parallel", "parallel"),
            vmem_limit_bytes=V7X_VMEM_LIMIT_BYTES),
        name="band_mix",
    )(*([proj3] * (3 * B_GROUPS + 1)), *tables, _partner_permutation(PARTIAL_ROPE_DIM))


def _merge_kernel(x_ref, ya_ref, yb_ref, za0, za1, za2, za3, zb0, zb1, zb2, zb3, bias_ref,
                  wa_ref, wb_ref, wo_ref, fg_ref, out_ref, merged_ref, *, final_norm):
    pa = jnp.dot(ya_ref[...], wa_ref[...], preferred_element_type=F32)
    pb = jnp.dot(yb_ref[...], wb_ref[...], preferred_element_type=F32)
    for c, (za, zb) in enumerate(((za0, zb0), (za1, zb1), (za2, zb2), (za3, zb3))):
        cols = slice(c * COL_BLK, (c + 1) * COL_BLK)
        gate_a = jax.nn.sigmoid(za[...].astype(F32) + bias_ref[0:1, cols])
        gate_b = jax.nn.sigmoid(zb[...].astype(F32) + bias_ref[1:2, cols])
        merged_ref[:, cols] = (gate_a * pa[:, cols] + gate_b * pb[:, cols]).astype(BF16)

    y = x_ref[...] + jnp.dot(merged_ref[...], wo_ref[...], preferred_element_type=F32)
    out_ref[...] = _rms(y, fg_ref[...]) if final_norm else y


def _merge(x2, ya, yb, proj2, bias, wa, wb, wo, final_gain, final_norm):
    T, D = x2.shape
    tm = MERGE_ROW_TILE
    za_blk = _OFF_ZA // COL_BLK
    n_z = D // COL_BLK

    def row_blk(width, col=0):
        return pl.BlockSpec((tm, width), lambda i: (i, col))

    def whole(a):
        return pl.BlockSpec(a.shape, lambda i: (0, 0))

    z_specs = [row_blk(COL_BLK, za_blk + c) for c in range(2 * n_z)]
    return pl.pallas_call(
        functools.partial(_merge_kernel, final_norm=final_norm),
        grid=(T // tm,),
        in_specs=[row_blk(D), row_blk(A_WIDTH), row_blk(B_WIDTH)] + z_specs
        + [whole(bias), whole(wa), whole(wb), whole(wo), whole(final_gain)],
        out_specs=row_blk(D),
        out_shape=jax.ShapeDtypeStruct((T, D), F32),
        scratch_shapes=[pltpu.VMEM((tm, D), BF16)],
        compiler_params=pltpu.CompilerParams(
            dimension_semantics=("parallel",),
            vmem_limit_bytes=V7X_VMEM_LIMIT_BYTES),
        name="merge_output",
    )(x2, ya, yb, *([proj2] * (2 * n_z)), bias, wa, wb, wo, final_gain)


def _angles(pos, dim, theta):
    expo = np.arange(0, dim, 2, dtype=np.float64) / dim
    return pos.astype(np.float64)[:, None] / np.power(float(theta), expo)[None, :]


def _lane_tables(ang_blocks):
    cos, sin = [], []
    for blk in ang_blocks:
        if isinstance(blk, int):
            S = cos[0].shape[0]
            cos.append(np.ones((S, blk)))
            sin.append(np.zeros((S, blk)))
            continue
        c, s = np.cos(blk), np.sin(blk)
        cos += [c, c]
        sin += [-s, s]
    return tuple(jnp.asarray(np.concatenate(t, axis=-1), dtype=F32) for t in (cos, sin))


def _partner_permutation(rotary_dim):
    half = rotary_dim // 2
    p = np.zeros((HEAD_DIM, HEAD_DIM))
    for i in range(rotary_dim):
        p[i + half if i < half else i - half, i] = 1.0
    return jnp.asarray(p, dtype=BF16)


def _axial_tables(S):
    pos = np.arange(S)
    half = HEAD_DIM // 2
    return _lane_tables([_angles(pos // GRID_W, half, AXIAL_THETA),
                         _angles(pos % GRID_W, half, AXIAL_THETA)])


def _partial_tables(S):
    pos = np.arange(S)
    return _lane_tables([_angles(pos, PARTIAL_ROPE_DIM, ROPE_THETA),
                         HEAD_DIM - PARTIAL_ROPE_DIM])


def kernel(x, norm_gain, w_in, q_norm_gain, k_norm_gain, merge_gate_bias, w_branch_a,
           w_branch_b, w_out, final_norm_gain):
    B, S, D = x.shape
    depth = norm_gain.shape[0]
    n_cols = _OFF_ZA + 2 * D
    assert w_in.shape[2] == n_cols and S % GRID_W == 0
    assert all(w // (2 * d) == BAND_HALF for w, d in B_PATTERNS)
    axial = _axial_tables(S)
    partial = _partial_tables(S)
    x2 = x.reshape(B * S, D)
    for l in range(depth):
        proj2 = _input_projection(x2, norm_gain[l][None, :], w_in[l])
        proj3 = proj2.reshape(B, S, n_cols)
        ya = _attention_a(proj3, q_norm_gain[l][None, :], k_norm_gain[l][None, :], axial)
        yb = _band_mix(proj3, partial)
        x2 = _merge(x2, ya.reshape(B * S, A_WIDTH), yb.reshape(B * S, B_WIDTH), proj2,
                    merge_gate_bias[l], w_branch_a[l].astype(BF16),
                    w_branch_b[l].astype(BF16), w_out[l].astype(BF16),
                    final_norm_gain[None, :], final_norm=(l == depth - 1))
    return x2.reshape(B, S, D)
```

```python
import functools
import math

import numpy as np
import jax
import jax.numpy as jnp
from jax import lax
from jax.experimental import pallas as pl
from jax.experimental.pallas import tpu as pltpu

F32 = jnp.float32
BF16 = jnp.bfloat16

HEAD_DIM = 128
GRID_W = 64
NORM_EPS = 1e-6
A_Q_HEADS = 8
A_KV_HEADS = 2
A_GROUP = A_Q_HEADS // A_KV_HEADS
A_WIDTH = A_Q_HEADS * HEAD_DIM
A_KV_WIDTH = A_KV_HEADS * HEAD_DIM
AXIAL_THETA = 10000.0
AXIAL_SHIFT = HEAD_DIM // 4
MAX_SAFE_SCORE_BOUND = 50.0
SCORE_BOUND_SLACK = 1.01
B_PATTERNS = ((128, 1), (512, 4), (2048, 16))
B_GROUPS = len(B_PATTERNS)
B_HEADS = 4
B_WIDTH = B_HEADS * HEAD_DIM
PARTIAL_ROPE_DIM = HEAD_DIM // 4
PARTIAL_SHIFT = PARTIAL_ROPE_DIM // 2
ROPE_THETA = 500000.0
BAND_HALF = 64
BAND_TQ = 128
BAND_WIN = BAND_TQ + 2 * BAND_HALF
MIX_ROWS = 256
SPLIT = 4
MASK_VALUE = -1e30
LOG2_E = math.log2(math.e)
LN_2 = math.log(2.0)
Q_SCALE = HEAD_DIM ** -0.5 * LOG2_E

COL_BLK = 512
_OFF_QA = 0
_OFF_KA = A_WIDTH
_OFF_VA = _OFF_KA + A_KV_WIDTH
_OFF_GA = _OFF_VA + A_KV_WIDTH
_OFF_QB = _OFF_GA + A_WIDTH
_OFF_KB = _OFF_QB + B_GROUPS * B_WIDTH
_OFF_VB = _OFF_KB + B_GROUPS * B_WIDTH
_OFF_GB = _OFF_VB + B_GROUPS * B_WIDTH
_OFF_ZA = _OFF_GB + B_WIDTH

V7X_VMEM_LIMIT_BYTES = 56 * 1024 * 1024
PROJ_ROW_TILE = 1024
PROJ_COL_TILE = 2 * COL_BLK
PROJ_ROW_TILES_PER_WEIGHT_TILE = 2
ATTN_Q_TILE = 512
MERGE_ROW_TILE = 512


def _rotary(x, cos, sin, shift):
    lane = lax.broadcasted_iota(jnp.int32, x.shape, 1)
    up = pltpu.roll(x, HEAD_DIM - shift, 1)
    dn = pltpu.roll(x, shift, 1)
    partner = jnp.where(lane % (2 * shift) < shift, up, dn)
    return x * cos + partner * sin


def _rms(x, gain):
    ms = jnp.mean(x * x, axis=-1, keepdims=True)
    return x * lax.rsqrt(ms + NORM_EPS) * gain


def _silu(g):
    return g * jax.nn.sigmoid(g)


def _proj_kernel(x_ref, g_ref, w_ref, o_ref, h_ref, wb_ref, *, n_full, tail):
    j, r = pl.program_id(1), pl.program_id(2)

    def project(width, first_tile):
        cols = slice(0, width)

        def normed_rows():
            if not first_tile:
                return h_ref[r]
            h = _rms(x_ref[...], g_ref[...]).astype(BF16)
            h_ref[r] = h
            return h

        @pl.when(r == 0)
        def _():
            w = w_ref[:, cols].astype(BF16)
            wb_ref[:, cols] = w
            o_ref[:, cols] = jnp.dot(normed_rows(), w, preferred_element_type=F32).astype(BF16)

        @pl.when(r != 0)
        def _():
            o_ref[:, cols] = jnp.dot(normed_rows(), wb_ref[:, cols],
                                     preferred_element_type=F32).astype(BF16)

    assert n_full >= 1

    @pl.when(j == 0)
    def _():
        project(w_ref.shape[1], True)

    @pl.when((j > 0) & (j < n_full))
    def _():
        project(w_ref.shape[1], False)

    if tail:
        @pl.when(j == n_full)
        def _():
            project(tail, False)


def _input_projection(x2, gain, w):
    T, D = x2.shape
    N = w.shape[1]
    tm, tn, R = PROJ_ROW_TILE, PROJ_COL_TILE, PROJ_ROW_TILES_PER_WEIGHT_TILE
    n_full, tail = divmod(N, tn)

    def x_index(g, j, r):
        return g * R + jnp.where(j == 0, r, R - 1), 0

    return pl.pallas_call(
        functools.partial(_proj_kernel, n_full=n_full, tail=tail),
        grid=(T // (tm * R), pl.cdiv(N, tn), R),
        in_specs=[
            pl.BlockSpec((tm, D), x_index),
            pl.BlockSpec((1, D), lambda g, j, r: (0, 0)),
            pl.BlockSpec((D, tn), lambda g, j, r: (0, j)),
        ],
        out_specs=pl.BlockSpec((tm, tn), lambda g, j, r: (g * R + r, j)),
        out_shape=jax.ShapeDtypeStruct((T, N), BF16),
        scratch_shapes=[pltpu.VMEM((R, tm, D), BF16), pltpu.VMEM((D, tn), BF16)],
        compiler_params=pltpu.CompilerParams(
            dimension_semantics=("parallel", "arbitrary", "arbitrary"),
            vmem_limit_bytes=V7X_VMEM_LIMIT_BYTES),
        name="input_projection",
    )(x2, gain, w)


def _attn_a_kernel(q_ref, qnext_ref, k_ref, v_ref, ga_ref, gq_ref, gk_ref, cos_ref, sin_ref,
                   o_ref, kn_ref, v1_ref, k2max_ref, qs_ref, q2max_ref, *, tq, n_q):
    qi = pl.program_id(2)
    slot = qi % 2

    def max_sq_norm(x_bf16):
        xf = x_bf16.astype(F32)
        return jnp.max(jnp.sum(xf * xf, axis=-1, keepdims=True), axis=0, keepdims=True)

    def prepare_queries(src_ref, tile, dst):
        rows = pl.ds(pl.multiple_of(tile * tq, tq), tq)
        cos, sin = cos_ref[rows, :], sin_ref[rows, :]
        q2max = None
        for h in range(A_GROUP):
            cols = slice(h * HEAD_DIM, (h + 1) * HEAD_DIM)
            q = _rms(src_ref[0, :, cols].astype(F32), gq_ref[...])
            q = (_rotary(q, cos, sin, AXIAL_SHIFT) * Q_SCALE).astype(BF16)
            qs_ref[dst, :, cols] = q
            q2max = max_sq_norm(q) if q2max is None else jnp.maximum(q2max, max_sq_norm(q))
        q2max_ref[dst] = jnp.broadcast_to(q2max, q2max_ref.shape[1:])

    @pl.when(qi == 0)
    def _():
        k = _rms(k_ref[0].astype(F32), gk_ref[...])
        k = _rotary(k, cos_ref[...], sin_ref[...], AXIAL_SHIFT).astype(BF16)
        kn_ref[...] = k
        k2max_ref[...] = jnp.broadcast_to(max_sq_norm(k), k2max_ref.shape)
        v1_ref[:, 0:HEAD_DIM] = v_ref[0]
        v1_ref[:, HEAD_DIM:] = jnp.ones((v1_ref.shape[0], HEAD_DIM), BF16)

    @pl.when((pl.program_id(0) == 0) & (pl.program_id(1) == 0) & (qi == 0))
    def _():
        prepare_queries(q_ref, 0, 0)

    bound = jnp.sqrt(q2max_ref[slot, 0:1, 0:1] * k2max_ref[0:1, 0:1]) * SCORE_BOUND_SLACK
    bound_is_safe = bound[0, 0] <= MAX_SAFE_SCORE_BOUND

    def attend(shift_of):
        prepare_queries(qnext_ref, (qi + 1) % n_q, 1 - slot)
        kn = kn_ref[...]
        v1 = v1_ref[...]
        for h in range(A_GROUP):
            cols = slice(h * HEAD_DIM, (h + 1) * HEAD_DIM)
            s = lax.dot_general(qs_ref[slot, :, cols], kn, (((1,), (1,)), ((), ())),
                                preferred_element_type=F32)
            p = jnp.exp2(s - shift_of(s))
            pv = jnp.dot(p.astype(BF16), v1, preferred_element_type=F32)
            o, l = pv[:, :HEAD_DIM], pv[:, HEAD_DIM:]
            g = ga_ref[0, :, cols].astype(F32)
            o_ref[0, :, cols] = (o * (1.0 / l) * _silu(g)).astype(BF16)

    @pl.when(bound_is_safe)
    def _():
        attend(lambda s: bound)

    @pl.when(jnp.logical_not(bound_is_safe))
    def _():
        attend(lambda s: jnp.max(s, axis=-1, keepdims=True))


def _attention_a(proj3, q_gain, k_gain, tables):
    B, S, _ = proj3.shape
    tq = ATTN_Q_TILE
    n_q = S // tq
    gw = A_GROUP * HEAD_DIM
    ka_blk = _OFF_KA // HEAD_DIM
    va_blk = _OFF_VA // HEAD_DIM
    ga_blk = _OFF_GA // gw
    full = pl.BlockSpec((S, HEAD_DIM), lambda b, g, i: (0, 0))
    vec = pl.BlockSpec((1, HEAD_DIM), lambda b, g, i: (0, 0))
    assert n_q % 2 == 0

    def next_step_q(b, g, i):
        n = jnp.minimum((b * A_KV_HEADS + g) * n_q + i + 1, B * A_KV_HEADS * n_q - 1)
        return n // (A_KV_HEADS * n_q), n % n_q, (n // n_q) % A_KV_HEADS

    return pl.pallas_call(
        functools.partial(_attn_a_kernel, tq=tq, n_q=n_q),
        grid=(B, A_KV_HEADS, n_q),
        in_specs=[
            pl.BlockSpec((1, tq, gw), lambda b, g, i: (b, i, g)),
            pl.BlockSpec((1, tq, gw), next_step_q),
            pl.BlockSpec((1, S, HEAD_DIM), lambda b, g, i: (b, 0, ka_blk + g)),
            pl.BlockSpec((1, S, HEAD_DIM), lambda b, g, i: (b, 0, va_blk + g)),
            pl.BlockSpec((1, tq, gw), lambda b, g, i: (b, i, ga_blk + g)),
            vec, vec, full, full,
        ],
        out_specs=pl.BlockSpec((1, tq, gw), lambda b, g, i: (b, i, g)),
        out_shape=jax.ShapeDtypeStruct((B, S, A_WIDTH), BF16),
        scratch_shapes=[
            pltpu.VMEM((S, HEAD_DIM), BF16),
            pltpu.VMEM((S, 2 * HEAD_DIM), BF16),
            pltpu.VMEM((8, HEAD_DIM), F32),
            pltpu.VMEM((2, tq, gw), BF16),
            pltpu.VMEM((2, 8, HEAD_DIM), F32),
        ],
        compiler_params=pltpu.CompilerParams(
            dimension_semantics=("arbitrary", "arbitrary", "arbitrary"),
            vmem_limit_bytes=V7X_VMEM_LIMIT_BYTES),
        name="attention_a",
    )(proj3, proj3, proj3, proj3, proj3, q_gain, k_gain, *tables)


def _band_tile(q, kw, vw, valid):
    s = lax.dot_general(q, kw, (((1,), (1,)), ((), ())), preferred_element_type=F32)
    s = jnp.where(valid, s, MASK_VALUE)
    m = jnp.max(s, axis=-1, keepdims=True)
    p = jnp.exp2(s - m)
    pv = jnp.dot(p.astype(BF16), vw, preferred_element_type=F32)
    o, l = pv[:, :HEAD_DIM], pv[:, HEAD_DIM:]
    return o * (1.0 / l), m * LN_2 + jnp.log(l)


def _window_mask(t, n_tiles):
    qi = lax.broadcasted_iota(jnp.int32, (BAND_TQ, BAND_WIN), 0)
    kj = lax.broadcasted_iota(jnp.int32, (BAND_TQ, BAND_WIN), 1)
    kpos = kj + (t * BAND_TQ - BAND_HALF)
    return (kj >= qi) & (kj <= qi + 2 * BAND_HALF) & (kpos >= 0) & (kpos < n_tiles * BAND_TQ)


def _zero_pads(ref, lead, n_rows):
    zeros = jnp.zeros((BAND_HALF, HEAD_DIM), BF16)
    lanes = slice(0, HEAD_DIM)
    for idx in lead:
        ref[idx + (slice(0, BAND_HALF), lanes)] = zeros
        ref[idx + (slice(BAND_HALF + n_rows, 2 * BAND_HALF + n_rows), lanes)] = zeros


def _band_mix_kernel(q1_ref, k1_ref, v1_ref, q4_ref, k4_ref, v4_ref, q16_ref, k16_ref, v16_ref,
                     gb_ref, cos_ref, sin_ref, perm_ref, y_ref,
                     nat_ref, tmp_ref, qd1_ref, kp1_ref, vp1_ref, qd4_ref, kp4_ref, vp4_ref,
                     qd16_ref, kd16_ref, vd16_ref, od16_ref, og_ref, lg_ref, *, S):
    L4 = S // SPLIT
    L16 = L4 // SPLIT
    n1, n4 = S // BAND_TQ, L4 // BAND_TQ
    per_tile = BAND_TQ // L16

    def rotated(src_ref, scale=None):
        x = src_ref[0]
        partner = jnp.dot(x, perm_ref[...], preferred_element_type=F32)
        y = x.astype(F32) * cos_ref[...] + partner * sin_ref[...]
        return y if scale is None else y * scale

    lanes = slice(0, HEAD_DIM)
    vp1_ref[:, HEAD_DIM:] = jnp.ones((vp1_ref.shape[0], HEAD_DIM), BF16)
    vp4_ref[:, :, HEAD_DIM:] = jnp.ones(vp4_ref.shape[:2] + (HEAD_DIM,), BF16)
    vd16_ref[:, :, HEAD_DIM:] = jnp.ones(vd16_ref.shape[:2] + (HEAD_DIM,), BF16)

    _zero_pads(kp1_ref, [()], S)
    _zero_pads(vp1_ref, [()], S)
    qd1_ref[...] = rotated(q1_ref, Q_SCALE).astype(BF16)
    kp1_ref[BAND_HALF:BAND_HALF + S, :] = rotated(k1_ref).astype(BF16)
    vp1_ref[BAND_HALF:BAND_HALF + S, lanes] = v1_ref[0]

    for t in range(n1):
        rows = slice(t * BAND_TQ, (t + 1) * BAND_TQ)
        win = slice(t * BAND_TQ, t * BAND_TQ + BAND_WIN)
        o, lse = _band_tile(qd1_ref[rows, :], kp1_ref[win, :], vp1_ref[win, :],
                            _window_mask(t, n1))
        og_ref[0, rows, :] = o
        lg_ref[0, rows, :] = lse

    _zero_pads(kp4_ref, [(r,) for r in range(SPLIT)], L4)
    _zero_pads(vp4_ref, [(r,) for r in range(SPLIT)], L4)
    for n, (val, dst_ref, lead) in enumerate(((rotated(q4_ref, Q_SCALE), qd4_ref, 0),
                                              (rotated(k4_ref), kp4_ref, BAND_HALF),
                                              (v4_ref[0].astype(F32), vp4_ref, BAND_HALF))):
        nat_ref[n] = val
        for r in range(SPLIT):
            dst_ref[r, lead:lead + L4, lanes] = \
                nat_ref[n, pl.ds(r, L4, stride=SPLIT), :].astype(BF16)

    for r in range(SPLIT):
        for t in range(n4):
            rows = pl.ds(r + t * (BAND_TQ * SPLIT), BAND_TQ, stride=SPLIT)
            win = slice(t * BAND_TQ, t * BAND_TQ + BAND_WIN)
            o, lse = _band_tile(qd4_ref[r, t * BAND_TQ:(t + 1) * BAND_TQ, :],
                                kp4_ref[r, win, :], vp4_ref[r, win, :], _window_mask(t, n4))
            og_ref[1, rows, :] = o
            lg_ref[1, rows, :] = lse

    for n, (val, dst_ref) in enumerate(((rotated(q16_ref, Q_SCALE), qd16_ref),
                                        (rotated(k16_ref), kd16_ref),
                                        (v16_ref[0].astype(F32), vd16_ref))):
        nat_ref[3 + n] = val
        for a in range(SPLIT):
            tmp_ref[n, a] = nat_ref[3 + n, pl.ds(a, L4, stride=SPLIT), :]
            for b in range(SPLIT):
                r = a + SPLIT * b
                dst_ref[r // per_tile, (r % per_tile) * L16:(r % per_tile + 1) * L16, lanes] = \
                    tmp_ref[n, a, pl.ds(b, L16, stride=SPLIT), :].astype(BF16)

    qi = lax.broadcasted_iota(jnp.int32, (BAND_TQ, BAND_TQ), 0)
    kj = lax.broadcasted_iota(jnp.int32, (BAND_TQ, BAND_TQ), 1)
    same_residue = functools.reduce(
        jnp.logical_or, [(qi >= c * L16) & (qi < (c + 1) * L16) & (kj >= c * L16)
                         & (kj < (c + 1) * L16) for c in range(per_tile)])
    stacked_band = same_residue & (kj >= qi - BAND_HALF) & (kj <= qi + BAND_HALF)
    for j in range(SPLIT * SPLIT // per_tile):
        o, lse = _band_tile(qd16_ref[j], kd16_ref[j], vd16_ref[j], stacked_band)
        od16_ref[0, j] = o
        od16_ref[1, j] = lse

    for n, dst_ref in enumerate((og_ref, lg_ref)):
        for a in range(SPLIT):
            for b in range(SPLIT):
                r = a + SPLIT * b
                tmp_ref[n, a, pl.ds(b, L16, stride=SPLIT), :] = \
                    od16_ref[n, r // per_tile, (r % per_tile) * L16:(r % per_tile + 1) * L16, :]
            dst_ref[2, pl.ds(a, L4, stride=SPLIT), :] = tmp_ref[n, a]

    chunk = MIX_ROWS

    def mix(i, carry):
        rows = pl.ds(pl.multiple_of(i * chunk, chunk), chunk)
        lses = [lg_ref[g, rows, :] for g in range(B_GROUPS)]
        lmax = functools.reduce(jnp.maximum, lses)
        es = [jnp.exp(l - lmax) for l in lses]
        num = sum(e * og_ref[g, rows, :] for g, e in enumerate(es))
        ob = num * (1.0 / sum(es))
        y_ref[0, rows, :] = (ob * _silu(gb_ref[0, rows, :].astype(F32))).astype(BF16)
        return carry

    lax.fori_loop(0, S // chunk, mix, 0)


def _band_mix(proj3, tables):
    B, S, NC = proj3.shape
    assert tuple(d for _, d in B_PATTERNS) == (1, SPLIT, SPLIT * SPLIT)
    L4, L16 = S // SPLIT, S // (SPLIT * SPLIT)
    assert L4 % BAND_TQ == 0 and BAND_TQ % L16 == 0
    n16 = S // BAND_TQ

    def head_blk(off, g):
        first = off // HEAD_DIM + g * B_HEADS
        return pl.BlockSpec((1, S, HEAD_DIM), lambda b, h: (b, 0, first + h))

    qkv_specs = [head_blk(off, g) for g in range(B_GROUPS)
                 for off in (_OFF_QB, _OFF_KB, _OFF_VB)]
    gb_first = _OFF_GB // HEAD_DIM
    table = pl.BlockSpec((S, HEAD_DIM), lambda b, h: (0, 0))
    return pl.pallas_call(
        functools.partial(_band_mix_kernel, S=S),
        grid=(B, B_HEADS),
        in_specs=qkv_specs
        + [pl.BlockSpec((1, S, HEAD_DIM), lambda b, h: (b, 0, gb_first + h)), table, table,
           pl.BlockSpec((HEAD_DIM, HEAD_DIM), lambda b, h: (0, 0))],
        out_specs=pl.BlockSpec((1, S, HEAD_DIM), lambda b, h: (b, 0, h)),
        out_shape=jax.ShapeDtypeStruct((B, S, B_WIDTH), BF16),
        scratch_shapes=[
            pltpu.VMEM((6, S, HEAD_DIM), F32),
            pltpu.VMEM((3, SPLIT, L4, HEAD_DIM), F32),
            pltpu.VMEM((S, HEAD_DIM), BF16),
            pltpu.VMEM((S + 2 * BAND_HALF, HEAD_DIM), BF16),
            pltpu.VMEM((S + 2 * BAND_HALF, 2 * HEAD_DIM), BF16),
            pltpu.VMEM((SPLIT, L4, HEAD_DIM), BF16),
            pltpu.VMEM((SPLIT, L4 + 2 * BAND_HALF, HEAD_DIM), BF16),
            pltpu.VMEM((SPLIT, L4 + 2 * BAND_HALF, 2 * HEAD_DIM), BF16),
            pltpu.VMEM((n16, BAND_TQ, HEAD_DIM), BF16),
            pltpu.VMEM((n16, BAND_TQ, HEAD_DIM), BF16),
            pltpu.VMEM((n16, BAND_TQ, 2 * HEAD_DIM), BF16),
            pltpu.VMEM((2, n16, BAND_TQ, HEAD_DIM), F32),
            pltpu.VMEM((B_GROUPS, S, HEAD_DIM), F32),
            pltpu.VMEM((B_GROUPS, S, HEAD_DIM), F32),
        ],
        compiler_params=pltpu.CompilerParams(
            dimension_semantics=("parallel", "parallel"),
            vmem_limit_bytes=V7X_VMEM_LIMIT_BYTES),
        name="band_mix",
    )(*([proj3] * (3 * B_GROUPS + 1)), *tables, _partner_permutation(PARTIAL_ROPE_DIM))


def _merge_kernel(x_ref, ya_ref, yb_ref, za0, za1, za2, za3, zb0, zb1, zb2, zb3, bias_ref,
                  wa_ref, wb_ref, wo_ref, fg_ref, out_ref, merged_ref, *, final_norm):
    pa = jnp.dot(ya_ref[...], wa_ref[...], preferred_element_type=F32)
    pb = jnp.dot(yb_ref[...], wb_ref[...], preferred_element_type=F32)
    for c, (za, zb) in enumerate(((za0, zb0), (za1, zb1), (za2, zb2), (za3, zb3))):
        cols = slice(c * COL_BLK, (c + 1) * COL_BLK)
        gate_a = jax.nn.sigmoid(za[...].astype(F32) + bias_ref[0:1, cols])
        gate_b = jax.nn.sigmoid(zb[...].astype(F32) + bias_ref[1:2, cols])
        merged_ref[:, cols] = (gate_a * pa[:, cols] + gate_b * pb[:, cols]).astype(BF16)

    y = x_ref[...] + jnp.dot(merged_ref[...], wo_ref[...], preferred_element_type=F32)
    out_ref[...] = _rms(y, fg_ref[...]) if final_norm else y


def _merge(x2, ya, yb, proj2, bias, wa, wb, wo, final_gain, final_norm):
    T, D = x2.shape
    tm = MERGE_ROW_TILE
    za_blk = _OFF_ZA // COL_BLK
    n_z = D // COL_BLK

    def row_blk(width, col=0):
        return pl.BlockSpec((tm, width), lambda i: (i, col))

    def whole(a):
        return pl.BlockSpec(a.shape, lambda i: (0, 0))

    z_specs = [row_blk(COL_BLK, za_blk + c) for c in range(2 * n_z)]
    return pl.pallas_call(
        functools.partial(_merge_kernel, final_norm=final_norm),
        grid=(T // tm,),
        in_specs=[row_blk(D), row_blk(A_WIDTH), row_blk(B_WIDTH)] + z_specs
        + [whole(bias), whole(wa), whole(wb), whole(wo), whole(final_gain)],
        out_specs=row_blk(D),
        out_shape=jax.ShapeDtypeStruct((T, D), F32),
        scratch_shapes=[pltpu.VMEM((tm, D), BF16)],
        compiler_params=pltpu.CompilerParams(
            dimension_semantics=("parallel",),
            vmem_limit_bytes=V7X_VMEM_LIMIT_BYTES),
        name="merge_output",
    )(x2, ya, yb, *([proj2] * (2 * n_z)), bias, wa, wb, wo, final_gain)


def _angles(pos, dim, theta):
    expo = np.arange(0, dim, 2, dtype=np.float64) / dim
    return pos.astype(np.float64)[:, None] / np.power(float(theta), expo)[None, :]


def _lane_tables(ang_blocks):
    cos, sin = [], []
    for blk in ang_blocks:
        if isinstance(blk, int):
            S = cos[0].shape[0]
            cos.append(np.ones((S, blk)))
            sin.append(np.zeros((S, blk)))
            continue
        c, s = np.cos(blk), np.sin(blk)
        cos += [c, c]
        sin += [-s, s]
    return tuple(jnp.asarray(np.concatenate(t, axis=-1), dtype=F32) for t in (cos, sin))


def _partner_permutation(rotary_dim):
    half = rotary_dim // 2
    p = np.zeros((HEAD_DIM, HEAD_DIM))
    for i in range(rotary_dim):
        p[i + half if i < half else i - half, i] = 1.0
    return jnp.asarray(p, dtype=BF16)


def _axial_tables(S):
    pos = np.arange(S)
    half = HEAD_DIM // 2
    return _lane_tables([_angles(pos // GRID_W, half, AXIAL_THETA),
                         _angles(pos % GRID_W, half, AXIAL_THETA)])


def _partial_tables(S):
    pos = np.arange(S)
    return _lane_tables([_angles(pos, PARTIAL_ROPE_DIM, ROPE_THETA),
                         HEAD_DIM - PARTIAL_ROPE_DIM])


def kernel(x, norm_gain, w_in, q_norm_gain, k_norm_gain, merge_gate_bias, w_branch_a,
           w_branch_b, w_out, final_norm_gain):
    B, S, D = x.shape
    depth = norm_gain.shape[0]
    n_cols = _OFF_ZA + 2 * D
    assert w_in.shape[2] == n_cols and S % GRID_W == 0
    assert all(w // (2 * d) == BAND_HALF for w, d in B_PATTERNS)
    axial = _axial_tables(S)
    partial = _partial_tables(S)
    x2 = x.reshape(B * S, D)
    for l in range(depth):
        proj2 = _input_projection(x2, norm_gain[l][None, :], w_in[l])
        proj3 = proj2.reshape(B, S, n_cols)
        ya = _attention_a(proj3, q_norm_gain[l][None, :], k_norm_gain[l][None, :], axial)
        yb = _band_mix(proj3, partial)
        x2 = _merge(x2, ya.reshape(B * S, A_WIDTH), yb.reshape(B * S, B_WIDTH), proj2,
                    merge_gate_bias[l], w_branch_a[l].astype(BF16),
                    w_branch_b[l].astype(BF16), w_out[l].astype(BF16),
                    final_norm_gain[None, :], final_norm=(l == depth - 1))
    return x2.reshape(B, S, D)
```

```python
import functools
import math

import numpy as np
import jax
import jax.numpy as jnp
from jax import lax
from jax.experimental import pallas as pl
from jax.experimental.pallas import tpu as pltpu

F32 = jnp.float32
BF16 = jnp.bfloat16

HEAD_DIM = 128
GRID_W = 64
NORM_EPS = 1e-6
A_Q_HEADS = 8
A_KV_HEADS = 2
A_GROUP = A_Q_HEADS // A_KV_HEADS
A_WIDTH = A_Q_HEADS * HEAD_DIM
A_KV_WIDTH = A_KV_HEADS * HEAD_DIM
AXIAL_THETA = 10000.0
AXIAL_SHIFT = HEAD_DIM // 4
MAX_SAFE_SCORE_BOUND = 50.0
SCORE_BOUND_SLACK = 1.01
B_PATTERNS = ((128, 1), (512, 4), (2048, 16))
B_GROUPS = len(B_PATTERNS)
B_HEADS = 4
B_WIDTH = B_HEADS * HEAD_DIM
PARTIAL_ROPE_DIM = HEAD_DIM // 4
PARTIAL_SHIFT = PARTIAL_ROPE_DIM // 2
ROPE_THETA = 500000.0
BAND_HALF = 64
BAND_TQ = 128
BAND_WIN = BAND_TQ + 2 * BAND_HALF
MIX_ROWS = 256
SPLIT = 4
MASK_VALUE = -1e30
LOG2_E = math.log2(math.e)
Q_SCALE = HEAD_DIM ** -0.5 * LOG2_E

COL_BLK = 512
_OFF_QA = 0
_OFF_KA = A_WIDTH
_OFF_VA = _OFF_KA + A_KV_WIDTH
_OFF_GA = _OFF_VA + A_KV_WIDTH
_OFF_QB = _OFF_GA + A_WIDTH
_OFF_KB = _OFF_QB + B_GROUPS * B_WIDTH
_OFF_VB = _OFF_KB + B_GROUPS * B_WIDTH
_OFF_GB = _OFF_VB + B_GROUPS * B_WIDTH
_OFF_ZA = _OFF_GB + B_WIDTH

V7X_VMEM_LIMIT_BYTES = 56 * 1024 * 1024
PROJ_ROW_TILE = 1024
PROJ_COL_TILE = 2 * COL_BLK
PROJ_ROW_TILES_PER_WEIGHT_TILE = 2
ATTN_Q_TILE = 512
MERGE_ROW_TILE = 512


def _rotary(x, cos, sin, shift):
    lane = lax.broadcasted_iota(jnp.int32, x.shape, 1)
    up = pltpu.roll(x, HEAD_DIM - shift, 1)
    dn = pltpu.roll(x, shift, 1)
    partner = jnp.where(lane % (2 * shift) < shift, up, dn)
    return x * cos + partner * sin


def _rms(x, gain):
    ms = jnp.mean(x * x, axis=-1, keepdims=True)
    return x * lax.rsqrt(ms + NORM_EPS) * gain


def _silu(g):
    return g * jax.nn.sigmoid(g)


def _proj_kernel(x_ref, g_ref, w_ref, o_ref, h_ref, wb_ref, *, n_full, tail):
    j, r = pl.program_id(1), pl.program_id(2)

    def project(width, first_tile):
        cols = slice(0, width)

        def normed_rows():
            if not first_tile:
                return h_ref[r]
            h = _rms(x_ref[...], g_ref[...]).astype(BF16)
            h_ref[r] = h
            return h

        @pl.when(r == 0)
        def _():
            w = w_ref[:, cols].astype(BF16)
            wb_ref[:, cols] = w
            o_ref[:, cols] = jnp.dot(normed_rows(), w, preferred_element_type=F32).astype(BF16)

        @pl.when(r != 0)
        def _():
            o_ref[:, cols] = jnp.dot(normed_rows(), wb_ref[:, cols],
                                     preferred_element_type=F32).astype(BF16)

    assert n_full >= 1

    @pl.when(j == 0)
    def _():
        project(w_ref.shape[1], True)

    @pl.when((j > 0) & (j < n_full))
    def _():
        project(w_ref.shape[1], False)

    if tail:
        @pl.when(j == n_full)
        def _():
            project(tail, False)


def _input_projection(x2, gain, w):
    T, D = x2.shape
    N = w.shape[1]
    tm, tn, R = PROJ_ROW_TILE, PROJ_COL_TILE, PROJ_ROW_TILES_PER_WEIGHT_TILE
    n_full, tail = divmod(N, tn)

    def x_index(g, j, r):
        return g * R + jnp.where(j == 0, r, R - 1), 0

    return pl.pallas_call(
        functools.partial(_proj_kernel, n_full=n_full, tail=tail),
        grid=(T // (tm * R), pl.cdiv(N, tn), R),
        in_specs=[
            pl.BlockSpec((tm, D), x_index),
            pl.BlockSpec((1, D), lambda g, j, r: (0, 0)),
            pl.BlockSpec((D, tn), lambda g, j, r: (0, j)),
        ],
        out_specs=pl.BlockSpec((tm, tn), lambda g, j, r: (g * R + r, j)),
        out_shape=jax.ShapeDtypeStruct((T, N), BF16),
        scratch_shapes=[pltpu.VMEM((R, tm, D), BF16), pltpu.VMEM((D, tn), BF16)],
        compiler_params=pltpu.CompilerParams(
            dimension_semantics=("parallel", "arbitrary", "arbitrary"),
            vmem_limit_bytes=V7X_VMEM_LIMIT_BYTES),
        name="input_projection",
    )(x2, gain, w)


def _attn_a_kernel(q_ref, qnext_ref, k_ref, v_ref, ga_ref, gq_ref, gk_ref, cos_ref, sin_ref,
                   o_ref, kn_ref, v1_ref, k2max_ref, qs_ref, q2max_ref, *, tq, n_q):
    qi = pl.program_id(2)
    slot = qi % 2

    def max_sq_norm(x_bf16):
        xf = x_bf16.astype(F32)
        return jnp.max(jnp.sum(xf * xf, axis=-1, keepdims=True), axis=0, keepdims=True)

    def prepare_queries(src_ref, tile, dst):
        rows = pl.ds(pl.multiple_of(tile * tq, tq), tq)
        cos, sin = cos_ref[rows, :], sin_ref[rows, :]
        q2max = None
        for h in range(A_GROUP):
            cols = slice(h * HEAD_DIM, (h + 1) * HEAD_DIM)
            q = _rms(src_ref[0, :, cols].astype(F32), gq_ref[...])
            q = (_rotary(q, cos, sin, AXIAL_SHIFT) * Q_SCALE).astype(BF16)
            qs_ref[dst, :, cols] = q
            q2max = max_sq_norm(q) if q2max is None else jnp.maximum(q2max, max_sq_norm(q))
        q2max_ref[dst] = jnp.broadcast_to(q2max, q2max_ref.shape[1:])

    @pl.when(qi == 0)
    def _():
        k = _rms(k_ref[0].astype(F32), gk_ref[...])
        k = _rotary(k, cos_ref[...], sin_ref[...], AXIAL_SHIFT).astype(BF16)
        kn_ref[...] = k
        k2max_ref[...] = jnp.broadcast_to(max_sq_norm(k), k2max_ref.shape)
        v1_ref[:, 0:HEAD_DIM] = v_ref[0]
        v1_ref[:, HEAD_DIM:] = jnp.ones((v1_ref.shape[0], HEAD_DIM), BF16)

    @pl.when((pl.program_id(0) == 0) & (pl.program_id(1) == 0) & (qi == 0))
    def _():
        prepare_queries(q_ref, 0, 0)

    bound = jnp.sqrt(q2max_ref[slot, 0:1, 0:1] * k2max_ref[0:1, 0:1]) * SCORE_BOUND_SLACK
    bound_is_safe = bound[0, 0] <= MAX_SAFE_SCORE_BOUND

    def attend(shift_of):
        prepare_queries(qnext_ref, (qi + 1) % n_q, 1 - slot)
        kn = kn_ref[...]
        v1 = v1_ref[...]
        for h in range(A_GROUP):
            cols = slice(h * HEAD_DIM, (h + 1) * HEAD_DIM)
            s = lax.dot_general(qs_ref[slot, :, cols], kn, (((1,), (1,)), ((), ())),
                                preferred_element_type=F32)
            p = jnp.exp2(s - shift_of(s))
            pv = jnp.dot(p.astype(BF16), v1, preferred_element_type=F32)
            o, l = pv[:, :HEAD_DIM], pv[:, HEAD_DIM:]
            g = ga_ref[0, :, cols].astype(F32)
            o_ref[0, :, cols] = (o * (1.0 / l) * _silu(g)).astype(BF16)

    @pl.when(bound_is_safe)
    def _():
        attend(lambda s: bound)

    @pl.when(jnp.logical_not(bound_is_safe))
    def _():
        attend(lambda s: jnp.max(s, axis=-1, keepdims=True))


def _attention_a(proj3, q_gain, k_gain, tables):
    B, S, _ = proj3.shape
    tq = ATTN_Q_TILE
    n_q = S // tq
    gw = A_GROUP * HEAD_DIM
    ka_blk = _OFF_KA // HEAD_DIM
    va_blk = _OFF_VA // HEAD_DIM
    ga_blk = _OFF_GA // gw
    full = pl.BlockSpec((S, HEAD_DIM), lambda b, g, i: (0, 0))
    vec = pl.BlockSpec((1, HEAD_DIM), lambda b, g, i: (0, 0))
    assert n_q % 2 == 0

    def next_step_q(b, g, i):
        n = jnp.minimum((b * A_KV_HEADS + g) * n_q + i + 1, B * A_KV_HEADS * n_q - 1)
        return n // (A_KV_HEADS * n_q), n % n_q, (n // n_q) % A_KV_HEADS

    return pl.pallas_call(
        functools.partial(_attn_a_kernel, tq=tq, n_q=n_q),
        grid=(B, A_KV_HEADS, n_q),
        in_specs=[
            pl.BlockSpec((1, tq, gw), lambda b, g, i: (b, i, g)),
            pl.BlockSpec((1, tq, gw), next_step_q),
            pl.BlockSpec((1, S, HEAD_DIM), lambda b, g, i: (b, 0, ka_blk + g)),
            pl.BlockSpec((1, S, HEAD_DIM), lambda b, g, i: (b, 0, va_blk + g)),
            pl.BlockSpec((1, tq, gw), lambda b, g, i: (b, i, ga_blk + g)),
            vec, vec, full, full,
        ],
        out_specs=pl.BlockSpec((1, tq, gw), lambda b, g, i: (b, i, g)),
        out_shape=jax.ShapeDtypeStruct((B, S, A_WIDTH), BF16),
        scratch_shapes=[
            pltpu.VMEM((S, HEAD_DIM), BF16),
            pltpu.VMEM((S, 2 * HEAD_DIM), BF16),
            pltpu.VMEM((8, HEAD_DIM), F32),
            pltpu.VMEM((2, tq, gw), BF16),
            pltpu.VMEM((2, 8, HEAD_DIM), F32),
        ],
        compiler_params=pltpu.CompilerParams(
            dimension_semantics=("arbitrary", "arbitrary", "arbitrary"),
            vmem_limit_bytes=V7X_VMEM_LIMIT_BYTES),
        name="attention_a",
    )(proj3, proj3, proj3, proj3, proj3, q_gain, k_gain, *tables)


def _band_tile(q, kw, vw, valid):
    s = lax.dot_general(q, kw, (((1,), (1,)), ((), ())), preferred_element_type=F32)
    s = jnp.where(valid, s, MASK_VALUE)
    m = jnp.max(s, axis=-1, keepdims=True)
    p = jnp.exp2(s - m)
    pv = jnp.dot(p.astype(BF16), vw, preferred_element_type=F32)
    o, l = pv[:, :HEAD_DIM], pv[:, HEAD_DIM:]
    return o * (1.0 / l), m + jnp.log2(l)


def _window_mask(t, n_tiles):
    qi = lax.broadcasted_iota(jnp.int32, (BAND_TQ, BAND_WIN), 0)
    kj = lax.broadcasted_iota(jnp.int32, (BAND_TQ, BAND_WIN), 1)
    kpos = kj + (t * BAND_TQ - BAND_HALF)
    return (kj >= qi) & (kj <= qi + 2 * BAND_HALF) & (kpos >= 0) & (kpos < n_tiles * BAND_TQ)


def _zero_pads(ref, lead, n_rows):
    zeros = jnp.zeros((BAND_HALF, HEAD_DIM), BF16)
    lanes = slice(0, HEAD_DIM)
    for idx in lead:
        ref[idx + (slice(0, BAND_HALF), lanes)] = zeros
        ref[idx + (slice(BAND_HALF + n_rows, 2 * BAND_HALF + n_rows), lanes)] = zeros


def _band_mix_kernel(q1_ref, k1_ref, v1_ref, q4_ref, k4_ref, v4_ref, q16_ref, k16_ref, v16_ref,
                     gb_ref, cos_ref, sin_ref, perm_ref, y_ref,
                     nat_ref, tmp_ref, qd1_ref, kp1_ref, vp1_ref, qd4_ref, kp4_ref, vp4_ref,
                     qd16_ref, kd16_ref, vd16_ref, od16_ref, og_ref, lg_ref, *, S):
    L4 = S // SPLIT
    L16 = L4 // SPLIT
    n1, n4 = S // BAND_TQ, L4 // BAND_TQ
    per_tile = BAND_TQ // L16

    def rotated(src_ref, scale=None):
        x = src_ref[0]
        partner = jnp.dot(x, perm_ref[...], preferred_element_type=F32)
        y = x.astype(F32) * cos_ref[...] + partner * sin_ref[...]
        return y if scale is None else y * scale

    lanes = slice(0, HEAD_DIM)
    vp1_ref[:, HEAD_DIM:] = jnp.ones((vp1_ref.shape[0], HEAD_DIM), BF16)
    vp4_ref[:, :, HEAD_DIM:] = jnp.ones(vp4_ref.shape[:2] + (HEAD_DIM,), BF16)
    vd16_ref[:, :, HEAD_DIM:] = jnp.ones(vd16_ref.shape[:2] + (HEAD_DIM,), BF16)

    _zero_pads(kp1_ref, [()], S)
    _zero_pads(vp1_ref, [()], S)
    qd1_ref[...] = rotated(q1_ref, Q_SCALE).astype(BF16)
    kp1_ref[BAND_HALF:BAND_HALF + S, :] = rotated(k1_ref).astype(BF16)
    vp1_ref[BAND_HALF:BAND_HALF + S, lanes] = v1_ref[0]

    for t in range(n1):
        rows = slice(t * BAND_TQ, (t + 1) * BAND_TQ)
        win = slice(t * BAND_TQ, t * BAND_TQ + BAND_WIN)
        o, lse = _band_tile(qd1_ref[rows, :], kp1_ref[win, :], vp1_ref[win, :],
                            _window_mask(t, n1))
        og_ref[0, rows, :] = o
        lg_ref[0, rows, :] = lse

    _zero_pads(kp4_ref, [(r,) for r in range(SPLIT)], L4)
    _zero_pads(vp4_ref, [(r,) for r in range(SPLIT)], L4)
    for n, (val, dst_ref, lead) in enumerate(((rotated(q4_ref, Q_SCALE), qd4_ref, 0),
                                              (rotated(k4_ref), kp4_ref, BAND_HALF),
                                              (v4_ref[0].astype(F32), vp4_ref, BAND_HALF))):
        nat_ref[n] = val
        for r in range(SPLIT):
            dst_ref[r, lead:lead + L4, lanes] = \
                nat_ref[n, pl.ds(r, L4, stride=SPLIT), :].astype(BF16)

    for r in range(SPLIT):
        for t in range(n4):
            rows = pl.ds(r + t * (BAND_TQ * SPLIT), BAND_TQ, stride=SPLIT)
            win = slice(t * BAND_TQ, t * BAND_TQ + BAND_WIN)
            o, lse = _band_tile(qd4_ref[r, t * BAND_TQ:(t + 1) * BAND_TQ, :],
                                kp4_ref[r, win, :], vp4_ref[r, win, :], _window_mask(t, n4))
            og_ref[1, rows, :] = o
            lg_ref[1, rows, :] = lse

    for n, (val, dst_ref) in enumerate(((rotated(q16_ref, Q_SCALE), qd16_ref),
                                        (rotated(k16_ref), kd16_ref),
                                        (v16_ref[0].astype(F32), vd16_ref))):
        nat_ref[3 + n] = val
        for a in range(SPLIT):
            tmp_ref[n, a] = nat_ref[3 + n, pl.ds(a, L4, stride=SPLIT), :]
            for b in range(SPLIT):
                r = a + SPLIT * b
                dst_ref[r // per_tile, (r % per_tile) * L16:(r % per_tile + 1) * L16, lanes] = \
                    tmp_ref[n, a, pl.ds(b, L16, stride=SPLIT), :].astype(BF16)

    qi = lax.broadcasted_iota(jnp.int32, (BAND_TQ, BAND_TQ), 0)
    kj = lax.broadcasted_iota(jnp.int32, (BAND_TQ, BAND_TQ), 1)
    same_residue = functools.reduce(
        jnp.logical_or, [(qi >= c * L16) & (qi < (c + 1) * L16) & (kj >= c * L16)
                         & (kj < (c + 1) * L16) for c in range(per_tile)])
    stacked_band = same_residue & (kj >= qi - BAND_HALF) & (kj <= qi + BAND_HALF)
    for j in range(SPLIT * SPLIT // per_tile):
        o, lse = _band_tile(qd16_ref[j], kd16_ref[j], vd16_ref[j], stacked_band)
        od16_ref[0, j] = o
        od16_ref[1, j] = lse

    for n, dst_ref in enumerate((og_ref, lg_ref)):
        for a in range(SPLIT):
            for b in range(SPLIT):
                r = a + SPLIT * b
                tmp_ref[n, a, pl.ds(b, L16, stride=SPLIT), :] = \
                    od16_ref[n, r // per_tile, (r % per_tile) * L16:(r % per_tile + 1) * L16, :]
            dst_ref[2, pl.ds(a, L4, stride=SPLIT), :] = tmp_ref[n, a]

    chunk = MIX_ROWS

    def mix(i, carry):
        rows = pl.ds(pl.multiple_of(i * chunk, chunk), chunk)
        lses = [lg_ref[g, rows, :] for g in range(B_GROUPS)]
        lmax = functools.reduce(jnp.maximum, lses)
        es = [jnp.exp2(l - lmax) for l in lses]
        num = sum(e * og_ref[g, rows, :] for g, e in enumerate(es))
        ob = num * (1.0 / sum(es))
        y_ref[0, rows, :] = (ob * _silu(gb_ref[0, rows, :].astype(F32))).astype(BF16)
        return carry

    lax.fori_loop(0, S // chunk, mix, 0)


def _band_mix(proj3, tables):
    B, S, NC = proj3.shape
    assert tuple(d for _, d in B_PATTERNS) == (1, SPLIT, SPLIT * SPLIT)
    L4, L16 = S // SPLIT, S // (SPLIT * SPLIT)
    assert L4 % BAND_TQ == 0 and BAND_TQ % L16 == 0
    n16 = S // BAND_TQ

    def head_blk(off, g):
        first = off // HEAD_DIM + g * B_HEADS
        return pl.BlockSpec((1, S, HEAD_DIM), lambda b, h: (b, 0, first + h))

    qkv_specs = [head_blk(off, g) for g in range(B_GROUPS)
                 for off in (_OFF_QB, _OFF_KB, _OFF_VB)]
    gb_first = _OFF_GB // HEAD_DIM
    table = pl.BlockSpec((S, HEAD_DIM), lambda b, h: (0, 0))
    return pl.pallas_call(
        functools.partial(_band_mix_kernel, S=S),
        grid=(B, B_HEADS),
        in_specs=qkv_specs
        + [pl.BlockSpec((1, S, HEAD_DIM), lambda b, h: (b, 0, gb_first + h)), table, table,
           pl.BlockSpec((HEAD_DIM, HEAD_DIM), lambda b, h: (0, 0))],
        out_specs=pl.BlockSpec((1, S, HEAD_DIM), lambda b, h: (b, 0, h)),
        out_shape=jax.ShapeDtypeStruct((B, S, B_WIDTH), BF16),
        scratch_shapes=[
            pltpu.VMEM((6, S, HEAD_DIM), F32),
            pltpu.VMEM((3, SPLIT, L4, HEAD_DIM), F32),
            pltpu.VMEM((S, HEAD_DIM), BF16),
            pltpu.VMEM((S + 2 * BAND_HALF, HEAD_DIM), BF16),
            pltpu.VMEM((S + 2 * BAND_HALF, 2 * HEAD_DIM), BF16),
            pltpu.VMEM((SPLIT, L4, HEAD_DIM), BF16),
            pltpu.VMEM((SPLIT, L4 + 2 * BAND_HALF, HEAD_DIM), BF16),
            pltpu.VMEM((SPLIT, L4 + 2 * BAND_HALF, 2 * HEAD_DIM), BF16),
            pltpu.VMEM((n16, BAND_TQ, HEAD_DIM), BF16),
            pltpu.VMEM((n16, BAND_TQ, HEAD_DIM), BF16),
            pltpu.VMEM((n16, BAND_TQ, 2 * HEAD_DIM), BF16),
            pltpu.VMEM((2, n16, BAND_TQ, HEAD_DIM), F32),
            pltpu.VMEM((B_GROUPS, S, HEAD_DIM), F32),
            pltpu.VMEM((B_GROUPS, S, HEAD_DIM), F32),
        ],
        compiler_params=pltpu.CompilerParams(
            dimension_semantics=("parallel", "parallel"),
            vmem_limit_bytes=V7X_VMEM_LIMIT_BYTES),
        name="band_mix",
    )(*([proj3] * (3 * B_GROUPS + 1)), *tables, _partner_permutation(PARTIAL_ROPE_DIM))


def _merge_kernel(x_ref, ya_ref, yb_ref, za0, za1, za2, za3, zb0, zb1, zb2, zb3, bias_ref,
                  wa_ref, wb_ref, wo_ref, fg_ref, out_ref, merged_ref, *, final_norm):
    pa = jnp.dot(ya_ref[...], wa_ref[...], preferred_element_type=F32)
    pb = jnp.dot(yb_ref[...], wb_ref[...], preferred_element_type=F32)
    for c, (za, zb) in enumerate(((za0, zb0), (za1, zb1), (za2, zb2), (za3, zb3))):
        cols = slice(c * COL_BLK, (c + 1) * COL_BLK)
        gate_a = jax.nn.sigmoid(za[...].astype(F32) + bias_ref[0:1, cols])
        gate_b = jax.nn.sigmoid(zb[...].astype(F32) + bias_ref[1:2, cols])
        merged_ref[:, cols] = (gate_a * pa[:, cols] + gate_b * pb[:, cols]).astype(BF16)

    y = x_ref[...] + jnp.dot(merged_ref[...], wo_ref[...], preferred_element_type=F32)
    out_ref[...] = _rms(y, fg_ref[...]) if final_norm else y


def _merge(x2, ya, yb, proj2, bias, wa, wb, wo, final_gain, final_norm):
    T, D = x2.shape
    tm = MERGE_ROW_TILE
    za_blk = _OFF_ZA // COL_BLK
    n_z = D // COL_BLK

    def row_blk(width, col=0):
        return pl.BlockSpec((tm, width), lambda i: (i, col))

    def whole(a):
        return pl.BlockSpec(a.shape, lambda i: (0, 0))

    z_specs = [row_blk(COL_BLK, za_blk + c) for c in range(2 * n_z)]
    return pl.pallas_call(
        functools.partial(_merge_kernel, final_norm=final_norm),
        grid=(T // tm,),
        in_specs=[row_blk(D), row_blk(A_WIDTH), row_blk(B_WIDTH)] + z_specs
        + [whole(bias), whole(wa), whole(wb), whole(wo), whole(final_gain)],
        out_specs=row_blk(D),
        out_shape=jax.ShapeDtypeStruct((T, D), F32),
        scratch_shapes=[pltpu.VMEM((tm, D), BF16)],
        compiler_params=pltpu.CompilerParams(
            dimension_semantics=("parallel",),
            vmem_limit_bytes=V7X_VMEM_LIMIT_BYTES),
        name="merge_output",
    )(x2, ya, yb, *([proj2] * (2 * n_z)), bias, wa, wb, wo, final_gain)


def _angles(pos, dim, theta):
    expo = np.arange(0, dim, 2, dtype=np.float64) / dim
    return pos.astype(np.float64)[:, None] / np.power(float(theta), expo)[None, :]


def _lane_tables(ang_blocks):
    cos, sin = [], []
    for blk in ang_blocks:
        if isinstance(blk, int):
            S = cos[0].shape[0]
            cos.append(np.ones((S, blk)))
            sin.append(np.zeros((S, blk)))
            continue
        c, s = np.cos(blk), np.sin(blk)
        cos += [c, c]
        sin += [-s, s]
    return tuple(jnp.asarray(np.concatenate(t, axis=-1), dtype=F32) for t in (cos, sin))


def _partner_permutation(rotary_dim):
    half = rotary_dim // 2
    p = np.zeros((HEAD_DIM, HEAD_DIM))
    for i in range(rotary_dim):
        p[i + half if i < half else i - half, i] = 1.0
    return jnp.asarray(p, dtype=BF16)


def _axial_tables(S):
    pos = np.arange(S)
    half = HEAD_DIM // 2
    return _lane_tables([_angles(pos // GRID_W, half, AXIAL_THETA),
                         _angles(pos % GRID_W, half, AXIAL_THETA)])


def _partial_tables(S):
    pos = np.arange(S)
    return _lane_tables([_angles(pos, PARTIAL_ROPE_DIM, ROPE_THETA),
                         HEAD_DIM - PARTIAL_ROPE_DIM])


def kernel(x, norm_gain, w_in, q_norm_gain, k_norm_gain, merge_gate_bias, w_branch_a,
           w_branch_b, w_out, final_norm_gain):
    B, S, D = x.shape
    depth = norm_gain.shape[0]
    n_cols = _OFF_ZA + 2 * D
    assert w_in.shape[2] == n_cols and S % GRID_W == 0
    assert all(w // (2 * d) == BAND_HALF for w, d in B_PATTERNS)
    axial = _axial_tables(S)
    partial = _partial_tables(S)
    x2 = x.reshape(B * S, D)
    for l in range(depth):
        proj2 = _input_projection(x2, norm_gain[l][None, :], w_in[l])
        proj3 = proj2.reshape(B, S, n_cols)
        ya = _attention_a(proj3, q_norm_gain[l][None, :], k_norm_gain[l][None, :], axial)
        yb = _band_mix(proj3, partial)
        x2 = _merge(x2, ya.reshape(B * S, A_WIDTH), yb.reshape(B * S, B_WIDTH), proj2,
                    merge_gate_bias[l], w_branch_a[l].astype(BF16),
                    w_branch_b[l].astype(BF16), w_out[l].astype(BF16),
                    final_norm_gain[None, :], final_norm=(l == depth - 1))
    return x2.reshape(B, S, D)
```

```python
import functools
import math

import numpy as np
import jax
import jax.numpy as jnp
from jax import lax
from jax.experimental import pallas as pl
from jax.experimental.pallas import tpu as pltpu

F32 = jnp.float32
BF16 = jnp.bfloat16

HEAD_DIM = 128
GRID_W = 64
NORM_EPS = 1e-6
A_Q_HEADS = 8
A_KV_HEADS = 2
A_GROUP = A_Q_HEADS // A_KV_HEADS
A_WIDTH = A_Q_HEADS * HEAD_DIM
A_KV_WIDTH = A_KV_HEADS * HEAD_DIM
AXIAL_THETA = 10000.0
AXIAL_SHIFT = HEAD_DIM // 4
MAX_SAFE_SCORE_BOUND = 50.0
SCORE_BOUND_SLACK = 1.01
B_PATTERNS = ((128, 1), (512, 4), (2048, 16))
B_GROUPS = len(B_PATTERNS)
B_HEADS = 4
B_WIDTH = B_HEADS * HEAD_DIM
PARTIAL_ROPE_DIM = HEAD_DIM // 4
PARTIAL_SHIFT = PARTIAL_ROPE_DIM // 2
ROPE_THETA = 500000.0
BAND_HALF = 64
BAND_TQ = 128
BAND_WIN = BAND_TQ + 2 * BAND_HALF
MIX_ROWS = 256
SPLIT = 4
MASK_VALUE = -1e30
LOG2_E = math.log2(math.e)
Q_SCALE = HEAD_DIM ** -0.5 * LOG2_E

COL_BLK = 512
_OFF_QA = 0
_OFF_KA = A_WIDTH
_OFF_VA = _OFF_KA + A_KV_WIDTH
_OFF_GA = _OFF_VA + A_KV_WIDTH
_OFF_QB = _OFF_GA + A_WIDTH
_OFF_KB = _OFF_QB + B_GROUPS * B_WIDTH
_OFF_VB = _OFF_KB + B_GROUPS * B_WIDTH
_OFF_GB = _OFF_VB + B_GROUPS * B_WIDTH
_OFF_ZA = _OFF_GB + B_WIDTH

V7X_VMEM_LIMIT_BYTES = 56 * 1024 * 1024
PROJ_ROW_TILE = 1024
PROJ_COL_TILE = 2 * COL_BLK
PROJ_ROW_TILES_PER_WEIGHT_TILE = 2
ATTN_Q_TILE = 512
MERGE_ROW_TILE = 512
WEIGHT_CHUNK_ROWS = 128


def _rotary(x, cos, sin, shift):
    lane = lax.broadcasted_iota(jnp.int32, x.shape, 1)
    up = pltpu.roll(x, HEAD_DIM - shift, 1)
    dn = pltpu.roll(x, shift, 1)
    partner = jnp.where(lane % (2 * shift) < shift, up, dn)
    return x * cos + partner * sin


def _rms(x, gain):
    ms = jnp.mean(x * x, axis=-1, keepdims=True)
    return x * lax.rsqrt(ms + NORM_EPS) * gain


def _silu(g):
    return g * jax.nn.sigmoid(g)


def _proj_kernel(x_ref, g_ref, w_ref, o_ref, h_ref, wb_ref, *, n_full, tail):
    j, r = pl.program_id(1), pl.program_id(2)

    def project(width, first_tile):
        cols = slice(0, width)

        def normed_rows():
            if not first_tile:
                return h_ref[r]
            h = _rms(x_ref[...], g_ref[...]).astype(BF16)
            h_ref[r] = h
            return h

        @pl.when(r == 0)
        def _():
            w = w_ref[:, cols].astype(BF16)
            wb_ref[:, cols] = w
            o_ref[:, cols] = jnp.dot(normed_rows(), w, preferred_element_type=F32).astype(BF16)

        @pl.when(r != 0)
        def _():
            o_ref[:, cols] = jnp.dot(normed_rows(), wb_ref[:, cols],
                                     preferred_element_type=F32).astype(BF16)

    assert n_full >= 1

    @pl.when(j == 0)
    def _():
        project(w_ref.shape[1], True)

    @pl.when((j > 0) & (j < n_full))
    def _():
        project(w_ref.shape[1], False)

    if tail:
        @pl.when(j == n_full)
        def _():
            project(tail, False)


def _input_projection(x2, gain, w):
    T, D = x2.shape
    N = w.shape[1]
    tm, tn, R = PROJ_ROW_TILE, PROJ_COL_TILE, PROJ_ROW_TILES_PER_WEIGHT_TILE
    n_full, tail = divmod(N, tn)

    def x_index(g, j, r):
        return g * R + jnp.where(j == 0, r, R - 1), 0

    return pl.pallas_call(
        functools.partial(_proj_kernel, n_full=n_full, tail=tail),
        grid=(T // (tm * R), pl.cdiv(N, tn), R),
        in_specs=[
            pl.BlockSpec((tm, D), x_index),
            pl.BlockSpec((1, D), lambda g, j, r: (0, 0)),
            pl.BlockSpec((D, tn), lambda g, j, r: (0, j)),
        ],
        out_specs=pl.BlockSpec((tm, tn), lambda g, j, r: (g * R + r, j)),
        out_shape=jax.ShapeDtypeStruct((T, N), BF16),
        scratch_shapes=[pltpu.VMEM((R, tm, D), BF16), pltpu.VMEM((D, tn), BF16)],
        compiler_params=pltpu.CompilerParams(
            dimension_semantics=("parallel", "arbitrary", "arbitrary"),
            vmem_limit_bytes=V7X_VMEM_LIMIT_BYTES),
        name="input_projection",
    )(x2, gain, w)


def _attn_a_kernel(q_ref, qnext_ref, k_ref, v_ref, ga_ref, gq_ref, gk_ref, cos_ref, sin_ref,
                   o_ref, kn_ref, v1_ref, k2max_ref, qs_ref, q2max_ref, *, tq, n_q):
    qi = pl.program_id(2)
    slot = qi % 2

    def max_sq_norm(x_bf16):
        xf = x_bf16.astype(F32)
        return jnp.max(jnp.sum(xf * xf, axis=-1, keepdims=True), axis=0, keepdims=True)

    def prepare_queries(src_ref, tile, dst):
        rows = pl.ds(pl.multiple_of(tile * tq, tq), tq)
        cos, sin = cos_ref[rows, :], sin_ref[rows, :]
        q2max = None
        for h in range(A_GROUP):
            cols = slice(h * HEAD_DIM, (h + 1) * HEAD_DIM)
            q = _rms(src_ref[0, :, cols].astype(F32), gq_ref[...])
            q = (_rotary(q, cos, sin, AXIAL_SHIFT) * Q_SCALE).astype(BF16)
            qs_ref[dst, :, cols] = q
            q2max = max_sq_norm(q) if q2max is None else jnp.maximum(q2max, max_sq_norm(q))
        q2max_ref[dst] = jnp.broadcast_to(q2max, q2max_ref.shape[1:])

    @pl.when(qi == 0)
    def _():
        k = _rms(k_ref[0].astype(F32), gk_ref[...])
        k = _rotary(k, cos_ref[...], sin_ref[...], AXIAL_SHIFT).astype(BF16)
        kn_ref[...] = k
        k2max_ref[...] = jnp.broadcast_to(max_sq_norm(k), k2max_ref.shape)
        v1_ref[:, 0:HEAD_DIM] = v_ref[0]
        v1_ref[:, HEAD_DIM:] = jnp.ones((v1_ref.shape[0], HEAD_DIM), BF16)

    @pl.when((pl.program_id(0) == 0) & (pl.program_id(1) == 0) & (qi == 0))
    def _():
        prepare_queries(q_ref, 0, 0)

    bound = jnp.sqrt(q2max_ref[slot, 0:1, 0:1] * k2max_ref[0:1, 0:1]) * SCORE_BOUND_SLACK
    bound_is_safe = bound[0, 0] <= MAX_SAFE_SCORE_BOUND

    def attend(shift_of):
        prepare_queries(qnext_ref, (qi + 1) % n_q, 1 - slot)
        kn = kn_ref[...]
        v1 = v1_ref[...]
        for h in range(A_GROUP):
            cols = slice(h * HEAD_DIM, (h + 1) * HEAD_DIM)
            s = lax.dot_general(qs_ref[slot, :, cols], kn, (((1,), (1,)), ((), ())),
                                preferred_element_type=F32)
            p = jnp.exp2(s - shift_of(s))
            pv = jnp.dot(p.astype(BF16), v1, preferred_element_type=F32)
            o, l = pv[:, :HEAD_DIM], pv[:, HEAD_DIM:]
            g = ga_ref[0, :, cols].astype(F32)
            o_ref[0, :, cols] = (o * (1.0 / l) * _silu(g)).astype(BF16)

    @pl.when(bound_is_safe)
    def _():
        attend(lambda s: bound)

    @pl.when(jnp.logical_not(bound_is_safe))
    def _():
        attend(lambda s: jnp.max(s, axis=-1, keepdims=True))


def _attention_a(proj3, q_gain, k_gain, tables):
    B, S, _ = proj3.shape
    tq = ATTN_Q_TILE
    n_q = S // tq
    gw = A_GROUP * HEAD_DIM
    ka_blk = _OFF_KA // HEAD_DIM
    va_blk = _OFF_VA // HEAD_DIM
    ga_blk = _OFF_GA // gw
    full = pl.BlockSpec((S, HEAD_DIM), lambda b, g, i: (0, 0))
    vec = pl.BlockSpec((1, HEAD_DIM), lambda b, g, i: (0, 0))
    assert n_q % 2 == 0

    def next_step_q(b, g, i):
        n = jnp.minimum((b * A_KV_HEADS + g) * n_q + i + 1, B * A_KV_HEADS * n_q - 1)
        return n // (A_KV_HEADS * n_q), n % n_q, (n // n_q) % A_KV_HEADS

    return pl.pallas_call(
        functools.partial(_attn_a_kernel, tq=tq, n_q=n_q),
        grid=(B, A_KV_HEADS, n_q),
        in_specs=[
            pl.BlockSpec((1, tq, gw), lambda b, g, i: (b, i, g)),
            pl.BlockSpec((1, tq, gw), next_step_q),
            pl.BlockSpec((1, S, HEAD_DIM), lambda b, g, i: (b, 0, ka_blk + g)),
            pl.BlockSpec((1, S, HEAD_DIM), lambda b, g, i: (b, 0, va_blk + g)),
            pl.BlockSpec((1, tq, gw), lambda b, g, i: (b, i, ga_blk + g)),
            vec, vec, full, full,
        ],
        out_specs=pl.BlockSpec((1, tq, gw), lambda b, g, i: (b, i, g)),
        out_shape=jax.ShapeDtypeStruct((B, S, A_WIDTH), BF16),
        scratch_shapes=[
            pltpu.VMEM((S, HEAD_DIM), BF16),
            pltpu.VMEM((S, 2 * HEAD_DIM), BF16),
            pltpu.VMEM((8, HEAD_DIM), F32),
            pltpu.VMEM((2, tq, gw), BF16),
            pltpu.VMEM((2, 8, HEAD_DIM), F32),
        ],
        compiler_params=pltpu.CompilerParams(
            dimension_semantics=("arbitrary", "arbitrary", "arbitrary"),
            vmem_limit_bytes=V7X_VMEM_LIMIT_BYTES),
        name="attention_a",
    )(proj3, proj3, proj3, proj3, proj3, q_gain, k_gain, *tables)


def _band_tile(q, kw, vw, valid):
    s = lax.dot_general(q, kw, (((1,), (1,)), ((), ())), preferred_element_type=F32)
    s = jnp.where(valid, s, MASK_VALUE)
    m = jnp.max(s, axis=-1, keepdims=True)
    p = jnp.exp2(s - m)
    pv = jnp.dot(p.astype(BF16), vw, preferred_element_type=F32)
    o, l = pv[:, :HEAD_DIM], pv[:, HEAD_DIM:]
    return o * (1.0 / l), m + jnp.log2(l)


def _window_mask(t, n_tiles):
    qi = lax.broadcasted_iota(jnp.int32, (BAND_TQ, BAND_WIN), 0)
    kj = lax.broadcasted_iota(jnp.int32, (BAND_TQ, BAND_WIN), 1)
    kpos = kj + (t * BAND_TQ - BAND_HALF)
    return (kj >= qi) & (kj <= qi + 2 * BAND_HALF) & (kpos >= 0) & (kpos < n_tiles * BAND_TQ)


def _zero_pads(ref, lead, n_rows):
    zeros = jnp.zeros((BAND_HALF, HEAD_DIM), BF16)
    lanes = slice(0, HEAD_DIM)
    for idx in lead:
        ref[idx + (slice(0, BAND_HALF), lanes)] = zeros
        ref[idx + (slice(BAND_HALF + n_rows, 2 * BAND_HALF + n_rows), lanes)] = zeros


def _band_mix_kernel(q1_ref, k1_ref, v1_ref, q4_ref, k4_ref, v4_ref, q16_ref, k16_ref, v16_ref,
                     gb_ref, cos_ref, sin_ref, perm_ref, y_ref,
                     nat_ref, tmp_ref, qd1_ref, kp1_ref, vp1_ref, qd4_ref, kp4_ref, vp4_ref,
                     qd16_ref, kd16_ref, vd16_ref, od16_ref, og_ref, lg_ref, *, S):
    L4 = S // SPLIT
    L16 = L4 // SPLIT
    n1, n4 = S // BAND_TQ, L4 // BAND_TQ
    per_tile = BAND_TQ // L16

    def rotated(src_ref, scale=None):
        x = src_ref[0]
        partner = jnp.dot(x, perm_ref[...], preferred_element_type=F32)
        y = x.astype(F32) * cos_ref[...] + partner * sin_ref[...]
        return y if scale is None else y * scale

    lanes = slice(0, HEAD_DIM)
    vp1_ref[:, HEAD_DIM:] = jnp.ones((vp1_ref.shape[0], HEAD_DIM), BF16)
    vp4_ref[:, :, HEAD_DIM:] = jnp.ones(vp4_ref.shape[:2] + (HEAD_DIM,), BF16)
    vd16_ref[:, :, HEAD_DIM:] = jnp.ones(vd16_ref.shape[:2] + (HEAD_DIM,), BF16)

    _zero_pads(kp1_ref, [()], S)
    _zero_pads(vp1_ref, [()], S)
    qd1_ref[...] = rotated(q1_ref, Q_SCALE).astype(BF16)
    kp1_ref[BAND_HALF:BAND_HALF + S, :] = rotated(k1_ref).astype(BF16)
    vp1_ref[BAND_HALF:BAND_HALF + S, lanes] = v1_ref[0]

    for t in range(n1):
        rows = slice(t * BAND_TQ, (t + 1) * BAND_TQ)
        win = slice(t * BAND_TQ, t * BAND_TQ + BAND_WIN)
        o, lse = _band_tile(qd1_ref[rows, :], kp1_ref[win, :], vp1_ref[win, :],
                            _window_mask(t, n1))
        og_ref[0, rows, :] = o
        lg_ref[0, rows, :] = lse

    _zero_pads(kp4_ref, [(r,) for r in range(SPLIT)], L4)
    _zero_pads(vp4_ref, [(r,) for r in range(SPLIT)], L4)
    for n, (val, dst_ref, lead) in enumerate(((rotated(q4_ref, Q_SCALE), qd4_ref, 0),
                                              (rotated(k4_ref), kp4_ref, BAND_HALF),
                                              (v4_ref[0].astype(F32), vp4_ref, BAND_HALF))):
        nat_ref[n] = val
        for r in range(SPLIT):
            dst_ref[r, lead:lead + L4, lanes] = \
                nat_ref[n, pl.ds(r, L4, stride=SPLIT), :].astype(BF16)

    for r in range(SPLIT):
        for t in range(n4):
            rows = pl.ds(r + t * (BAND_TQ * SPLIT), BAND_TQ, stride=SPLIT)
            win = slice(t * BAND_TQ, t * BAND_TQ + BAND_WIN)
            o, lse = _band_tile(qd4_ref[r, t * BAND_TQ:(t + 1) * BAND_TQ, :],
                                kp4_ref[r, win, :], vp4_ref[r, win, :], _window_mask(t, n4))
            og_ref[1, rows, :] = o
            lg_ref[1, rows, :] = lse

    for n, (val, dst_ref) in enumerate(((rotated(q16_ref, Q_SCALE), qd16_ref),
                                        (rotated(k16_ref), kd16_ref),
                                        (v16_ref[0].astype(F32), vd16_ref))):
        nat_ref[3 + n] = val
        for a in range(SPLIT):
            tmp_ref[n, a] = nat_ref[3 + n, pl.ds(a, L4, stride=SPLIT), :]
            for b in range(SPLIT):
                r = a + SPLIT * b
                dst_ref[r // per_tile, (r % per_tile) * L16:(r % per_tile + 1) * L16, lanes] = \
                    tmp_ref[n, a, pl.ds(b, L16, stride=SPLIT), :].astype(BF16)

    qi = lax.broadcasted_iota(jnp.int32, (BAND_TQ, BAND_TQ), 0)
    kj = lax.broadcasted_iota(jnp.int32, (BAND_TQ, BAND_TQ), 1)
    same_residue = functools.reduce(
        jnp.logical_or, [(qi >= c * L16) & (qi < (c + 1) * L16) & (kj >= c * L16)
                         & (kj < (c + 1) * L16) for c in range(per_tile)])
    stacked_band = same_residue & (kj >= qi - BAND_HALF) & (kj <= qi + BAND_HALF)
    for j in range(SPLIT * SPLIT // per_tile):
        o, lse = _band_tile(qd16_ref[j], kd16_ref[j], vd16_ref[j], stacked_band)
        od16_ref[0, j] = o
        od16_ref[1, j] = lse

    for n, dst_ref in enumerate((og_ref, lg_ref)):
        for a in range(SPLIT):
            for b in range(SPLIT):
                r = a + SPLIT * b
                tmp_ref[n, a, pl.ds(b, L16, stride=SPLIT), :] = \
                    od16_ref[n, r // per_tile, (r % per_tile) * L16:(r % per_tile + 1) * L16, :]
            dst_ref[2, pl.ds(a, L4, stride=SPLIT), :] = tmp_ref[n, a]

    chunk = MIX_ROWS

    def mix(i, carry):
        rows = pl.ds(pl.multiple_of(i * chunk, chunk), chunk)
        lses = [lg_ref[g, rows, :] for g in range(B_GROUPS)]
        lmax = functools.reduce(jnp.maximum, lses)
        es = [jnp.exp2(l - lmax) for l in lses]
        num = sum(e * og_ref[g, rows, :] for g, e in enumerate(es))
        ob = num * (1.0 / sum(es))
        y_ref[0, rows, :] = (ob * _silu(gb_ref[0, rows, :].astype(F32))).astype(BF16)
        return carry

    lax.fori_loop(0, S // chunk, mix, 0)


def _band_mix(proj3, tables):
    B, S, NC = proj3.shape
    assert tuple(d for _, d in B_PATTERNS) == (1, SPLIT, SPLIT * SPLIT)
    L4, L16 = S // SPLIT, S // (SPLIT * SPLIT)
    assert L4 % BAND_TQ == 0 and BAND_TQ % L16 == 0
    n16 = S // BAND_TQ

    def head_blk(off, g):
        first = off // HEAD_DIM + g * B_HEADS
        return pl.BlockSpec((1, S, HEAD_DIM), lambda b, h: (b, 0, first + h))

    qkv_specs = [head_blk(off, g) for g in range(B_GROUPS)
                 for off in (_OFF_QB, _OFF_KB, _OFF_VB)]
    gb_first = _OFF_GB // HEAD_DIM
    table = pl.BlockSpec((S, HEAD_DIM), lambda b, h: (0, 0))
    return pl.pallas_call(
        functools.partial(_band_mix_kernel, S=S),
        grid=(B, B_HEADS),
        in_specs=qkv_specs
        + [pl.BlockSpec((1, S, HEAD_DIM), lambda b, h: (b, 0, gb_first + h)), table, table,
           pl.BlockSpec((HEAD_DIM, HEAD_DIM), lambda b, h: (0, 0))],
        out_specs=pl.BlockSpec((1, S, HEAD_DIM), lambda b, h: (b, 0, h)),
        out_shape=jax.ShapeDtypeStruct((B, S, B_WIDTH), BF16),
        scratch_shapes=[
            pltpu.VMEM((6, S, HEAD_DIM), F32),
            pltpu.VMEM((3, SPLIT, L4, HEAD_DIM), F32),
            pltpu.VMEM((S, HEAD_DIM), BF16),
            pltpu.VMEM((S + 2 * BAND_HALF, HEAD_DIM), BF16),
            pltpu.VMEM((S + 2 * BAND_HALF, 2 * HEAD_DIM), BF16),
            pltpu.VMEM((SPLIT, L4, HEAD_DIM), BF16),
            pltpu.VMEM((SPLIT, L4 + 2 * BAND_HALF, HEAD_DIM), BF16),
            pltpu.VMEM((SPLIT, L4 + 2 * BAND_HALF, 2 * HEAD_DIM), BF16),
            pltpu.VMEM((n16, BAND_TQ, HEAD_DIM), BF16),
            pltpu.VMEM((n16, BAND_TQ, HEAD_DIM), BF16),
            pltpu.VMEM((n16, BAND_TQ, 2 * HEAD_DIM), BF16),
            pltpu.VMEM((2, n16, BAND_TQ, HEAD_DIM), F32),
            pltpu.VMEM((B_GROUPS, S, HEAD_DIM), F32),
            pltpu.VMEM((B_GROUPS, S, HEAD_DIM), F32),
        ],
        compiler_params=pltpu.CompilerParams(
            dimension_semantics=("parallel", "parallel"),
            vmem_limit_bytes=V7X_VMEM_LIMIT_BYTES),
        name="band_mix",
    )(*([proj3] * (3 * B_GROUPS + 1)), *tables, _partner_permutation(PARTIAL_ROPE_DIM))


def _load_weights_as_bf16(hbm_refs, dst_refs, stage_ref, sem):
    chunks = [(src, dst, r0) for src, dst in zip(hbm_refs, dst_refs)
              for r0 in range(0, src.shape[0], WEIGHT_CHUNK_ROWS)]

    def chunk_copy(c):
        src, _, r0 = chunks[c]
        return pltpu.make_async_copy(src.at[pl.ds(r0, WEIGHT_CHUNK_ROWS), :],
                                     stage_ref.at[c % 2], sem.at[c % 2])

    chunk_copy(0).start()
    for c, (_, dst, r0) in enumerate(chunks):
        if c + 1 < len(chunks):
            chunk_copy(c + 1).start()
        chunk_copy(c).wait()
        dst[r0:r0 + WEIGHT_CHUNK_ROWS, :] = stage_ref[c % 2].astype(BF16)


def _merge_kernel(x_ref, ya_ref, yb_ref, za0, za1, za2, za3, zb0, zb1, zb2, zb3, bias_ref,
                  wa_hbm, wb_hbm, wo_hbm, fg_ref, out_ref, merged_ref, wa_ref, wb_ref, wo_ref,
                  stage_ref, sem, *, final_norm):
    @pl.when(pl.program_id(0) == 0)
    def _():
        _load_weights_as_bf16((wa_hbm, wb_hbm, wo_hbm), (wa_ref, wb_ref, wo_ref), stage_ref, sem)

    pa = jnp.dot(ya_ref[...], wa_ref[...], preferred_element_type=F32)
    pb = jnp.dot(yb_ref[...], wb_ref[...], preferred_element_type=F32)
    for c, (za, zb) in enumerate(((za0, zb0), (za1, zb1), (za2, zb2), (za3, zb3))):
        cols = slice(c * COL_BLK, (c + 1) * COL_BLK)
        gate_a = jax.nn.sigmoid(za[...].astype(F32) + bias_ref[0:1, cols])
        gate_b = jax.nn.sigmoid(zb[...].astype(F32) + bias_ref[1:2, cols])
        merged_ref[:, cols] = (gate_a * pa[:, cols] + gate_b * pb[:, cols]).astype(BF16)

    y = x_ref[...] + jnp.dot(merged_ref[...], wo_ref[...], preferred_element_type=F32)
    out_ref[...] = _rms(y, fg_ref[...]) if final_norm else y


def _merge(x2, ya, yb, proj2, bias, wa, wb, wo, final_gain, final_norm):
    T, D = x2.shape
    tm = MERGE_ROW_TILE
    za_blk = _OFF_ZA // COL_BLK
    n_z = D // COL_BLK

    def row_blk(width, col=0):
        return pl.BlockSpec((tm, width), lambda i: (i, col))

    def whole(a):
        return pl.BlockSpec(a.shape, lambda i: (0, 0))

    in_hbm = pl.BlockSpec(memory_space=pl.ANY)
    assert all(w.shape[0] % WEIGHT_CHUNK_ROWS == 0 and w.shape[1] == D for w in (wa, wb, wo))
    z_specs = [row_blk(COL_BLK, za_blk + c) for c in range(2 * n_z)]
    return pl.pallas_call(
        functools.partial(_merge_kernel, final_norm=final_norm),
        grid=(T // tm,),
        in_specs=[row_blk(D), row_blk(A_WIDTH), row_blk(B_WIDTH)] + z_specs
        + [whole(bias), in_hbm, in_hbm, in_hbm, whole(final_gain)],
        out_specs=row_blk(D),
        out_shape=jax.ShapeDtypeStruct((T, D), F32),
        scratch_shapes=[
            pltpu.VMEM((tm, D), BF16),
            pltpu.VMEM(wa.shape, BF16), pltpu.VMEM(wb.shape, BF16), pltpu.VMEM(wo.shape, BF16),
            pltpu.VMEM((2, WEIGHT_CHUNK_ROWS, D), F32),
            pltpu.SemaphoreType.DMA((2,)),
        ],
        compiler_params=pltpu.CompilerParams(
            dimension_semantics=("arbitrary",),
            vmem_limit_bytes=V7X_VMEM_LIMIT_BYTES),
        name="merge_output",
    )(x2, ya, yb, *([proj2] * (2 * n_z)), bias, wa, wb, wo, final_gain)


def _angles(pos, dim, theta):
    expo = np.arange(0, dim, 2, dtype=np.float64) / dim
    return pos.astype(np.float64)[:, None] / np.power(float(theta), expo)[None, :]


def _lane_tables(ang_blocks):
    cos, sin = [], []
    for blk in ang_blocks:
        if isinstance(blk, int):
            S = cos[0].shape[0]
            cos.append(np.ones((S, blk)))
            sin.append(np.zeros((S, blk)))
            continue
        c, s = np.cos(blk), np.sin(blk)
        cos += [c, c]
        sin += [-s, s]
    return tuple(jnp.asarray(np.concatenate(t, axis=-1), dtype=F32) for t in (cos, sin))


def _partner_permutation(rotary_dim):
    half = rotary_dim // 2
    p = np.zeros((HEAD_DIM, HEAD_DIM))
    for i in range(rotary_dim):
        p[i + half if i < half else i - half, i] = 1.0
    return jnp.asarray(p, dtype=BF16)


def _axial_tables(S):
    pos = np.arange(S)
    half = HEAD_DIM // 2
    return _lane_tables([_angles(pos // GRID_W, half, AXIAL_THETA),
                         _angles(pos % GRID_W, half, AXIAL_THETA)])


def _partial_tables(S):
    pos = np.arange(S)
    return _lane_tables([_angles(pos, PARTIAL_ROPE_DIM, ROPE_THETA),
                         HEAD_DIM - PARTIAL_ROPE_DIM])


def kernel(x, norm_gain, w_in, q_norm_gain, k_norm_gain, merge_gate_bias, w_branch_a,
           w_branch_b, w_out, final_norm_gain):
    B, S, D = x.shape
    depth = norm_gain.shape[0]
    n_cols = _OFF_ZA + 2 * D
    assert w_in.shape[2] == n_cols and S % GRID_W == 0
    assert all(w // (2 * d) == BAND_HALF for w, d in B_PATTERNS)
    axial = _axial_tables(S)
    partial = _partial_tables(S)
    x2 = x.reshape(B * S, D)
    for l in range(depth):
        proj2 = _input_projection(x2, norm_gain[l][None, :], w_in[l])
        proj3 = proj2.reshape(B, S, n_cols)
        ya = _attention_a(proj3, q_norm_gain[l][None, :], k_norm_gain[l][None, :], axial)
        yb = _band_mix(proj3, partial)
        x2 = _merge(x2, ya.reshape(B * S, A_WIDTH), yb.reshape(B * S, B_WIDTH), proj2,
                    merge_gate_bias[l], w_branch_a[l], w_branch_b[l], w_out[l],
                    final_norm_gain[None, :], final_norm=(l == depth - 1))
    return x2.reshape(B, S, D)
```

```python
import functools
import math

import numpy as np
import jax
import jax.numpy as jnp
from jax import lax
from jax.experimental import pallas as pl
from jax.experimental.pallas import tpu as pltpu

F32 = jnp.float32
BF16 = jnp.bfloat16

HEAD_DIM = 128
GRID_W = 64
NORM_EPS = 1e-6
A_Q_HEADS = 8
A_KV_HEADS = 2
A_GROUP = A_Q_HEADS // A_KV_HEADS
A_WIDTH = A_Q_HEADS * HEAD_DIM
A_KV_WIDTH = A_KV_HEADS * HEAD_DIM
AXIAL_THETA = 10000.0
AXIAL_SHIFT = HEAD_DIM // 4
MAX_SAFE_SCORE_BOUND = 50.0
SCORE_BOUND_SLACK = 1.01
B_PATTERNS = ((128, 1), (512, 4), (2048, 16))
B_GROUPS = len(B_PATTERNS)
B_HEADS = 4
B_WIDTH = B_HEADS * HEAD_DIM
PARTIAL_ROPE_DIM = HEAD_DIM // 4
PARTIAL_SHIFT = PARTIAL_ROPE_DIM // 2
ROPE_THETA = 500000.0
BAND_HALF = 64
BAND_TQ = 128
BAND_WIN = BAND_TQ + 2 * BAND_HALF
MIX_ROWS = 256
SPLIT = 4
MASK_VALUE = -1e30
LOG2_E = math.log2(math.e)
Q_SCALE = HEAD_DIM ** -0.5 * LOG2_E

COL_BLK = 512
_OFF_QA = 0
_OFF_KA = A_WIDTH
_OFF_VA = _OFF_KA + A_KV_WIDTH
_OFF_GA = _OFF_VA + A_KV_WIDTH
_OFF_QB = _OFF_GA + A_WIDTH
_OFF_KB = _OFF_QB + B_GROUPS * B_WIDTH
_OFF_VB = _OFF_KB + B_GROUPS * B_WIDTH
_OFF_GB = _OFF_VB + B_GROUPS * B_WIDTH
_OFF_ZA = _OFF_GB + B_WIDTH

V7X_VMEM_LIMIT_BYTES = 56 * 1024 * 1024
PROJ_ROW_TILE = 1024
PROJ_COL_TILE = 2 * COL_BLK
PROJ_ROW_TILES_PER_WEIGHT_TILE = 2
ATTN_Q_TILE = 512
MERGE_ROW_TILE = 512


def _rotary(x, cos, sin, shift):
    lane = lax.broadcasted_iota(jnp.int32, x.shape, 1)
    up = pltpu.roll(x, HEAD_DIM - shift, 1)
    dn = pltpu.roll(x, shift, 1)
    partner = jnp.where(lane % (2 * shift) < shift, up, dn)
    return x * cos + partner * sin


def _rms(x, gain):
    ms = jnp.mean(x * x, axis=-1, keepdims=True)
    return x * lax.rsqrt(ms + NORM_EPS) * gain


def _silu(g):
    return g * jax.nn.sigmoid(g)


def _proj_kernel(x_ref, g_ref, w_ref, o_ref, h_ref, wb_ref, *, n_full, tail):
    j, r = pl.program_id(1), pl.program_id(2)

    def project(width, first_tile):
        cols = slice(0, width)

        def normed_rows():
            if not first_tile:
                return h_ref[r]
            h = _rms(x_ref[...], g_ref[...]).astype(BF16)
            h_ref[r] = h
            return h

        @pl.when(r == 0)
        def _():
            w = w_ref[:, cols].astype(BF16)
            wb_ref[:, cols] = w
            o_ref[:, cols] = jnp.dot(normed_rows(), w, preferred_element_type=F32).astype(BF16)

        @pl.when(r != 0)
        def _():
            o_ref[:, cols] = jnp.dot(normed_rows(), wb_ref[:, cols],
                                     preferred_element_type=F32).astype(BF16)

    assert n_full >= 1

    @pl.when(j == 0)
    def _():
        project(w_ref.shape[1], True)

    @pl.when((j > 0) & (j < n_full))
    def _():
        project(w_ref.shape[1], False)

    if tail:
        @pl.when(j == n_full)
        def _():
            project(tail, False)


def _input_projection(x2, gain, w):
    T, D = x2.shape
    N = w.shape[1]
    tm, tn, R = PROJ_ROW_TILE, PROJ_COL_TILE, PROJ_ROW_TILES_PER_WEIGHT_TILE
    n_full, tail = divmod(N, tn)

    def x_index(g, j, r):
        return g * R + jnp.where(j == 0, r, R - 1), 0

    return pl.pallas_call(
        functools.partial(_proj_kernel, n_full=n_full, tail=tail),
        grid=(T // (tm * R), pl.cdiv(N, tn), R),
        in_specs=[
            pl.BlockSpec((tm, D), x_index),
            pl.BlockSpec((1, D), lambda g, j, r: (0, 0)),
            pl.BlockSpec((D, tn), lambda g, j, r: (0, j)),
        ],
        out_specs=pl.BlockSpec((tm, tn), lambda g, j, r: (g * R + r, j)),
        out_shape=jax.ShapeDtypeStruct((T, N), BF16),
        scratch_shapes=[pltpu.VMEM((R, tm, D), BF16), pltpu.VMEM((D, tn), BF16)],
        compiler_params=pltpu.CompilerParams(
            dimension_semantics=("parallel", "arbitrary", "arbitrary"),
            vmem_limit_bytes=V7X_VMEM_LIMIT_BYTES),
        name="input_projection",
    )(x2, gain, w)


def _attn_a_kernel(q_ref, qnext_ref, k_ref, v_ref, ga_ref, gq_ref, gk_ref, cos_ref, sin_ref,
                   o_ref, kn_ref, v1_ref, k2max_ref, qs_ref, q2max_ref, *, tq, n_q):
    qi = pl.program_id(2)
    slot = qi % 2

    def max_sq_norm(x_bf16):
        xf = x_bf16.astype(F32)
        return jnp.max(jnp.sum(xf * xf, axis=-1, keepdims=True), axis=0, keepdims=True)

    def prepare_queries(src_ref, tile, dst):
        rows = pl.ds(pl.multiple_of(tile * tq, tq), tq)
        cos, sin = cos_ref[rows, :], sin_ref[rows, :]
        q2max = None
        for h in range(A_GROUP):
            cols = slice(h * HEAD_DIM, (h + 1) * HEAD_DIM)
            q = _rms(src_ref[0, :, cols].astype(F32), gq_ref[...])
            q = (_rotary(q, cos, sin, AXIAL_SHIFT) * Q_SCALE).astype(BF16)
            qs_ref[dst, :, cols] = q
            q2max = max_sq_norm(q) if q2max is None else jnp.maximum(q2max, max_sq_norm(q))
        q2max_ref[dst] = jnp.broadcast_to(q2max, q2max_ref.shape[1:])

    @pl.when(qi == 0)
    def _():
        k = _rms(k_ref[0].astype(F32), gk_ref[...])
        k = _rotary(k, cos_ref[...], sin_ref[...], AXIAL_SHIFT).astype(BF16)
        kn_ref[...] = k
        k2max_ref[...] = jnp.broadcast_to(max_sq_norm(k), k2max_ref.shape)
        v1_ref[:, 0:HEAD_DIM] = v_ref[0]
        v1_ref[:, HEAD_DIM:] = jnp.ones((v1_ref.shape[0], HEAD_DIM), BF16)

    @pl.when((pl.program_id(0) == 0) & (pl.program_id(1) == 0) & (qi == 0))
    def _():
        prepare_queries(q_ref, 0, 0)

    bound = jnp.sqrt(q2max_ref[slot, 0:1, 0:1] * k2max_ref[0:1, 0:1]) * SCORE_BOUND_SLACK
    bound_is_safe = bound[0, 0] <= MAX_SAFE_SCORE_BOUND

    def attend(shift_of):
        prepare_queries(qnext_ref, (qi + 1) % n_q, 1 - slot)
        kn = kn_ref[...]
        v1 = v1_ref[...]
        for h in range(A_GROUP):
            cols = slice(h * HEAD_DIM, (h + 1) * HEAD_DIM)
            s = lax.dot_general(qs_ref[slot, :, cols], kn, (((1,), (1,)), ((), ())),
                                preferred_element_type=F32)
            p = jnp.exp2(s - shift_of(s))
            pv = jnp.dot(p.astype(BF16), v1, preferred_element_type=F32)
            o, l = pv[:, :HEAD_DIM], pv[:, HEAD_DIM:]
            g = ga_ref[0, :, cols].astype(F32)
            o_ref[0, :, cols] = (o * (1.0 / l) * _silu(g)).astype(BF16)

    @pl.when(bound_is_safe)
    def _():
        attend(lambda s: bound)

    @pl.when(jnp.logical_not(bound_is_safe))
    def _():
        attend(lambda s: jnp.max(s, axis=-1, keepdims=True))


def _attention_a(proj3, q_gain, k_gain, tables):
    B, S, _ = proj3.shape
    tq = ATTN_Q_TILE
    n_q = S // tq
    gw = A_GROUP * HEAD_DIM
    ka_blk = _OFF_KA // HEAD_DIM
    va_blk = _OFF_VA // HEAD_DIM
    ga_blk = _OFF_GA // gw
    full = pl.BlockSpec((S, HEAD_DIM), lambda b, g, i: (0, 0))
    vec = pl.BlockSpec((1, HEAD_DIM), lambda b, g, i: (0, 0))
    assert n_q % 2 == 0

    def next_step_q(b, g, i):
        n = jnp.minimum((b * A_KV_HEADS + g) * n_q + i + 1, B * A_KV_HEADS * n_q - 1)
        return n // (A_KV_HEADS * n_q), n % n_q, (n // n_q) % A_KV_HEADS

    return pl.pallas_call(
        functools.partial(_attn_a_kernel, tq=tq, n_q=n_q),
        grid=(B, A_KV_HEADS, n_q),
        in_specs=[
            pl.BlockSpec((1, tq, gw), lambda b, g, i: (b, i, g)),
            pl.BlockSpec((1, tq, gw), next_step_q),
            pl.BlockSpec((1, S, HEAD_DIM), lambda b, g, i: (b, 0, ka_blk + g)),
            pl.BlockSpec((1, S, HEAD_DIM), lambda b, g, i: (b, 0, va_blk + g)),
            pl.BlockSpec((1, tq, gw), lambda b, g, i: (b, i, ga_blk + g)),
            vec, vec, full, full,
        ],
        out_specs=pl.BlockSpec((1, tq, gw), lambda b, g, i: (b, i, g)),
        out_shape=jax.ShapeDtypeStruct((B, S, A_WIDTH), BF16),
        scratch_shapes=[
            pltpu.VMEM((S, HEAD_DIM), BF16),
            pltpu.VMEM((S, 2 * HEAD_DIM), BF16),
            pltpu.VMEM((8, HEAD_DIM), F32),
            pltpu.VMEM((2, tq, gw), BF16),
            pltpu.VMEM((2, 8, HEAD_DIM), F32),
        ],
        compiler_params=pltpu.CompilerParams(
            dimension_semantics=("arbitrary", "arbitrary", "arbitrary"),
            vmem_limit_bytes=V7X_VMEM_LIMIT_BYTES),
        name="attention_a",
    )(proj3, proj3, proj3, proj3, proj3, q_gain, k_gain, *tables)


def _band_tile(q, kw, vw, valid):
    s = lax.dot_general(q, kw, (((1,), (1,)), ((), ())), preferred_element_type=F32)
    s = jnp.where(valid, s, MASK_VALUE)
    m = jnp.max(s, axis=-1, keepdims=True)
    p = jnp.exp2(s - m)
    pv = jnp.dot(p.astype(BF16), vw, preferred_element_type=F32)
    o, l = pv[:, :HEAD_DIM], pv[:, HEAD_DIM:]
    return o * (1.0 / l), m + jnp.log2(l)


def _window_mask(t, n_tiles):
    qi = lax.broadcasted_iota(jnp.int32, (BAND_TQ, BAND_WIN), 0)
    kj = lax.broadcasted_iota(jnp.int32, (BAND_TQ, BAND_WIN), 1)
    kpos = kj + (t * BAND_TQ - BAND_HALF)
    return (kj >= qi) & (kj <= qi + 2 * BAND_HALF) & (kpos >= 0) & (kpos < n_tiles * BAND_TQ)


def _zero_pads(ref, lead, n_rows):
    zeros = jnp.zeros((BAND_HALF, HEAD_DIM), BF16)
    lanes = slice(0, HEAD_DIM)
    for idx in lead:
        ref[idx + (slice(0, BAND_HALF), lanes)] = zeros
        ref[idx + (slice(BAND_HALF + n_rows, 2 * BAND_HALF + n_rows), lanes)] = zeros


def _band_mix_kernel(q1_ref, k1_ref, v1_ref, q4_ref, k4_ref, v4_ref, q16_ref, k16_ref, v16_ref,
                     gb_ref, cos_ref, sin_ref, perm_ref, y_ref,
                     nat_ref, tmp_ref, qd1_ref, kp1_ref, vp1_ref, qd4_ref, kp4_ref, vp4_ref,
                     qd16_ref, kd16_ref, vd16_ref, od16_ref, og_ref, lg_ref, *, S):
    L4 = S // SPLIT
    L16 = L4 // SPLIT
    n1, n4 = S // BAND_TQ, L4 // BAND_TQ
    per_tile = BAND_TQ // L16

    def rotated(src_ref, scale=None):
        x = src_ref[0]
        partner = jnp.dot(x, perm_ref[...], preferred_element_type=F32)
        y = x.astype(F32) * cos_ref[...] + partner * sin_ref[...]
        return y if scale is None else y * scale

    lanes = slice(0, HEAD_DIM)
    vp1_ref[:, HEAD_DIM:] = jnp.ones((vp1_ref.shape[0], HEAD_DIM), BF16)
    vp4_ref[:, :, HEAD_DIM:] = jnp.ones(vp4_ref.shape[:2] + (HEAD_DIM,), BF16)
    vd16_ref[:, :, HEAD_DIM:] = jnp.ones(vd16_ref.shape[:2] + (HEAD_DIM,), BF16)

    _zero_pads(kp1_ref, [()], S)
    _zero_pads(vp1_ref, [()], S)
    qd1_ref[...] = rotated(q1_ref, Q_SCALE).astype(BF16)
    kp1_ref[BAND_HALF:BAND_HALF + S, :] = rotated(k1_ref).astype(BF16)
    vp1_ref[BAND_HALF:BAND_HALF + S, lanes] = v1_ref[0]

    for t in range(n1):
        rows = slice(t * BAND_TQ, (t + 1) * BAND_TQ)
        win = slice(t * BAND_TQ, t * BAND_TQ + BAND_WIN)
        o, lse = _band_tile(qd1_ref[rows, :], kp1_ref[win, :], vp1_ref[win, :],
                            _window_mask(t, n1))
        og_ref[0, rows, :] = o
        lg_ref[0, rows, :] = lse

    _zero_pads(kp4_ref, [(r,) for r in range(SPLIT)], L4)
    _zero_pads(vp4_ref, [(r,) for r in range(SPLIT)], L4)
    for n, (val, dst_ref, lead) in enumerate(((rotated(q4_ref, Q_SCALE), qd4_ref, 0),
                                              (rotated(k4_ref), kp4_ref, BAND_HALF),
                                              (v4_ref[0].astype(F32), vp4_ref, BAND_HALF))):
        nat_ref[n] = val
        for r in range(SPLIT):
            dst_ref[r, lead:lead + L4, lanes] = \
                nat_ref[n, pl.ds(r, L4, stride=SPLIT), :].astype(BF16)

    for r in range(SPLIT):
        for t in range(n4):
            rows = pl.ds(r + t * (BAND_TQ * SPLIT), BAND_TQ, stride=SPLIT)
            win = slice(t * BAND_TQ, t * BAND_TQ + BAND_WIN)
            o, lse = _band_tile(qd4_ref[r, t * BAND_TQ:(t + 1) * BAND_TQ, :],
                                kp4_ref[r, win, :], vp4_ref[r, win, :], _window_mask(t, n4))
            og_ref[1, rows, :] = o
            lg_ref[1, rows, :] = lse

    for n, (val, dst_ref) in enumerate(((rotated(q16_ref, Q_SCALE), qd16_ref),
                                        (rotated(k16_ref), kd16_ref),
                                        (v16_ref[0].astype(F32), vd16_ref))):
        nat_ref[3 + n] = val
        for a in range(SPLIT):
            tmp_ref[n, a] = nat_ref[3 + n, pl.ds(a, L4, stride=SPLIT), :]
            for b in range(SPLIT):
                r = a + SPLIT * b
                dst_ref[r // per_tile, (r % per_tile) * L16:(r % per_tile + 1) * L16, lanes] = \
                    tmp_ref[n, a, pl.ds(b, L16, stride=SPLIT), :].astype(BF16)

    qi = lax.broadcasted_iota(jnp.int32, (BAND_TQ, BAND_TQ), 0)
    kj = lax.broadcasted_iota(jnp.int32, (BAND_TQ, BAND_TQ), 1)
    same_residue = functools.reduce(
        jnp.logical_or, [(qi >= c * L16) & (qi < (c + 1) * L16) & (kj >= c * L16)
                         & (kj < (c + 1) * L16) for c in range(per_tile)])
    stacked_band = same_residue & (kj >= qi - BAND_HALF) & (kj <= qi + BAND_HALF)
    for j in range(SPLIT * SPLIT // per_tile):
        o, lse = _band_tile(qd16_ref[j], kd16_ref[j], vd16_ref[j], stacked_band)
        od16_ref[0, j] = o
        od16_ref[1, j] = lse

    for n, dst_ref in enumerate((og_ref, lg_ref)):
        for a in range(SPLIT):
            for b in range(SPLIT):
                r = a + SPLIT * b
                tmp_ref[n, a, pl.ds(b, L16, stride=SPLIT), :] = \
                    od16_ref[n, r // per_tile, (r % per_tile) * L16:(r % per_tile + 1) * L16, :]
            dst_ref[2, pl.ds(a, L4, stride=SPLIT), :] = tmp_ref[n, a]

    chunk = MIX_ROWS

    def mix(i, carry):
        rows = pl.ds(pl.multiple_of(i * chunk, chunk), chunk)
        lses = [lg_ref[g, rows, :] for g in range(B_GROUPS)]
        lmax = functools.reduce(jnp.maximum, lses)
        es = [jnp.exp2(l - lmax) for l in lses]
        num = sum(e * og_ref[g, rows, :] for g, e in enumerate(es))
        ob = num * (1.0 / sum(es))
        y_ref[0, rows, :] = (ob * _silu(gb_ref[0, rows, :].astype(F32))).astype(BF16)
        return carry

    lax.fori_loop(0, S // chunk, mix, 0)


def _band_mix(proj3, tables):
    B, S, NC = proj3.shape
    assert tuple(d for _, d in B_PATTERNS) == (1, SPLIT, SPLIT * SPLIT)
    L4, L16 = S // SPLIT, S // (SPLIT * SPLIT)
    assert L4 % BAND_TQ == 0 and BAND_TQ % L16 == 0
    n16 = S // BAND_TQ

    def head_blk(off, g):
        first = off // HEAD_DIM + g * B_HEADS
        return pl.BlockSpec((1, S, HEAD_DIM), lambda b, h: (b, 0, first + h))

    qkv_specs = [head_blk(off, g) for g in range(B_GROUPS)
                 for off in (_OFF_QB, _OFF_KB, _OFF_VB)]
    gb_first = _OFF_GB // HEAD_DIM
    table = pl.BlockSpec((S, HEAD_DIM), lambda b, h: (0, 0))
    return pl.pallas_call(
        functools.partial(_band_mix_kernel, S=S),
        grid=(B, B_HEADS),
        in_specs=qkv_specs
        + [pl.BlockSpec((1, S, HEAD_DIM), lambda b, h: (b, 0, gb_first + h)), table, table,
           pl.BlockSpec((HEAD_DIM, HEAD_DIM), lambda b, h: (0, 0))],
        out_specs=pl.BlockSpec((1, S, HEAD_DIM), lambda b, h: (b, 0, h)),
        out_shape=jax.ShapeDtypeStruct((B, S, B_WIDTH), BF16),
        scratch_shapes=[
            pltpu.VMEM((6, S, HEAD_DIM), F32),
            pltpu.VMEM((3, SPLIT, L4, HEAD_DIM), F32),
            pltpu.VMEM((S, HEAD_DIM), BF16),
            pltpu.VMEM((S + 2 * BAND_HALF, HEAD_DIM), BF16),
            pltpu.VMEM((S + 2 * BAND_HALF, 2 * HEAD_DIM), BF16),
            pltpu.VMEM((SPLIT, L4, HEAD_DIM), BF16),
            pltpu.VMEM((SPLIT, L4 + 2 * BAND_HALF, HEAD_DIM), BF16),
            pltpu.VMEM((SPLIT, L4 + 2 * BAND_HALF, 2 * HEAD_DIM), BF16),
            pltpu.VMEM((n16, BAND_TQ, HEAD_DIM), BF16),
            pltpu.VMEM((n16, BAND_TQ, HEAD_DIM), BF16),
            pltpu.VMEM((n16, BAND_TQ, 2 * HEAD_DIM), BF16),
            pltpu.VMEM((2, n16, BAND_TQ, HEAD_DIM), F32),
            pltpu.VMEM((B_GROUPS, S, HEAD_DIM), F32),
            pltpu.VMEM((B_GROUPS, S, HEAD_DIM), F32),
        ],
        compiler_params=pltpu.CompilerParams(
            dimension_semantics=("parallel", "parallel"),
            vmem_limit_bytes=V7X_VMEM_LIMIT_BYTES),
        name="band_mix",
    )(*([proj3] * (3 * B_GROUPS + 1)), *tables, _partner_permutation(PARTIAL_ROPE_DIM))


def _merge_kernel(x_ref, ya_ref, yb_ref, za_ref, zb_ref, bias_ref,
                  wa_ref, wb_ref, wo_ref, fg_ref, out_ref, merged_ref, *, final_norm):
    pa = jnp.dot(ya_ref[...], wa_ref[...], preferred_element_type=F32)
    pb = jnp.dot(yb_ref[...], wb_ref[...], preferred_element_type=F32)
    for c in range(x_ref.shape[1] // COL_BLK):
        cols = slice(c * COL_BLK, (c + 1) * COL_BLK)
        gate_a = jax.nn.sigmoid(za_ref[:, cols].astype(F32) + bias_ref[0:1, cols])
        gate_b = jax.nn.sigmoid(zb_ref[:, cols].astype(F32) + bias_ref[1:2, cols])
        merged_ref[:, cols] = (gate_a * pa[:, cols] + gate_b * pb[:, cols]).astype(BF16)

    y = x_ref[...] + jnp.dot(merged_ref[...], wo_ref[...], preferred_element_type=F32)
    out_ref[...] = _rms(y, fg_ref[...]) if final_norm else y


def _merge(x2, ya, yb, proj2, bias, wa, wb, wo, final_gain, final_norm):
    T, D = x2.shape
    tm = MERGE_ROW_TILE
    def row_blk(width, col=0):
        return pl.BlockSpec((tm, width), lambda i: (i, col))

    def whole(a):
        return pl.BlockSpec(a.shape, lambda i: (0, 0))

    z_specs = [pl.BlockSpec((pl.Element(tm), pl.Element(D)), lambda i, off=off: (i * tm, off))
               for off in (_OFF_ZA, _OFF_ZA + D)]
    return pl.pallas_call(
        functools.partial(_merge_kernel, final_norm=final_norm),
        grid=(T // tm,),
        in_specs=[row_blk(D), row_blk(A_WIDTH), row_blk(B_WIDTH)] + z_specs
        + [whole(bias), whole(wa), whole(wb), whole(wo), whole(final_gain)],
        out_specs=row_blk(D),
        out_shape=jax.ShapeDtypeStruct((T, D), F32),
        scratch_shapes=[pltpu.VMEM((tm, D), BF16)],
        compiler_params=pltpu.CompilerParams(
            dimension_semantics=("parallel",),
            vmem_limit_bytes=V7X_VMEM_LIMIT_BYTES),
        name="merge_output",
    )(x2, ya, yb, proj2, proj2, bias, wa, wb, wo, final_gain)


def _angles(pos, dim, theta):
    expo = np.arange(0, dim, 2, dtype=np.float64) / dim
    return pos.astype(np.float64)[:, None] / np.power(float(theta), expo)[None, :]


def _lane_tables(ang_blocks):
    cos, sin = [], []
    for blk in ang_blocks:
        if isinstance(blk, int):
            S = cos[0].shape[0]
            cos.append(np.ones((S, blk)))
            sin.append(np.zeros((S, blk)))
            continue
        c, s = np.cos(blk), np.sin(blk)
        cos += [c, c]
        sin += [-s, s]
    return tuple(jnp.asarray(np.concatenate(t, axis=-1), dtype=F32) for t in (cos, sin))


def _partner_permutation(rotary_dim):
    half = rotary_dim // 2
    p = np.zeros((HEAD_DIM, HEAD_DIM))
    for i in range(rotary_dim):
        p[i + half if i < half else i - half, i] = 1.0
    return jnp.asarray(p, dtype=BF16)


def _axial_tables(S):
    pos = np.arange(S)
    half = HEAD_DIM // 2
    return _lane_tables([_angles(pos // GRID_W, half, AXIAL_THETA),
                         _angles(pos % GRID_W, half, AXIAL_THETA)])


def _partial_tables(S):
    pos = np.arange(S)
    return _lane_tables([_angles(pos, PARTIAL_ROPE_DIM, ROPE_THETA),
                         HEAD_DIM - PARTIAL_ROPE_DIM])


def kernel(x, norm_gain, w_in, q_norm_gain, k_norm_gain, merge_gate_bias, w_branch_a,
           w_branch_b, w_out, final_norm_gain):
    B, S, D = x.shape
    depth = norm_gain.shape[0]
    n_cols = _OFF_ZA + 2 * D
    assert w_in.shape[2] == n_cols and S % GRID_W == 0
    assert all(w // (2 * d) == BAND_HALF for w, d in B_PATTERNS)
    axial = _axial_tables(S)
    partial = _partial_tables(S)
    x2 = x.reshape(B * S, D)
    for l in range(depth):
        proj2 = _input_projection(x2, norm_gain[l][None, :], w_in[l])
        proj3 = proj2.reshape(B, S, n_cols)
        ya = _attention_a(proj3, q_norm_gain[l][None, :], k_norm_gain[l][None, :], axial)
        yb = _band_mix(proj3, partial)
        x2 = _merge(x2, ya.reshape(B * S, A_WIDTH), yb.reshape(B * S, B_WIDTH), proj2,
                    merge_gate_bias[l], w_branch_a[l].astype(BF16),
                    w_branch_b[l].astype(BF16), w_out[l].astype(BF16),
                    final_norm_gain[None, :], final_norm=(l == depth - 1))
    return x2.reshape(B, S, D)
```

```python
import functools
import math

import numpy as np
import jax
import jax.numpy as jnp
from jax import lax
from jax.experimental import pallas as pl
from jax.experimental.pallas import tpu as pltpu

F32 = jnp.float32
BF16 = jnp.bfloat16

HEAD_DIM = 128
GRID_W = 64
NORM_EPS = 1e-6
A_Q_HEADS = 8
A_KV_HEADS = 2
A_GROUP = A_Q_HEADS // A_KV_HEADS
A_WIDTH = A_Q_HEADS * HEAD_DIM
A_KV_WIDTH = A_KV_HEADS * HEAD_DIM
AXIAL_THETA = 10000.0
AXIAL_SHIFT = HEAD_DIM // 4
MAX_SAFE_SCORE_BOUND = 50.0
SCORE_BOUND_SLACK = 1.01
B_PATTERNS = ((128, 1), (512, 4), (2048, 16))
B_GROUPS = len(B_PATTERNS)
B_HEADS = 4
B_WIDTH = B_HEADS * HEAD_DIM
PARTIAL_ROPE_DIM = HEAD_DIM // 4
PARTIAL_SHIFT = PARTIAL_ROPE_DIM // 2
ROPE_THETA = 500000.0
BAND_HALF = 64
BAND_TQ = 128
BAND_WIN = BAND_TQ + 2 * BAND_HALF
MIX_ROWS = 256
SPLIT = 4
MASK_VALUE = -1e30
LOG2_E = math.log2(math.e)
Q_SCALE = HEAD_DIM ** -0.5 * LOG2_E

COL_BLK = 512
_OFF_QA = 0
_OFF_KA = A_WIDTH
_OFF_VA = _OFF_KA + A_KV_WIDTH
_OFF_GA = _OFF_VA + A_KV_WIDTH
_OFF_QB = _OFF_GA + A_WIDTH
_OFF_KB = _OFF_QB + B_GROUPS * B_WIDTH
_OFF_VB = _OFF_KB + B_GROUPS * B_WIDTH
_OFF_GB = _OFF_VB + B_GROUPS * B_WIDTH
_OFF_ZA = _OFF_GB + B_WIDTH

V7X_VMEM_LIMIT_BYTES = 56 * 1024 * 1024
PROJ_ROW_TILE = 512
PROJ_COL_TILE = 2 * COL_BLK
PROJ_ROW_TILES_PER_WEIGHT_TILE = 4
ATTN_Q_TILE = 512
MERGE_ROW_TILE = 512


def _rotary(x, cos, sin, shift):
    lane = lax.broadcasted_iota(jnp.int32, x.shape, 1)
    up = pltpu.roll(x, HEAD_DIM - shift, 1)
    dn = pltpu.roll(x, shift, 1)
    partner = jnp.where(lane % (2 * shift) < shift, up, dn)
    return x * cos + partner * sin


def _rms(x, gain):
    ms = jnp.mean(x * x, axis=-1, keepdims=True)
    return x * lax.rsqrt(ms + NORM_EPS) * gain


def _silu(g):
    return g * jax.nn.sigmoid(g)


def _proj_kernel(x_ref, g_ref, w_ref, o_ref, h_ref, *, n_full, tail, tm, n_rows):
    s = pl.program_id(1)
    tn = w_ref.shape[1]
    assert n_full >= 1

    for r in range(n_rows):
        @pl.when(s == r)
        def _():
            h = _rms(x_ref[...], g_ref[...]).astype(BF16)
            h_ref[r] = h
            o_ref[r * tm:(r + 1) * tm, :] = jnp.dot(
                h, w_ref[...].astype(BF16), preferred_element_type=F32).astype(BF16)

    def all_row_tiles(width):
        w = w_ref[:, 0:width].astype(BF16)
        for r in range(n_rows):
            o_ref[r * tm:(r + 1) * tm, 0:width] = jnp.dot(
                h_ref[r], w, preferred_element_type=F32).astype(BF16)

    col = s - (n_rows - 1)

    @pl.when((col >= 1) & (col < n_full))
    def _():
        all_row_tiles(tn)

    if tail:
        @pl.when(col == n_full)
        def _():
            all_row_tiles(tail)


def _input_projection(x2, gain, w):
    T, D = x2.shape
    N = w.shape[1]
    tm, tn, R = PROJ_ROW_TILE, PROJ_COL_TILE, PROJ_ROW_TILES_PER_WEIGHT_TILE
    n_full, tail = divmod(N, tn)
    n_col = pl.cdiv(N, tn)

    def col_tile(s):
        return jnp.maximum(s - (R - 1), 0)

    return pl.pallas_call(
        functools.partial(_proj_kernel, n_full=n_full, tail=tail, tm=tm, n_rows=R),
        grid=(T // (tm * R), R - 1 + n_col),
        in_specs=[
            pl.BlockSpec((tm, D), lambda g, s: (g * R + jnp.minimum(s, R - 1), 0)),
            pl.BlockSpec((1, D), lambda g, s: (0, 0)),
            pl.BlockSpec((D, tn), lambda g, s: (0, col_tile(s))),
        ],
        out_specs=pl.BlockSpec((R * tm, tn), lambda g, s: (g, col_tile(s))),
        out_shape=jax.ShapeDtypeStruct((T, N), BF16),
        scratch_shapes=[pltpu.VMEM((R, tm, D), BF16)],
        compiler_params=pltpu.CompilerParams(
            dimension_semantics=("parallel", "arbitrary"),
            vmem_limit_bytes=V7X_VMEM_LIMIT_BYTES),
        name="input_projection",
    )(x2, gain, w)


def _attn_a_kernel(q_ref, qnext_ref, k_ref, v_ref, ga_ref, gq_ref, gk_ref, cos_ref, sin_ref,
                   o_ref, kn_ref, v1_ref, k2max_ref, qs_ref, q2max_ref, *, tq, n_q):
    qi = pl.program_id(2)
    slot = qi % 2

    def max_sq_norm(x_bf16):
        xf = x_bf16.astype(F32)
        return jnp.max(jnp.sum(xf * xf, axis=-1, keepdims=True), axis=0, keepdims=True)

    def prepare_queries(src_ref, tile, dst):
        rows = pl.ds(pl.multiple_of(tile * tq, tq), tq)
        cos, sin = cos_ref[rows, :], sin_ref[rows, :]
        q2max = None
        for h in range(A_GROUP):
            cols = slice(h * HEAD_DIM, (h + 1) * HEAD_DIM)
            q = _rms(src_ref[0, :, cols].astype(F32), gq_ref[...])
            q = (_rotary(q, cos, sin, AXIAL_SHIFT) * Q_SCALE).astype(BF16)
            qs_ref[dst, :, cols] = q
            q2max = max_sq_norm(q) if q2max is None else jnp.maximum(q2max, max_sq_norm(q))
        q2max_ref[dst] = jnp.broadcast_to(q2max, q2max_ref.shape[1:])

    @pl.when(qi == 0)
    def _():
        k = _rms(k_ref[0].astype(F32), gk_ref[...])
        k = _rotary(k, cos_ref[...], sin_ref[...], AXIAL_SHIFT).astype(BF16)
        kn_ref[...] = k
        k2max_ref[...] = jnp.broadcast_to(max_sq_norm(k), k2max_ref.shape)
        v1_ref[:, 0:HEAD_DIM] = v_ref[0]
        v1_ref[:, HEAD_DIM:] = jnp.ones((v1_ref.shape[0], HEAD_DIM), BF16)

    @pl.when((pl.program_id(0) == 0) & (pl.program_id(1) == 0) & (qi == 0))
    def _():
        prepare_queries(q_ref, 0, 0)

    bound = jnp.sqrt(q2max_ref[slot, 0:1, 0:1] * k2max_ref[0:1, 0:1]) * SCORE_BOUND_SLACK
    bound_is_safe = bound[0, 0] <= MAX_SAFE_SCORE_BOUND

    def attend(shift_of):
        prepare_queries(qnext_ref, (qi + 1) % n_q, 1 - slot)
        kn = kn_ref[...]
        v1 = v1_ref[...]
        for h in range(A_GROUP):
            cols = slice(h * HEAD_DIM, (h + 1) * HEAD_DIM)
            s = lax.dot_general(qs_ref[slot, :, cols], kn, (((1,), (1,)), ((), ())),
                                preferred_element_type=F32)
            p = jnp.exp2(s - shift_of(s))
            pv = jnp.dot(p.astype(BF16), v1, preferred_element_type=F32)
            o, l = pv[:, :HEAD_DIM], pv[:, HEAD_DIM:]
            g = ga_ref[0, :, cols].astype(F32)
            o_ref[0, :, cols] = (o * (1.0 / l) * _silu(g)).astype(BF16)

    @pl.when(bound_is_safe)
    def _():
        attend(lambda s: bound)

    @pl.when(jnp.logical_not(bound_is_safe))
    def _():
        attend(lambda s: jnp.max(s, axis=-1, keepdims=True))


def _attention_a(proj3, q_gain, k_gain, tables):
    B, S, _ = proj3.shape
    tq = ATTN_Q_TILE
    n_q = S // tq
    gw = A_GROUP * HEAD_DIM
    ka_blk = _OFF_KA // HEAD_DIM
    va_blk = _OFF_VA // HEAD_DIM
    ga_blk = _OFF_GA // gw
    full = pl.BlockSpec((S, HEAD_DIM), lambda b, g, i: (0, 0))
    vec = pl.BlockSpec((1, HEAD_DIM), lambda b, g, i: (0, 0))
    assert n_q % 2 == 0

    def next_step_q(b, g, i):
        n = jnp.minimum((b * A_KV_HEADS + g) * n_q + i + 1, B * A_KV_HEADS * n_q - 1)
        return n // (A_KV_HEADS * n_q), n % n_q, (n // n_q) % A_KV_HEADS

    return pl.pallas_call(
        functools.partial(_attn_a_kernel, tq=tq, n_q=n_q),
        grid=(B, A_KV_HEADS, n_q),
        in_specs=[
            pl.BlockSpec((1, tq, gw), lambda b, g, i: (b, i, g)),
            pl.BlockSpec((1, tq, gw), next_step_q),
            pl.BlockSpec((1, S, HEAD_DIM), lambda b, g, i: (b, 0, ka_blk + g)),
            pl.BlockSpec((1, S, HEAD_DIM), lambda b, g, i: (b, 0, va_blk + g)),
            pl.BlockSpec((1, tq, gw), lambda b, g, i: (b, i, ga_blk + g)),
            vec, vec, full, full,
        ],
        out_specs=pl.BlockSpec((1, tq, gw), lambda b, g, i: (b, i, g)),
        out_shape=jax.ShapeDtypeStruct((B, S, A_WIDTH), BF16),
        scratch_shapes=[
            pltpu.VMEM((S, HEAD_DIM), BF16),
            pltpu.VMEM((S, 2 * HEAD_DIM), BF16),
            pltpu.VMEM((8, HEAD_DIM), F32),
            pltpu.VMEM((2, tq, gw), BF16),
            pltpu.VMEM((2, 8, HEAD_DIM), F32),
        ],
        compiler_params=pltpu.CompilerParams(
            dimension_semantics=("arbitrary", "arbitrary", "arbitrary"),
            vmem_limit_bytes=V7X_VMEM_LIMIT_BYTES),
        name="attention_a",
    )(proj3, proj3, proj3, proj3, proj3, q_gain, k_gain, *tables)


def _band_tile(q, kw, vw, valid):
    s = lax.dot_general(q, kw, (((1,), (1,)), ((), ())), preferred_element_type=F32)
    s = jnp.where(valid, s, MASK_VALUE)
    m = jnp.max(s, axis=-1, keepdims=True)
    p = jnp.exp2(s - m)
    pv = jnp.dot(p.astype(BF16), vw, preferred_element_type=F32)
    o, l = pv[:, :HEAD_DIM], pv[:, HEAD_DIM:]
    return o * (1.0 / l), m + jnp.log2(l)


def _window_mask(t, n_tiles):
    qi = lax.broadcasted_iota(jnp.int32, (BAND_TQ, BAND_WIN), 0)
    kj = lax.broadcasted_iota(jnp.int32, (BAND_TQ, BAND_WIN), 1)
    kpos = kj + (t * BAND_TQ - BAND_HALF)
    return (kj >= qi) & (kj <= qi + 2 * BAND_HALF) & (kpos >= 0) & (kpos < n_tiles * BAND_TQ)


def _zero_pads(ref, lead, n_rows):
    zeros = jnp.zeros((BAND_HALF, HEAD_DIM), BF16)
    lanes = slice(0, HEAD_DIM)
    for idx in lead:
        ref[idx + (slice(0, BAND_HALF), lanes)] = zeros
        ref[idx + (slice(BAND_HALF + n_rows, 2 * BAND_HALF + n_rows), lanes)] = zeros


def _band_mix_kernel(q1_ref, k1_ref, v1_ref, q4_ref, k4_ref, v4_ref, q16_ref, k16_ref, v16_ref,
                     gb_ref, cos_ref, sin_ref, perm_ref, y_ref,
                     nat_ref, tmp_ref, qd1_ref, kp1_ref, vp1_ref, qd4_ref, kp4_ref, vp4_ref,
                     qd16_ref, kd16_ref, vd16_ref, od16_ref, og_ref, lg_ref, *, S):
    L4 = S // SPLIT
    L16 = L4 // SPLIT
    n1, n4 = S // BAND_TQ, L4 // BAND_TQ
    per_tile = BAND_TQ // L16

    def rotated(src_ref, scale=None):
        x = src_ref[0]
        partner = jnp.dot(x, perm_ref[...], preferred_element_type=F32)
        y = x.astype(F32) * cos_ref[...] + partner * sin_ref[...]
        return y if scale is None else y * scale

    lanes = slice(0, HEAD_DIM)
    vp1_ref[:, HEAD_DIM:] = jnp.ones((vp1_ref.shape[0], HEAD_DIM), BF16)
    vp4_ref[:, :, HEAD_DIM:] = jnp.ones(vp4_ref.shape[:2] + (HEAD_DIM,), BF16)
    vd16_ref[:, :, HEAD_DIM:] = jnp.ones(vd16_ref.shape[:2] + (HEAD_DIM,), BF16)

    _zero_pads(kp1_ref, [()], S)
    _zero_pads(vp1_ref, [()], S)
    qd1_ref[...] = rotated(q1_ref, Q_SCALE).astype(BF16)
    kp1_ref[BAND_HALF:BAND_HALF + S, :] = rotated(k1_ref).astype(BF16)
    vp1_ref[BAND_HALF:BAND_HALF + S, lanes] = v1_ref[0]

    for t in range(n1):
        rows = slice(t * BAND_TQ, (t + 1) * BAND_TQ)
        win = slice(t * BAND_TQ, t * BAND_TQ + BAND_WIN)
        o, lse = _band_tile(qd1_ref[rows, :], kp1_ref[win, :], vp1_ref[win, :],
                            _window_mask(t, n1))
        og_ref[0, rows, :] = o
        lg_ref[0, rows, :] = lse

    _zero_pads(kp4_ref, [(r,) for r in range(SPLIT)], L4)
    _zero_pads(vp4_ref, [(r,) for r in range(SPLIT)], L4)
    for n, (val, dst_ref, lead) in enumerate(((rotated(q4_ref, Q_SCALE), qd4_ref, 0),
                                              (rotated(k4_ref), kp4_ref, BAND_HALF),
                                              (v4_ref[0].astype(F32), vp4_ref, BAND_HALF))):
        nat_ref[n] = val
        for r in range(SPLIT):
            dst_ref[r, lead:lead + L4, lanes] = \
                nat_ref[n, pl.ds(r, L4, stride=SPLIT), :].astype(BF16)

    for r in range(SPLIT):
        for t in range(n4):
            rows = pl.ds(r + t * (BAND_TQ * SPLIT), BAND_TQ, stride=SPLIT)
            win = slice(t * BAND_TQ, t * BAND_TQ + BAND_WIN)
            o, lse = _band_tile(qd4_ref[r, t * BAND_TQ:(t + 1) * BAND_TQ, :],
                                kp4_ref[r, win, :], vp4_ref[r, win, :], _window_mask(t, n4))
            og_ref[1, rows, :] = o
            lg_ref[1, rows, :] = lse

    for n, (val, dst_ref) in enumerate(((rotated(q16_ref, Q_SCALE), qd16_ref),
                                        (rotated(k16_ref), kd16_ref),
                                        (v16_ref[0].astype(F32), vd16_ref))):
        nat_ref[3 + n] = val
        for a in range(SPLIT):
            tmp_ref[n, a] = nat_ref[3 + n, pl.ds(a, L4, stride=SPLIT), :]
            for b in range(SPLIT):
                r = a + SPLIT * b
                dst_ref[r // per_tile, (r % per_tile) * L16:(r % per_tile + 1) * L16, lanes] = \
                    tmp_ref[n, a, pl.ds(b, L16, stride=SPLIT), :].astype(BF16)

    qi = lax.broadcasted_iota(jnp.int32, (BAND_TQ, BAND_TQ), 0)
    kj = lax.broadcasted_iota(jnp.int32, (BAND_TQ, BAND_TQ), 1)
    same_residue = functools.reduce(
        jnp.logical_or, [(qi >= c * L16) & (qi < (c + 1) * L16) & (kj >= c * L16)
                         & (kj < (c + 1) * L16) for c in range(per_tile)])
    stacked_band = same_residue & (kj >= qi - BAND_HALF) & (kj <= qi + BAND_HALF)
    for j in range(SPLIT * SPLIT // per_tile):
        o, lse = _band_tile(qd16_ref[j], kd16_ref[j], vd16_ref[j], stacked_band)
        od16_ref[0, j] = o
        od16_ref[1, j] = lse

    for n, dst_ref in enumerate((og_ref, lg_ref)):
        for a in range(SPLIT):
            for b in range(SPLIT):
                r = a + SPLIT * b
                tmp_ref[n, a, pl.ds(b, L16, stride=SPLIT), :] = \
                    od16_ref[n, r // per_tile, (r % per_tile) * L16:(r % per_tile + 1) * L16, :]
            dst_ref[2, pl.ds(a, L4, stride=SPLIT), :] = tmp_ref[n, a]

    chunk = MIX_ROWS

    def mix(i, carry):
        rows = pl.ds(pl.multiple_of(i * chunk, chunk), chunk)
        lses = [lg_ref[g, rows, :] for g in range(B_GROUPS)]
        lmax = functools.reduce(jnp.maximum, lses)
        es = [jnp.exp2(l - lmax) for l in lses]
        num = sum(e * og_ref[g, rows, :] for g, e in enumerate(es))
        ob = num * (1.0 / sum(es))
        y_ref[0, rows, :] = (ob * _silu(gb_ref[0, rows, :].astype(F32))).astype(BF16)
        return carry

    lax.fori_loop(0, S // chunk, mix, 0)


def _band_mix(proj3, tables):
    B, S, NC = proj3.shape
    assert tuple(d for _, d in B_PATTERNS) == (1, SPLIT, SPLIT * SPLIT)
    L4, L16 = S // SPLIT, S // (SPLIT * SPLIT)
    assert L4 % BAND_TQ == 0 and BAND_TQ % L16 == 0
    n16 = S // BAND_TQ

    def head_blk(off, g):
        first = off // HEAD_DIM + g * B_HEADS
        return pl.BlockSpec((1, S, HEAD_DIM), lambda b, h: (b, 0, first + h))

    qkv_specs = [head_blk(off, g) for g in range(B_GROUPS)
                 for off in (_OFF_QB, _OFF_KB, _OFF_VB)]
    gb_first = _OFF_GB // HEAD_DIM
    table = pl.BlockSpec((S, HEAD_DIM), lambda b, h: (0, 0))
    return pl.pallas_call(
        functools.partial(_band_mix_kernel, S=S),
        grid=(B, B_HEADS),
        in_specs=qkv_specs
        + [pl.BlockSpec((1, S, HEAD_DIM), lambda b, h: (b, 0, gb_first + h)), table, table,
           pl.BlockSpec((HEAD_DIM, HEAD_DIM), lambda b, h: (0, 0))],
        out_specs=pl.BlockSpec((1, S, HEAD_DIM), lambda b, h: (b, 0, h)),
        out_shape=jax.ShapeDtypeStruct((B, S, B_WIDTH), BF16),
        scratch_shapes=[
            pltpu.VMEM((6, S, HEAD_DIM), F32),
            pltpu.VMEM((3, SPLIT, L4, HEAD_DIM), F32),
            pltpu.VMEM((S, HEAD_DIM), BF16),
            pltpu.VMEM((S + 2 * BAND_HALF, HEAD_DIM), BF16),
            pltpu.VMEM((S + 2 * BAND_HALF, 2 * HEAD_DIM), BF16),
            pltpu.VMEM((SPLIT, L4, HEAD_DIM), BF16),
            pltpu.VMEM((SPLIT, L4 + 2 * BAND_HALF, HEAD_DIM), BF16),
            pltpu.VMEM((SPLIT, L4 + 2 * BAND_HALF, 2 * HEAD_DIM), BF16),
            pltpu.VMEM((n16, BAND_TQ, HEAD_DIM), BF16),
            pltpu.VMEM((n16, BAND_TQ, HEAD_DIM), BF16),
            pltpu.VMEM((n16, BAND_TQ, 2 * HEAD_DIM), BF16),
            pltpu.VMEM((2, n16, BAND_TQ, HEAD_DIM), F32),
            pltpu.VMEM((B_GROUPS, S, HEAD_DIM), F32),
            pltpu.VMEM((B_GROUPS, S, HEAD_DIM), F32),
        ],
        compiler_params=pltpu.CompilerParams(
            dimension_semantics=("parallel", "parallel"),
            vmem_limit_bytes=V7X_VMEM_LIMIT_BYTES),
        name="band_mix",
    )(*([proj3] * (3 * B_GROUPS + 1)), *tables, _partner_permutation(PARTIAL_ROPE_DIM))


def _merge_kernel(x_ref, ya_ref, yb_ref, za_ref, zb_ref, bias_ref,
                  wa_ref, wb_ref, wo_ref, fg_ref, out_ref, merged_ref, *, final_norm):
    pa = jnp.dot(ya_ref[...], wa_ref[...], preferred_element_type=F32)
    pb = jnp.dot(yb_ref[...], wb_ref[...], preferred_element_type=F32)
    for c in range(x_ref.shape[1] // COL_BLK):
        cols = slice(c * COL_BLK, (c + 1) * COL_BLK)
        gate_a = jax.nn.sigmoid(za_ref[:, cols].astype(F32) + bias_ref[0:1, cols])
        gate_b = jax.nn.sigmoid(zb_ref[:, cols].astype(F32) + bias_ref[1:2, cols])
        merged_ref[:, cols] = (gate_a * pa[:, cols] + gate_b * pb[:, cols]).astype(BF16)

    y = x_ref[...] + jnp.dot(merged_ref[...], wo_ref[...], preferred_element_type=F32)
    out_ref[...] = _rms(y, fg_ref[...]) if final_norm else y


def _merge(x2, ya, yb, proj2, bias, wa, wb, wo, final_gain, final_norm):
    T, D = x2.shape
    tm = MERGE_ROW_TILE
    def row_blk(width, col=0):
        return pl.BlockSpec((tm, width), lambda i: (i, col))

    def whole(a):
        return pl.BlockSpec(a.shape, lambda i: (0, 0))

    z_specs = [pl.BlockSpec((pl.Element(tm), pl.Element(D)), lambda i, off=off: (i * tm, off))
               for off in (_OFF_ZA, _OFF_ZA + D)]
    return pl.pallas_call(
        functools.partial(_merge_kernel, final_norm=final_norm),
        grid=(T // tm,),
        in_specs=[row_blk(D), row_blk(A_WIDTH), row_blk(B_WIDTH)] + z_specs
        + [whole(bias), whole(wa), whole(wb), whole(wo), whole(final_gain)],
        out_specs=row_blk(D),
        out_shape=jax.ShapeDtypeStruct((T, D), F32),
        scratch_shapes=[pltpu.VMEM((tm, D), BF16)],
        compiler_params=pltpu.CompilerParams(
            dimension_semantics=("parallel",),
            vmem_limit_bytes=V7X_VMEM_LIMIT_BYTES),
        name="merge_output",
    )(x2, ya, yb, proj2, proj2, bias, wa, wb, wo, final_gain)


def _angles(pos, dim, theta):
    expo = np.arange(0, dim, 2, dtype=np.float64) / dim
    return pos.astype(np.float64)[:, None] / np.power(float(theta), expo)[None, :]


def _lane_tables(ang_blocks):
    cos, sin = [], []
    for blk in ang_blocks:
        if isinstance(blk, int):
            S = cos[0].shape[0]
            cos.append(np.ones((S, blk)))
            sin.append(np.zeros((S, blk)))
            continue
        c, s = np.cos(blk), np.sin(blk)
        cos += [c, c]
        sin += [-s, s]
    return tuple(jnp.asarray(np.concatenate(t, axis=-1), dtype=F32) for t in (cos, sin))


def _partner_permutation(rotary_dim):
    half = rotary_dim // 2
    p = np.zeros((HEAD_DIM, HEAD_DIM))
    for i in range(rotary_dim):
        p[i + half if i < half else i - half, i] = 1.0
    return jnp.asarray(p, dtype=BF16)


def _axial_tables(S):
    pos = np.arange(S)
    half = HEAD_DIM // 2
    return _lane_tables([_angles(pos // GRID_W, half, AXIAL_THETA),
                         _angles(pos % GRID_W, half, AXIAL_THETA)])


def _partial_tables(S):
    pos = np.arange(S)
    return _lane_tables([_angles(pos, PARTIAL_ROPE_DIM, ROPE_THETA),
                         HEAD_DIM - PARTIAL_ROPE_DIM])


def kernel(x, norm_gain, w_in, q_norm_gain, k_norm_gain, merge_gate_bias, w_branch_a,
           w_branch_b, w_out, final_norm_gain):
    B, S, D = x.shape
    depth = norm_gain.shape[0]
    n_cols = _OFF_ZA + 2 * D
    assert w_in.shape[2] == n_cols and S % GRID_W == 0
    assert all(w // (2 * d) == BAND_HALF for w, d in B_PATTERNS)
    axial = _axial_tables(S)
    partial = _partial_tables(S)
    x2 = x.reshape(B * S, D)
    for l in range(depth):
        proj2 = _input_projection(x2, norm_gain[l][None, :], w_in[l])
        proj3 = proj2.reshape(B, S, n_cols)
        ya = _attention_a(proj3, q_norm_gain[l][None, :], k_norm_gain[l][None, :], axial)
        yb = _band_mix(proj3, partial)
        x2 = _merge(x2, ya.reshape(B * S, A_WIDTH), yb.reshape(B * S, B_WIDTH), proj2,
                    merge_gate_bias[l], w_branch_a[l].astype(BF16),
                    w_branch_b[l].astype(BF16), w_out[l].astype(BF16),
                    final_norm_gain[None, :], final_norm=(l == depth - 1))
    return x2.reshape(B, S, D)
```

```python
import functools
import math

import numpy as np
import jax
import jax.numpy as jnp
from jax import lax
from jax.experimental import pallas as pl
from jax.experimental.pallas import tpu as pltpu

F32 = jnp.float32
BF16 = jnp.bfloat16

HEAD_DIM = 128
GRID_W = 64
NORM_EPS = 1e-6
A_Q_HEADS = 8
A_KV_HEADS = 2
A_GROUP = A_Q_HEADS // A_KV_HEADS
A_WIDTH = A_Q_HEADS * HEAD_DIM
A_KV_WIDTH = A_KV_HEADS * HEAD_DIM
AXIAL_THETA = 10000.0
AXIAL_SHIFT = HEAD_DIM // 4
MAX_SAFE_SCORE_BOUND = 50.0
SCORE_BOUND_SLACK = 1.01
B_PATTERNS = ((128, 1), (512, 4), (2048, 16))
B_GROUPS = len(B_PATTERNS)
B_HEADS = 4
B_WIDTH = B_HEADS * HEAD_DIM
PARTIAL_ROPE_DIM = HEAD_DIM // 4
PARTIAL_SHIFT = PARTIAL_ROPE_DIM // 2
ROPE_THETA = 500000.0
BAND_HALF = 64
BAND_TQ = 128
BAND_WIN = BAND_TQ + 2 * BAND_HALF
MIX_ROWS = 256
SPLIT = 4
MASK_VALUE = -1e30
LOG2_E = math.log2(math.e)
Q_SCALE = HEAD_DIM ** -0.5 * LOG2_E

COL_BLK = 512
_OFF_QA = 0
_OFF_KA = A_WIDTH
_OFF_VA = _OFF_KA + A_KV_WIDTH
_OFF_GA = _OFF_VA + A_KV_WIDTH
_OFF_QB = _OFF_GA + A_WIDTH
_OFF_KB = _OFF_QB + B_GROUPS * B_WIDTH
_OFF_VB = _OFF_KB + B_GROUPS * B_WIDTH
_OFF_GB = _OFF_VB + B_GROUPS * B_WIDTH
_OFF_ZA = _OFF_GB + B_WIDTH

V7X_VMEM_LIMIT_BYTES = 56 * 1024 * 1024
PROJ_ROW_TILE = 512
PROJ_COL_TILE = 2 * COL_BLK
PROJ_ROW_TILES_PER_WEIGHT_TILE = 4
ATTN_Q_TILE = 1024
MERGE_ROW_TILE = 512


def _rotary(x, cos, sin, shift):
    lane = lax.broadcasted_iota(jnp.int32, x.shape, 1)
    up = pltpu.roll(x, HEAD_DIM - shift, 1)
    dn = pltpu.roll(x, shift, 1)
    partner = jnp.where(lane % (2 * shift) < shift, up, dn)
    return x * cos + partner * sin


def _rms(x, gain):
    ms = jnp.mean(x * x, axis=-1, keepdims=True)
    return x * lax.rsqrt(ms + NORM_EPS) * gain


def _silu(g):
    return g * jax.nn.sigmoid(g)


def _proj_kernel(x_ref, g_ref, w_ref, o_ref, h_ref, *, n_full, tail, tm, n_rows):
    s = pl.program_id(1)
    tn = w_ref.shape[1]
    assert n_full >= 1

    for r in range(n_rows):
        @pl.when(s == r)
        def _():
            h = _rms(x_ref[...], g_ref[...]).astype(BF16)
            h_ref[r] = h
            o_ref[r * tm:(r + 1) * tm, :] = jnp.dot(
                h, w_ref[...].astype(BF16), preferred_element_type=F32).astype(BF16)

    def all_row_tiles(width):
        w = w_ref[:, 0:width].astype(BF16)
        for r in range(n_rows):
            o_ref[r * tm:(r + 1) * tm, 0:width] = jnp.dot(
                h_ref[r], w, preferred_element_type=F32).astype(BF16)

    col = s - (n_rows - 1)

    @pl.when((col >= 1) & (col < n_full))
    def _():
        all_row_tiles(tn)

    if tail:
        @pl.when(col == n_full)
        def _():
            all_row_tiles(tail)


def _input_projection(x2, gain, w):
    T, D = x2.shape
    N = w.shape[1]
    tm, tn, R = PROJ_ROW_TILE, PROJ_COL_TILE, PROJ_ROW_TILES_PER_WEIGHT_TILE
    n_full, tail = divmod(N, tn)
    n_col = pl.cdiv(N, tn)

    def col_tile(s):
        return jnp.maximum(s - (R - 1), 0)

    return pl.pallas_call(
        functools.partial(_proj_kernel, n_full=n_full, tail=tail, tm=tm, n_rows=R),
        grid=(T // (tm * R), R - 1 + n_col),
        in_specs=[
            pl.BlockSpec((tm, D), lambda g, s: (g * R + jnp.minimum(s, R - 1), 0)),
            pl.BlockSpec((1, D), lambda g, s: (0, 0)),
            pl.BlockSpec((D, tn), lambda g, s: (0, col_tile(s))),
        ],
        out_specs=pl.BlockSpec((R * tm, tn), lambda g, s: (g, col_tile(s))),
        out_shape=jax.ShapeDtypeStruct((T, N), BF16),
        scratch_shapes=[pltpu.VMEM((R, tm, D), BF16)],
        compiler_params=pltpu.CompilerParams(
            dimension_semantics=("parallel", "arbitrary"),
            vmem_limit_bytes=V7X_VMEM_LIMIT_BYTES),
        name="input_projection",
    )(x2, gain, w)


def _attn_a_kernel(q_ref, qnext_ref, k_ref, v_ref, ga_ref, gq_ref, gk_ref, cos_ref, sin_ref,
                   o_ref, kn_ref, v1_ref, k2max_ref, qs_ref, q2max_ref, *, tq, n_q):
    qi = pl.program_id(2)
    slot = qi % 2

    def max_sq_norm(x_bf16):
        xf = x_bf16.astype(F32)
        return jnp.max(jnp.sum(xf * xf, axis=-1, keepdims=True), axis=0, keepdims=True)

    def prepare_queries(src_ref, tile, dst):
        rows = pl.ds(pl.multiple_of(tile * tq, tq), tq)
        cos, sin = cos_ref[rows, :], sin_ref[rows, :]
        q2max = None
        for h in range(A_GROUP):
            cols = slice(h * HEAD_DIM, (h + 1) * HEAD_DIM)
            q = _rms(src_ref[0, :, cols].astype(F32), gq_ref[...])
            q = (_rotary(q, cos, sin, AXIAL_SHIFT) * Q_SCALE).astype(BF16)
            qs_ref[dst, :, cols] = q
            q2max = max_sq_norm(q) if q2max is None else jnp.maximum(q2max, max_sq_norm(q))
        q2max_ref[dst] = jnp.broadcast_to(q2max, q2max_ref.shape[1:])

    @pl.when(qi == 0)
    def _():
        k = _rms(k_ref[0].astype(F32), gk_ref[...])
        k = _rotary(k, cos_ref[...], sin_ref[...], AXIAL_SHIFT).astype(BF16)
        kn_ref[...] = k
        k2max_ref[...] = jnp.broadcast_to(max_sq_norm(k), k2max_ref.shape)
        v1_ref[:, 0:HEAD_DIM] = v_ref[0]
        v1_ref[:, HEAD_DIM:] = jnp.ones((v1_ref.shape[0], HEAD_DIM), BF16)

    @pl.when((pl.program_id(0) == 0) & (pl.program_id(1) == 0) & (qi == 0))
    def _():
        prepare_queries(q_ref, 0, 0)

    bound = jnp.sqrt(q2max_ref[slot, 0:1, 0:1] * k2max_ref[0:1, 0:1]) * SCORE_BOUND_SLACK
    bound_is_safe = bound[0, 0] <= MAX_SAFE_SCORE_BOUND

    def attend(shift_of):
        prepare_queries(qnext_ref, (qi + 1) % n_q, 1 - slot)
        kn = kn_ref[...]
        v1 = v1_ref[...]
        for h in range(A_GROUP):
            cols = slice(h * HEAD_DIM, (h + 1) * HEAD_DIM)
            s = lax.dot_general(qs_ref[slot, :, cols], kn, (((1,), (1,)), ((), ())),
                                preferred_element_type=F32)
            p = jnp.exp2(s - shift_of(s))
            pv = jnp.dot(p.astype(BF16), v1, preferred_element_type=F32)
            o, l = pv[:, :HEAD_DIM], pv[:, HEAD_DIM:]
            g = ga_ref[0, :, cols].astype(F32)
            o_ref[0, :, cols] = (o * (1.0 / l) * _silu(g)).astype(BF16)

    @pl.when(bound_is_safe)
    def _():
        attend(lambda s: bound)

    @pl.when(jnp.logical_not(bound_is_safe))
    def _():
        attend(lambda s: jnp.max(s, axis=-1, keepdims=True))


def _attention_a(proj3, q_gain, k_gain, tables):
    B, S, _ = proj3.shape
    tq = ATTN_Q_TILE
    n_q = S // tq
    gw = A_GROUP * HEAD_DIM
    ka_blk = _OFF_KA // HEAD_DIM
    va_blk = _OFF_VA // HEAD_DIM
    ga_blk = _OFF_GA // gw
    full = pl.BlockSpec((S, HEAD_DIM), lambda b, g, i: (0, 0))
    vec = pl.BlockSpec((1, HEAD_DIM), lambda b, g, i: (0, 0))
    assert n_q % 2 == 0

    def next_step_q(b, g, i):
        n = jnp.minimum((b * A_KV_HEADS + g) * n_q + i + 1, B * A_KV_HEADS * n_q - 1)
        return n // (A_KV_HEADS * n_q), n % n_q, (n // n_q) % A_KV_HEADS

    return pl.pallas_call(
        functools.partial(_attn_a_kernel, tq=tq, n_q=n_q),
        grid=(B, A_KV_HEADS, n_q),
        in_specs=[
            pl.BlockSpec((1, tq, gw), lambda b, g, i: (b, i, g)),
            pl.BlockSpec((1, tq, gw), next_step_q),
            pl.BlockSpec((1, S, HEAD_DIM), lambda b, g, i: (b, 0, ka_blk + g)),
            pl.BlockSpec((1, S, HEAD_DIM), lambda b, g, i: (b, 0, va_blk + g)),
            pl.BlockSpec((1, tq, gw), lambda b, g, i: (b, i, ga_blk + g)),
            vec, vec, full, full,
        ],
        out_specs=pl.BlockSpec((1, tq, gw), lambda b, g, i: (b, i, g)),
        out_shape=jax.ShapeDtypeStruct((B, S, A_WIDTH), BF16),
        scratch_shapes=[
            pltpu.VMEM((S, HEAD_DIM), BF16),
            pltpu.VMEM((S, 2 * HEAD_DIM), BF16),
            pltpu.VMEM((8, HEAD_DIM), F32),
            pltpu.VMEM((2, tq, gw), BF16),
            pltpu.VMEM((2, 8, HEAD_DIM), F32),
        ],
        compiler_params=pltpu.CompilerParams(
            dimension_semantics=("arbitrary", "arbitrary", "arbitrary"),
            vmem_limit_bytes=V7X_VMEM_LIMIT_BYTES),
        name="attention_a",
    )(proj3, proj3, proj3, proj3, proj3, q_gain, k_gain, *tables)


def _band_tile(q, kw, vw, valid):
    s = lax.dot_general(q, kw, (((1,), (1,)), ((), ())), preferred_element_type=F32)
    s = jnp.where(valid, s, MASK_VALUE)
    m = jnp.max(s, axis=-1, keepdims=True)
    p = jnp.exp2(s - m)
    pv = jnp.dot(p.astype(BF16), vw, preferred_element_type=F32)
    o, l = pv[:, :HEAD_DIM], pv[:, HEAD_DIM:]
    return o * (1.0 / l), m + jnp.log2(l)


def _window_mask(t, n_tiles):
    qi = lax.broadcasted_iota(jnp.int32, (BAND_TQ, BAND_WIN), 0)
    kj = lax.broadcasted_iota(jnp.int32, (BAND_TQ, BAND_WIN), 1)
    kpos = kj + (t * BAND_TQ - BAND_HALF)
    return (kj >= qi) & (kj <= qi + 2 * BAND_HALF) & (kpos >= 0) & (kpos < n_tiles * BAND_TQ)


def _zero_pads(ref, lead, n_rows):
    zeros = jnp.zeros((BAND_HALF, HEAD_DIM), BF16)
    lanes = slice(0, HEAD_DIM)
    for idx in lead:
        ref[idx + (slice(0, BAND_HALF), lanes)] = zeros
        ref[idx + (slice(BAND_HALF + n_rows, 2 * BAND_HALF + n_rows), lanes)] = zeros


def _band_mix_kernel(q1_ref, k1_ref, v1_ref, q4_ref, k4_ref, v4_ref, q16_ref, k16_ref, v16_ref,
                     gb_ref, cos_ref, sin_ref, perm_ref, y_ref,
                     nat_ref, tmp_ref, qd1_ref, kp1_ref, vp1_ref, qd4_ref, kp4_ref, vp4_ref,
                     qd16_ref, kd16_ref, vd16_ref, od16_ref, og_ref, lg_ref, *, S):
    L4 = S // SPLIT
    L16 = L4 // SPLIT
    n1, n4 = S // BAND_TQ, L4 // BAND_TQ
    per_tile = BAND_TQ // L16

    def rotated(src_ref, scale=None):
        x = src_ref[0]
        partner = jnp.dot(x, perm_ref[...], preferred_element_type=F32)
        y = x.astype(F32) * cos_ref[...] + partner * sin_ref[...]
        return y if scale is None else y * scale

    lanes = slice(0, HEAD_DIM)
    vp1_ref[:, HEAD_DIM:] = jnp.ones((vp1_ref.shape[0], HEAD_DIM), BF16)
    vp4_ref[:, :, HEAD_DIM:] = jnp.ones(vp4_ref.shape[:2] + (HEAD_DIM,), BF16)
    vd16_ref[:, :, HEAD_DIM:] = jnp.ones(vd16_ref.shape[:2] + (HEAD_DIM,), BF16)

    _zero_pads(kp1_ref, [()], S)
    _zero_pads(vp1_ref, [()], S)
    qd1_ref[...] = rotated(q1_ref, Q_SCALE).astype(BF16)
    kp1_ref[BAND_HALF:BAND_HALF + S, :] = rotated(k1_ref).astype(BF16)
    vp1_ref[BAND_HALF:BAND_HALF + S, lanes] = v1_ref[0]

    for t in range(n1):
        rows = slice(t * BAND_TQ, (t + 1) * BAND_TQ)
        win = slice(t * BAND_TQ, t * BAND_TQ + BAND_WIN)
        o, lse = _band_tile(qd1_ref[rows, :], kp1_ref[win, :], vp1_ref[win, :],
                            _window_mask(t, n1))
        og_ref[0, rows, :] = o
        lg_ref[0, rows, :] = lse

    _zero_pads(kp4_ref, [(r,) for r in range(SPLIT)], L4)
    _zero_pads(vp4_ref, [(r,) for r in range(SPLIT)], L4)
    for n, (val, dst_ref, lead) in enumerate(((rotated(q4_ref, Q_SCALE), qd4_ref, 0),
                                              (rotated(k4_ref), kp4_ref, BAND_HALF),
                                              (v4_ref[0].astype(F32), vp4_ref, BAND_HALF))):
        nat_ref[n] = val
        for r in range(SPLIT):
            dst_ref[r, lead:lead + L4, lanes] = \
                nat_ref[n, pl.ds(r, L4, stride=SPLIT), :].astype(BF16)

    for r in range(SPLIT):
        for t in range(n4):
            rows = pl.ds(r + t * (BAND_TQ * SPLIT), BAND_TQ, stride=SPLIT)
            win = slice(t * BAND_TQ, t * BAND_TQ + BAND_WIN)
            o, lse = _band_tile(qd4_ref[r, t * BAND_TQ:(t + 1) * BAND_TQ, :],
                                kp4_ref[r, win, :], vp4_ref[r, win, :], _window_mask(t, n4))
            og_ref[1, rows, :] = o
            lg_ref[1, rows, :] = lse

    for n, (val, dst_ref) in enumerate(((rotated(q16_ref, Q_SCALE), qd16_ref),
                                        (rotated(k16_ref), kd16_ref),
                                        (v16_ref[0].astype(F32), vd16_ref))):
        nat_ref[3 + n] = val
        for a in range(SPLIT):
            tmp_ref[n, a] = nat_ref[3 + n, pl.ds(a, L4, stride=SPLIT), :]
            for b in range(SPLIT):
                r = a + SPLIT * b
                dst_ref[r // per_tile, (r % per_tile) * L16:(r % per_tile + 1) * L16, lanes] = \
                    tmp_ref[n, a, pl.ds(b, L16, stride=SPLIT), :].astype(BF16)

    qi = lax.broadcasted_iota(jnp.int32, (BAND_TQ, BAND_TQ), 0)
    kj = lax.broadcasted_iota(jnp.int32, (BAND_TQ, BAND_TQ), 1)
    same_residue = functools.reduce(
        jnp.logical_or, [(qi >= c * L16) & (qi < (c + 1) * L16) & (kj >= c * L16)
                         & (kj < (c + 1) * L16) for c in range(per_tile)])
    stacked_band = same_residue & (kj >= qi - BAND_HALF) & (kj <= qi + BAND_HALF)
    for j in range(SPLIT * SPLIT // per_tile):
        o, lse = _band_tile(qd16_ref[j], kd16_ref[j], vd16_ref[j], stacked_band)
        od16_ref[0, j] = o
        od16_ref[1, j] = lse

    for n, dst_ref in enumerate((og_ref, lg_ref)):
        for a in range(SPLIT):
            for b in range(SPLIT):
                r = a + SPLIT * b
                tmp_ref[n, a, pl.ds(b, L16, stride=SPLIT), :] = \
                    od16_ref[n, r // per_tile, (r % per_tile) * L16:(r % per_tile + 1) * L16, :]
            dst_ref[2, pl.ds(a, L4, stride=SPLIT), :] = tmp_ref[n, a]

    chunk = MIX_ROWS

    def mix(i, carry):
        rows = pl.ds(pl.multiple_of(i * chunk, chunk), chunk)
        lses = [lg_ref[g, rows, :] for g in range(B_GROUPS)]
        lmax = functools.reduce(jnp.maximum, lses)
        es = [jnp.exp2(l - lmax) for l in lses]
        num = sum(e * og_ref[g, rows, :] for g, e in enumerate(es))
        ob = num * (1.0 / sum(es))
        y_ref[0, rows, :] = (ob * _silu(gb_ref[0, rows, :].astype(F32))).astype(BF16)
        return carry

    lax.fori_loop(0, S // chunk, mix, 0)


def _band_mix(proj3, tables):
    B, S, NC = proj3.shape
    assert tuple(d for _, d in B_PATTERNS) == (1, SPLIT, SPLIT * SPLIT)
    L4, L16 = S // SPLIT, S // (SPLIT * SPLIT)
    assert L4 % BAND_TQ == 0 and BAND_TQ % L16 == 0
    n16 = S // BAND_TQ

    def head_blk(off, g):
        first = off // HEAD_DIM + g * B_HEADS
        return pl.BlockSpec((1, S, HEAD_DIM), lambda b, h: (b, 0, first + h))

    qkv_specs = [head_blk(off, g) for g in range(B_GROUPS)
                 for off in (_OFF_QB, _OFF_KB, _OFF_VB)]
    gb_first = _OFF_GB // HEAD_DIM
    table = pl.BlockSpec((S, HEAD_DIM), lambda b, h: (0, 0))
    return pl.pallas_call(
        functools.partial(_band_mix_kernel, S=S),
        grid=(B, B_HEADS),
        in_specs=qkv_specs
        + [pl.BlockSpec((1, S, HEAD_DIM), lambda b, h: (b, 0, gb_first + h)), table, table,
           pl.BlockSpec((HEAD_DIM, HEAD_DIM), lambda b, h: (0, 0))],
        out_specs=pl.BlockSpec((1, S, HEAD_DIM), lambda b, h: (b, 0, h)),
        out_shape=jax.ShapeDtypeStruct((B, S, B_WIDTH), BF16),
        scratch_shapes=[
            pltpu.VMEM((6, S, HEAD_DIM), F32),
            pltpu.VMEM((3, SPLIT, L4, HEAD_DIM), F32),
            pltpu.VMEM((S, HEAD_DIM), BF16),
            pltpu.VMEM((S + 2 * BAND_HALF, HEAD_DIM), BF16),
            pltpu.VMEM((S + 2 * BAND_HALF, 2 * HEAD_DIM), BF16),
            pltpu.VMEM((SPLIT, L4, HEAD_DIM), BF16),
            pltpu.VMEM((SPLIT, L4 + 2 * BAND_HALF, HEAD_DIM), BF16),
            pltpu.VMEM((SPLIT, L4 + 2 * BAND_HALF, 2 * HEAD_DIM), BF16),
            pltpu.VMEM((n16, BAND_TQ, HEAD_DIM), BF16),
            pltpu.VMEM((n16, BAND_TQ, HEAD_DIM), BF16),
            pltpu.VMEM((n16, BAND_TQ, 2 * HEAD_DIM), BF16),
            pltpu.VMEM((2, n16, BAND_TQ, HEAD_DIM), F32),
            pltpu.VMEM((B_GROUPS, S, HEAD_DIM), F32),
            pltpu.VMEM((B_GROUPS, S, HEAD_DIM), F32),
        ],
        compiler_params=pltpu.CompilerParams(
            dimension_semantics=("parallel", "parallel"),
            vmem_limit_bytes=V7X_VMEM_LIMIT_BYTES),
        name="band_mix",
    )(*([proj3] * (3 * B_GROUPS + 1)), *tables, _partner_permutation(PARTIAL_ROPE_DIM))


def _merge_kernel(x_ref, ya_ref, yb_ref, za_ref, zb_ref, bias_ref,
                  wa_ref, wb_ref, wo_ref, fg_ref, out_ref, merged_ref, *, final_norm):
    pa = jnp.dot(ya_ref[...], wa_ref[...], preferred_element_type=F32)
    pb = jnp.dot(yb_ref[...], wb_ref[...], preferred_element_type=F32)
    for c in range(x_ref.shape[1] // COL_BLK):
        cols = slice(c * COL_BLK, (c + 1) * COL_BLK)
        gate_a = jax.nn.sigmoid(za_ref[:, cols].astype(F32) + bias_ref[0:1, cols])
        gate_b = jax.nn.sigmoid(zb_ref[:, cols].astype(F32) + bias_ref[1:2, cols])
        merged_ref[:, cols] = (gate_a * pa[:, cols] + gate_b * pb[:, cols]).astype(BF16)

    y = x_ref[...] + jnp.dot(merged_ref[...], wo_ref[...], preferred_element_type=F32)
    out_ref[...] = _rms(y, fg_ref[...]) if final_norm else y


def _merge(x2, ya, yb, proj2, bias, wa, wb, wo, final_gain, final_norm):
    T, D = x2.shape
    tm = MERGE_ROW_TILE
    def row_blk(width, col=0):
        return pl.BlockSpec((tm, width), lambda i: (i, col))

    def whole(a):
        return pl.BlockSpec(a.shape, lambda i: (0, 0))

    z_specs = [pl.BlockSpec((pl.Element(tm), pl.Element(D)), lambda i, off=off: (i * tm, off))
               for off in (_OFF_ZA, _OFF_ZA + D)]
    return pl.pallas_call(
        functools.partial(_merge_kernel, final_norm=final_norm),
        grid=(T // tm,),
        in_specs=[row_blk(D), row_blk(A_WIDTH), row_blk(B_WIDTH)] + z_specs
        + [whole(bias), whole(wa), whole(wb), whole(wo), whole(final_gain)],
        out_specs=row_blk(D),
        out_shape=jax.ShapeDtypeStruct((T, D), F32),
        scratch_shapes=[pltpu.VMEM((tm, D), BF16)],
        compiler_params=pltpu.CompilerParams(
            dimension_semantics=("parallel",),
            vmem_limit_bytes=V7X_VMEM_LIMIT_BYTES),
        name="merge_output",
    )(x2, ya, yb, proj2, proj2, bias, wa, wb, wo, final_gain)


def _angles(pos, dim, theta):
    expo = np.arange(0, dim, 2, dtype=np.float64) / dim
    return pos.astype(np.float64)[:, None] / np.power(float(theta), expo)[None, :]


def _lane_tables(ang_blocks):
    cos, sin = [], []
    for blk in ang_blocks:
        if isinstance(blk, int):
            S = cos[0].shape[0]
            cos.append(np.ones((S, blk)))
            sin.append(np.zeros((S, blk)))
            continue
        c, s = np.cos(blk), np.sin(blk)
        cos += [c, c]
        sin += [-s, s]
    return tuple(jnp.asarray(np.concatenate(t, axis=-1), dtype=F32) for t in (cos, sin))


def _partner_permutation(rotary_dim):
    half = rotary_dim // 2
    p = np.zeros((HEAD_DIM, HEAD_DIM))
    for i in range(rotary_dim):
        p[i + half if i < half else i - half, i] = 1.0
    return jnp.asarray(p, dtype=BF16)


def _axial_tables(S):
    pos = np.arange(S)
    half = HEAD_DIM // 2
    return _lane_tables([_angles(pos // GRID_W, half, AXIAL_THETA),
                         _angles(pos % GRID_W, half, AXIAL_THETA)])


def _partial_tables(S):
    pos = np.arange(S)
    return _lane_tables([_angles(pos, PARTIAL_ROPE_DIM, ROPE_THETA),
                         HEAD_DIM - PARTIAL_ROPE_DIM])


def kernel(x, norm_gain, w_in, q_norm_gain, k_norm_gain, merge_gate_bias, w_branch_a,
           w_branch_b, w_out, final_norm_gain):
    B, S, D = x.shape
    depth = norm_gain.shape[0]
    n_cols = _OFF_ZA + 2 * D
    assert w_in.shape[2] == n_cols and S % GRID_W == 0
    assert all(w // (2 * d) == BAND_HALF for w, d in B_PATTERNS)
    axial = _axial_tables(S)
    partial = _partial_tables(S)
    x2 = x.reshape(B * S, D)
    for l in range(depth):
        proj2 = _input_projection(x2, norm_gain[l][None, :], w_in[l])
        proj3 = proj2.reshape(B, S, n_cols)
        ya = _attention_a(proj3, q_norm_gain[l][None, :], k_norm_gain[l][None, :], axial)
        yb = _band_mix(proj3, partial)
        x2 = _merge(x2, ya.reshape(B * S, A_WIDTH), yb.reshape(B * S, B_WIDTH), proj2,
                    merge_gate_bias[l], w_branch_a[l].astype(BF16),
                    w_branch_b[l].astype(BF16), w_out[l].astype(BF16),
                    final_norm_gain[None, :], final_norm=(l == depth - 1))
    return x2.reshape(B, S, D)
```

```python
import functools
import math

import numpy as np
import jax
import jax.numpy as jnp
from jax import lax
from jax.experimental import pallas as pl
from jax.experimental.pallas import tpu as pltpu

F32 = jnp.float32
BF16 = jnp.bfloat16

HEAD_DIM = 128
GRID_W = 64
NORM_EPS = 1e-6
A_Q_HEADS = 8
A_KV_HEADS = 2
A_GROUP = A_Q_HEADS // A_KV_HEADS
A_WIDTH = A_Q_HEADS * HEAD_DIM
A_KV_WIDTH = A_KV_HEADS * HEAD_DIM
AXIAL_THETA = 10000.0
AXIAL_SHIFT = HEAD_DIM // 4
MAX_SAFE_SCORE_BOUND = 50.0
SCORE_BOUND_SLACK = 1.01
B_PATTERNS = ((128, 1), (512, 4), (2048, 16))
B_GROUPS = len(B_PATTERNS)
B_HEADS = 4
B_WIDTH = B_HEADS * HEAD_DIM
PARTIAL_ROPE_DIM = HEAD_DIM // 4
PARTIAL_SHIFT = PARTIAL_ROPE_DIM // 2
ROPE_THETA = 500000.0
BAND_HALF = 64
BAND_TQ = 128
BAND_WIN = BAND_TQ + 2 * BAND_HALF
MIX_ROWS = 256
SPLIT = 4
MASK_VALUE = -1e30
LOG2_E = math.log2(math.e)
Q_SCALE = HEAD_DIM ** -0.5 * LOG2_E

COL_BLK = 512
_OFF_QA = 0
_OFF_KA = A_WIDTH
_OFF_VA = _OFF_KA + A_KV_WIDTH
_OFF_GA = _OFF_VA + A_KV_WIDTH
_OFF_QB = _OFF_GA + A_WIDTH
_OFF_KB = _OFF_QB + B_GROUPS * B_WIDTH
_OFF_VB = _OFF_KB + B_GROUPS * B_WIDTH
_OFF_GB = _OFF_VB + B_GROUPS * B_WIDTH
_OFF_ZA = _OFF_GB + B_WIDTH

V7X_VMEM_LIMIT_BYTES = 56 * 1024 * 1024
PROJ_ROW_TILE = 512
PROJ_COL_TILE = 2 * COL_BLK
PROJ_ROW_TILES_PER_WEIGHT_TILE = 4
ATTN_Q_TILE = 512
MERGE_ROW_TILE = 512


def _rotary(x, cos, sin, shift):
    lane = lax.broadcasted_iota(jnp.int32, x.shape, 1)
    up = pltpu.roll(x, HEAD_DIM - shift, 1)
    dn = pltpu.roll(x, shift, 1)
    partner = jnp.where(lane % (2 * shift) < shift, up, dn)
    return x * cos + partner * sin


def _rms(x, gain):
    ms = jnp.mean(x * x, axis=-1, keepdims=True)
    return x * lax.rsqrt(ms + NORM_EPS) * gain


def _silu(g):
    return g * jax.nn.sigmoid(g)


def _proj_kernel(x_ref, g_ref, w_ref, o_ref, h_ref, *, n_full, tail, tm, n_rows):
    s = pl.program_id(1)
    tn = w_ref.shape[1]
    assert n_full >= 1

    for r in range(n_rows):
        @pl.when(s == r)
        def _():
            h = _rms(x_ref[...], g_ref[...]).astype(BF16)
            h_ref[r] = h
            o_ref[r * tm:(r + 1) * tm, :] = jnp.dot(
                h, w_ref[...].astype(BF16), preferred_element_type=F32).astype(BF16)

    def all_row_tiles(width):
        w = w_ref[:, 0:width].astype(BF16)
        for r in range(n_rows):
            o_ref[r * tm:(r + 1) * tm, 0:width] = jnp.dot(
                h_ref[r], w, preferred_element_type=F32).astype(BF16)

    col = s - (n_rows - 1)

    @pl.when((col >= 1) & (col < n_full))
    def _():
        all_row_tiles(tn)

    if tail:
        @pl.when(col == n_full)
        def _():
            all_row_tiles(tail)


def _input_projection(x2, gain, w):
    T, D = x2.shape
    N = w.shape[1]
    tm, tn, R = PROJ_ROW_TILE, PROJ_COL_TILE, PROJ_ROW_TILES_PER_WEIGHT_TILE
    n_full, tail = divmod(N, tn)
    n_col = pl.cdiv(N, tn)

    def col_tile(s):
        return jnp.maximum(s - (R - 1), 0)

    return pl.pallas_call(
        functools.partial(_proj_kernel, n_full=n_full, tail=tail, tm=tm, n_rows=R),
        grid=(T // (tm * R), R - 1 + n_col),
        in_specs=[
            pl.BlockSpec((tm, D), lambda g, s: (g * R + jnp.minimum(s, R - 1), 0)),
            pl.BlockSpec((1, D), lambda g, s: (0, 0)),
            pl.BlockSpec((D, tn), lambda g, s: (0, col_tile(s))),
        ],
        out_specs=pl.BlockSpec((R * tm, tn), lambda g, s: (g, col_tile(s))),
        out_shape=jax.ShapeDtypeStruct((T, N), BF16),
        scratch_shapes=[pltpu.VMEM((R, tm, D), BF16)],
        compiler_params=pltpu.CompilerParams(
            dimension_semantics=("parallel", "arbitrary"),
            vmem_limit_bytes=V7X_VMEM_LIMIT_BYTES),
        name="input_projection",
    )(x2, gain, w)


def _attn_a_kernel(q_ref, qnext_ref, k_ref, v_ref, *refs, tq, n_q):
    ga_refs = refs[:A_KV_HEADS]
    (gq_ref, gk_ref, cos_ref, sin_ref, o_ref,
     kn_ref, v1_ref, k2max_ref, qs_ref, q2max_ref) = refs[A_KV_HEADS:]
    qi = pl.program_id(1)
    slot = qi % 2

    def max_sq_norm(x_bf16):
        xf = x_bf16.astype(F32)
        return jnp.max(jnp.sum(xf * xf, axis=-1, keepdims=True), axis=0, keepdims=True)

    def prepare_queries(src_ref, tile, dst):
        rows = pl.ds(pl.multiple_of(tile * tq, tq), tq)
        cos, sin = cos_ref[rows, :], sin_ref[rows, :]
        q2max = None
        for h in range(A_Q_HEADS):
            cols = slice(h * HEAD_DIM, (h + 1) * HEAD_DIM)
            q = _rms(src_ref[0, :, cols].astype(F32), gq_ref[...])
            q = (_rotary(q, cos, sin, AXIAL_SHIFT) * Q_SCALE).astype(BF16)
            qs_ref[dst, :, cols] = q
            q2max = max_sq_norm(q) if q2max is None else jnp.maximum(q2max, max_sq_norm(q))
        q2max_ref[dst] = jnp.broadcast_to(q2max, q2max_ref.shape[1:])

    @pl.when(qi == 0)
    def _():
        k2max = None
        for g in range(A_KV_HEADS):
            cols = slice(g * HEAD_DIM, (g + 1) * HEAD_DIM)
            k = _rms(k_ref[0, :, cols].astype(F32), gk_ref[...])
            k = _rotary(k, cos_ref[...], sin_ref[...], AXIAL_SHIFT).astype(BF16)
            kn_ref[g] = k
            k2max = max_sq_norm(k) if k2max is None else jnp.maximum(k2max, max_sq_norm(k))
            v1_ref[g, :, 0:HEAD_DIM] = v_ref[0, :, cols]
            v1_ref[g, :, HEAD_DIM:] = jnp.ones((v1_ref.shape[1], HEAD_DIM), BF16)
        k2max_ref[...] = jnp.broadcast_to(k2max, k2max_ref.shape)

    @pl.when((pl.program_id(0) == 0) & (qi == 0))
    def _():
        prepare_queries(q_ref, 0, 0)

    bound = jnp.sqrt(q2max_ref[slot, 0:1, 0:1] * k2max_ref[0:1, 0:1]) * SCORE_BOUND_SLACK
    bound_is_safe = bound[0, 0] <= MAX_SAFE_SCORE_BOUND

    def attend(shift_of):
        prepare_queries(qnext_ref, (qi + 1) % n_q, 1 - slot)
        for h in range(A_Q_HEADS):
            g, c = divmod(h, A_GROUP)
            cols = slice(h * HEAD_DIM, (h + 1) * HEAD_DIM)
            s = lax.dot_general(qs_ref[slot, :, cols], kn_ref[g], (((1,), (1,)), ((), ())),
                                preferred_element_type=F32)
            p = jnp.exp2(s - shift_of(s))
            pv = jnp.dot(p.astype(BF16), v1_ref[g], preferred_element_type=F32)
            o, l = pv[:, :HEAD_DIM], pv[:, HEAD_DIM:]
            gate = ga_refs[g][0, :, c * HEAD_DIM:(c + 1) * HEAD_DIM].astype(F32)
            o_ref[0, :, cols] = (o * (1.0 / l) * _silu(gate)).astype(BF16)

    @pl.when(bound_is_safe)
    def _():
        attend(lambda s: bound)

    @pl.when(jnp.logical_not(bound_is_safe))
    def _():
        attend(lambda s: jnp.max(s, axis=-1, keepdims=True))


def _attention_a(proj3, q_gain, k_gain, tables):
    B, S, _ = proj3.shape
    tq = ATTN_Q_TILE
    n_q = S // tq
    gw = A_GROUP * HEAD_DIM
    full = pl.BlockSpec((S, HEAD_DIM), lambda b, i: (0, 0))
    vec = pl.BlockSpec((1, HEAD_DIM), lambda b, i: (0, 0))
    assert n_q % 2 == 0
    assert _OFF_QA == 0 and _OFF_KA % A_KV_WIDTH == 0 and _OFF_VA % A_KV_WIDTH == 0

    def next_step_q(b, i):
        n = jnp.minimum(b * n_q + i + 1, B * n_q - 1)
        return n // n_q, n % n_q, 0

    return pl.pallas_call(
        functools.partial(_attn_a_kernel, tq=tq, n_q=n_q),
        grid=(B, n_q),
        in_specs=[
            pl.BlockSpec((1, tq, A_WIDTH), lambda b, i: (b, i, 0)),
            pl.BlockSpec((1, tq, A_WIDTH), next_step_q),
            pl.BlockSpec((1, S, A_KV_WIDTH), lambda b, i: (b, 0, _OFF_KA // A_KV_WIDTH)),
            pl.BlockSpec((1, S, A_KV_WIDTH), lambda b, i: (b, 0, _OFF_VA // A_KV_WIDTH)),
        ] + [pl.BlockSpec((1, tq, gw), lambda b, i, g=g: (b, i, _OFF_GA // gw + g))
             for g in range(A_KV_HEADS)]
        + [vec, vec, full, full],
        out_specs=pl.BlockSpec((1, tq, A_WIDTH), lambda b, i: (b, i, 0)),
        out_shape=jax.ShapeDtypeStruct((B, S, A_WIDTH), BF16),
        scratch_shapes=[
            pltpu.VMEM((A_KV_HEADS, S, HEAD_DIM), BF16),
            pltpu.VMEM((A_KV_HEADS, S, 2 * HEAD_DIM), BF16),
            pltpu.VMEM((8, HEAD_DIM), F32),
            pltpu.VMEM((2, tq, A_WIDTH), BF16),
            pltpu.VMEM((2, 8, HEAD_DIM), F32),
        ],
        compiler_params=pltpu.CompilerParams(
            dimension_semantics=("arbitrary", "arbitrary"),
            vmem_limit_bytes=V7X_VMEM_LIMIT_BYTES),
        name="attention_a",
    )(proj3, proj3, proj3, proj3, *([proj3] * A_KV_HEADS), q_gain, k_gain, *tables)


def _band_tile(q, kw, vw, valid):
    s = lax.dot_general(q, kw, (((1,), (1,)), ((), ())), preferred_element_type=F32)
    s = jnp.where(valid, s, MASK_VALUE)
    m = jnp.max(s, axis=-1, keepdims=True)
    p = jnp.exp2(s - m)
    pv = jnp.dot(p.astype(BF16), vw, preferred_element_type=F32)
    o, l = pv[:, :HEAD_DIM], pv[:, HEAD_DIM:]
    return o * (1.0 / l), m + jnp.log2(l)


def _window_mask(t, n_tiles):
    qi = lax.broadcasted_iota(jnp.int32, (BAND_TQ, BAND_WIN), 0)
    kj = lax.broadcasted_iota(jnp.int32, (BAND_TQ, BAND_WIN), 1)
    kpos = kj + (t * BAND_TQ - BAND_HALF)
    return (kj >= qi) & (kj <= qi + 2 * BAND_HALF) & (kpos >= 0) & (kpos < n_tiles * BAND_TQ)


def _zero_pads(ref, lead, n_rows):
    zeros = jnp.zeros((BAND_HALF, HEAD_DIM), BF16)
    lanes = slice(0, HEAD_DIM)
    for idx in lead:
        ref[idx + (slice(0, BAND_HALF), lanes)] = zeros
        ref[idx + (slice(BAND_HALF + n_rows, 2 * BAND_HALF + n_rows), lanes)] = zeros


def _band_mix_kernel(q1_ref, k1_ref, v1_ref, q4_ref, k4_ref, v4_ref, q16_ref, k16_ref, v16_ref,
                     gb_ref, cos_ref, sin_ref, perm_ref, y_ref,
                     nat_ref, tmp_ref, qd1_ref, kp1_ref, vp1_ref, qd4_ref, kp4_ref, vp4_ref,
                     qd16_ref, kd16_ref, vd16_ref, od16_ref, og_ref, lg_ref, *, S):
    L4 = S // SPLIT
    L16 = L4 // SPLIT
    n1, n4 = S // BAND_TQ, L4 // BAND_TQ
    per_tile = BAND_TQ // L16

    def rotated(src_ref, scale=None):
        x = src_ref[0]
        partner = jnp.dot(x, perm_ref[...], preferred_element_type=F32)
        y = x.astype(F32) * cos_ref[...] + partner * sin_ref[...]
        return y if scale is None else y * scale

    lanes = slice(0, HEAD_DIM)
    vp1_ref[:, HEAD_DIM:] = jnp.ones((vp1_ref.shape[0], HEAD_DIM), BF16)
    vp4_ref[:, :, HEAD_DIM:] = jnp.ones(vp4_ref.shape[:2] + (HEAD_DIM,), BF16)
    vd16_ref[:, :, HEAD_DIM:] = jnp.ones(vd16_ref.shape[:2] + (HEAD_DIM,), BF16)

    _zero_pads(kp1_ref, [()], S)
    _zero_pads(vp1_ref, [()], S)
    qd1_ref[...] = rotated(q1_ref, Q_SCALE).astype(BF16)
    kp1_ref[BAND_HALF:BAND_HALF + S, :] = rotated(k1_ref).astype(BF16)
    vp1_ref[BAND_HALF:BAND_HALF + S, lanes] = v1_ref[0]

    for t in range(n1):
        rows = slice(t * BAND_TQ, (t + 1) * BAND_TQ)
        win = slice(t * BAND_TQ, t * BAND_TQ + BAND_WIN)
        o, lse = _band_tile(qd1_ref[rows, :], kp1_ref[win, :], vp1_ref[win, :],
                            _window_mask(t, n1))
        og_ref[0, rows, :] = o
        lg_ref[0, rows, :] = lse

    _zero_pads(kp4_ref, [(r,) for r in range(SPLIT)], L4)
    _zero_pads(vp4_ref, [(r,) for r in range(SPLIT)], L4)
    for n, (val, dst_ref, lead) in enumerate(((rotated(q4_ref, Q_SCALE), qd4_ref, 0),
                                              (rotated(k4_ref), kp4_ref, BAND_HALF),
                                              (v4_ref[0].astype(F32), vp4_ref, BAND_HALF))):
        nat_ref[n] = val
        for r in range(SPLIT):
            dst_ref[r, lead:lead + L4, lanes] = \
                nat_ref[n, pl.ds(r, L4, stride=SPLIT), :].astype(BF16)

    for r in range(SPLIT):
        for t in range(n4):
            rows = pl.ds(r + t * (BAND_TQ * SPLIT), BAND_TQ, stride=SPLIT)
            win = slice(t * BAND_TQ, t * BAND_TQ + BAND_WIN)
            o, lse = _band_tile(qd4_ref[r, t * BAND_TQ:(t + 1) * BAND_TQ, :],
                                kp4_ref[r, win, :], vp4_ref[r, win, :], _window_mask(t, n4))
            og_ref[1, rows, :] = o
            lg_ref[1, rows, :] = lse

    for n, (val, dst_ref) in enumerate(((rotated(q16_ref, Q_SCALE), qd16_ref),
                                        (rotated(k16_ref), kd16_ref),
                                        (v16_ref[0].astype(F32), vd16_ref))):
        nat_ref[3 + n] = val
        for a in range(SPLIT):
            tmp_ref[n, a] = nat_ref[3 + n, pl.ds(a, L4, stride=SPLIT), :]
            for b in range(SPLIT):
                r = a + SPLIT * b
                dst_ref[r // per_tile, (r % per_tile) * L16:(r % per_tile + 1) * L16, lanes] = \
                    tmp_ref[n, a, pl.ds(b, L16, stride=SPLIT), :].astype(BF16)

    qi = lax.broadcasted_iota(jnp.int32, (BAND_TQ, BAND_TQ), 0)
    kj = lax.broadcasted_iota(jnp.int32, (BAND_TQ, BAND_TQ), 1)
    same_residue = functools.reduce(
        jnp.logical_or, [(qi >= c * L16) & (qi < (c + 1) * L16) & (kj >= c * L16)
                         & (kj < (c + 1) * L16) for c in range(per_tile)])
    stacked_band = same_residue & (kj >= qi - BAND_HALF) & (kj <= qi + BAND_HALF)
    for j in range(SPLIT * SPLIT // per_tile):
        o, lse = _band_tile(qd16_ref[j], kd16_ref[j], vd16_ref[j], stacked_band)
        od16_ref[0, j] = o
        od16_ref[1, j] = lse

    for n, dst_ref in enumerate((og_ref, lg_ref)):
        for a in range(SPLIT):
            for b in range(SPLIT):
                r = a + SPLIT * b
                tmp_ref[n, a, pl.ds(b, L16, stride=SPLIT), :] = \
                    od16_ref[n, r // per_tile, (r % per_tile) * L16:(r % per_tile + 1) * L16, :]
            dst_ref[2, pl.ds(a, L4, stride=SPLIT), :] = tmp_ref[n, a]

    chunk = MIX_ROWS

    def mix(i, carry):
        rows = pl.ds(pl.multiple_of(i * chunk, chunk), chunk)
        lses = [lg_ref[g, rows, :] for g in range(B_GROUPS)]
        lmax = functools.reduce(jnp.maximum, lses)
        es = [jnp.exp2(l - lmax) for l in lses]
        num = sum(e * og_ref[g, rows, :] for g, e in enumerate(es))
        ob = num * (1.0 / sum(es))
        y_ref[0, rows, :] = (ob * _silu(gb_ref[0, rows, :].astype(F32))).astype(BF16)
        return carry

    lax.fori_loop(0, S // chunk, mix, 0)


def _band_mix(proj3, tables):
    B, S, NC = proj3.shape
    assert tuple(d for _, d in B_PATTERNS) == (1, SPLIT, SPLIT * SPLIT)
    L4, L16 = S // SPLIT, S // (SPLIT * SPLIT)
    assert L4 % BAND_TQ == 0 and BAND_TQ % L16 == 0
    n16 = S // BAND_TQ

    def head_blk(off, g):
        first = off // HEAD_DIM + g * B_HEADS
        return pl.BlockSpec((1, S, HEAD_DIM), lambda b, h: (b, 0, first + h))

    qkv_specs = [head_blk(off, g) for g in range(B_GROUPS)
                 for off in (_OFF_QB, _OFF_KB, _OFF_VB)]
    gb_first = _OFF_GB // HEAD_DIM
    table = pl.BlockSpec((S, HEAD_DIM), lambda b, h: (0, 0))
    return pl.pallas_call(
        functools.partial(_band_mix_kernel, S=S),
        grid=(B, B_HEADS),
        in_specs=qkv_specs
        + [pl.BlockSpec((1, S, HEAD_DIM), lambda b, h: (b, 0, gb_first + h)), table, table,
           pl.BlockSpec((HEAD_DIM, HEAD_DIM), lambda b, h: (0, 0))],
        out_specs=pl.BlockSpec((1, S, HEAD_DIM), lambda b, h: (b, 0, h)),
        out_shape=jax.ShapeDtypeStruct((B, S, B_WIDTH), BF16),
        scratch_shapes=[
            pltpu.VMEM((6, S, HEAD_DIM), F32),
            pltpu.VMEM((3, SPLIT, L4, HEAD_DIM), F32),
            pltpu.VMEM((S, HEAD_DIM), BF16),
            pltpu.VMEM((S + 2 * BAND_HALF, HEAD_DIM), BF16),
            pltpu.VMEM((S + 2 * BAND_HALF, 2 * HEAD_DIM), BF16),
            pltpu.VMEM((SPLIT, L4, HEAD_DIM), BF16),
            pltpu.VMEM((SPLIT, L4 + 2 * BAND_HALF, HEAD_DIM), BF16),
            pltpu.VMEM((SPLIT, L4 + 2 * BAND_HALF, 2 * HEAD_DIM), BF16),
            pltpu.VMEM((n16, BAND_TQ, HEAD_DIM), BF16),
            pltpu.VMEM((n16, BAND_TQ, HEAD_DIM), BF16),
            pltpu.VMEM((n16, BAND_TQ, 2 * HEAD_DIM), BF16),
            pltpu.VMEM((2, n16, BAND_TQ, HEAD_DIM), F32),
            pltpu.VMEM((B_GROUPS, S, HEAD_DIM), F32),
            pltpu.VMEM((B_GROUPS, S, HEAD_DIM), F32),
        ],
        compiler_params=pltpu.CompilerParams(
            dimension_semantics=("parallel", "parallel"),
            vmem_limit_bytes=V7X_VMEM_LIMIT_BYTES),
        name="band_mix",
    )(*([proj3] * (3 * B_GROUPS + 1)), *tables, _partner_permutation(PARTIAL_ROPE_DIM))


def _merge_kernel(x_ref, ya_ref, yb_ref, za_ref, zb_ref, bias_ref,
                  wa_ref, wb_ref, wo_ref, fg_ref, out_ref, merged_ref, *, final_norm):
    pa = jnp.dot(ya_ref[...], wa_ref[...], preferred_element_type=F32)
    pb = jnp.dot(yb_ref[...], wb_ref[...], preferred_element_type=F32)
    for c in range(x_ref.shape[1] // COL_BLK):
        cols = slice(c * COL_BLK, (c + 1) * COL_BLK)
        gate_a = jax.nn.sigmoid(za_ref[:, cols].astype(F32) + bias_ref[0:1, cols])
        gate_b = jax.nn.sigmoid(zb_ref[:, cols].astype(F32) + bias_ref[1:2, cols])
        merged_ref[:, cols] = (gate_a * pa[:, cols] + gate_b * pb[:, cols]).astype(BF16)

    y = x_ref[...] + jnp.dot(merged_ref[...], wo_ref[...], preferred_element_type=F32)
    out_ref[...] = _rms(y, fg_ref[...]) if final_norm else y


def _merge(x2, ya, yb, proj2, bias, wa, wb, wo, final_gain, final_norm):
    T, D = x2.shape
    tm = MERGE_ROW_TILE
    def row_blk(width, col=0):
        return pl.BlockSpec((tm, width), lambda i: (i, col))

    def whole(a):
        return pl.BlockSpec(a.shape, lambda i: (0, 0))

    z_specs = [pl.BlockSpec((pl.Element(tm), pl.Element(D)), lambda i, off=off: (i * tm, off))
               for off in (_OFF_ZA, _OFF_ZA + D)]
    return pl.pallas_call(
        functools.partial(_merge_kernel, final_norm=final_norm),
        grid=(T // tm,),
        in_specs=[row_blk(D), row_blk(A_WIDTH), row_blk(B_WIDTH)] + z_specs
        + [whole(bias), whole(wa), whole(wb), whole(wo), whole(final_gain)],
        out_specs=row_blk(D),
        out_shape=jax.ShapeDtypeStruct((T, D), F32),
        scratch_shapes=[pltpu.VMEM((tm, D), BF16)],
        compiler_params=pltpu.CompilerParams(
            dimension_semantics=("parallel",),
            vmem_limit_bytes=V7X_VMEM_LIMIT_BYTES),
        name="merge_output",
    )(x2, ya, yb, proj2, proj2, bias, wa, wb, wo, final_gain)


def _angles(pos, dim, theta):
    expo = np.arange(0, dim, 2, dtype=np.float64) / dim
    return pos.astype(np.float64)[:, None] / np.power(float(theta), expo)[None, :]


def _lane_tables(ang_blocks):
    cos, sin = [], []
    for blk in ang_blocks:
        if isinstance(blk, int):
            S = cos[0].shape[0]
            cos.append(np.ones((S, blk)))
            sin.append(np.zeros((S, blk)))
            continue
        c, s = np.cos(blk), np.sin(blk)
        cos += [c, c]
        sin += [-s, s]
    return tuple(jnp.asarray(np.concatenate(t, axis=-1), dtype=F32) for t in (cos, sin))


def _partner_permutation(rotary_dim):
    half = rotary_dim // 2
    p = np.zeros((HEAD_DIM, HEAD_DIM))
    for i in range(rotary_dim):
        p[i + half if i < half else i - half, i] = 1.0
    return jnp.asarray(p, dtype=BF16)


def _axial_tables(S):
    pos = np.arange(S)
    half = HEAD_DIM // 2
    return _lane_tables([_angles(pos // GRID_W, half, AXIAL_THETA),
                         _angles(pos % GRID_W, half, AXIAL_THETA)])


def _partial_tables(S):
    pos = np.arange(S)
    return _lane_tables([_angles(pos, PARTIAL_ROPE_DIM, ROPE_THETA),
                         HEAD_DIM - PARTIAL_ROPE_DIM])


def kernel(x, norm_gain, w_in, q_norm_gain, k_norm_gain, merge_gate_bias, w_branch_a,
           w_branch_b, w_out, final_norm_gain):
    B, S, D = x.shape
    depth = norm_gain.shape[0]
    n_cols = _OFF_ZA + 2 * D
    assert w_in.shape[2] == n_cols and S % GRID_W == 0
    assert all(w // (2 * d) == BAND_HALF for w, d in B_PATTERNS)
    axial = _axial_tables(S)
    partial = _partial_tables(S)
    x2 = x.reshape(B * S, D)
    for l in range(depth):
        proj2 = _input_projection(x2, norm_gain[l][None, :], w_in[l])
        proj3 = proj2.reshape(B, S, n_cols)
        ya = _attention_a(proj3, q_norm_gain[l][None, :], k_norm_gain[l][None, :], axial)
        yb = _band_mix(proj3, partial)
        x2 = _merge(x2, ya.reshape(B * S, A_WIDTH), yb.reshape(B * S, B_WIDTH), proj2,
                    merge_gate_bias[l], w_branch_a[l].astype(BF16),
                    w_branch_b[l].astype(BF16), w_out[l].astype(BF16),
                    final_norm_gain[None, :], final_norm=(l == depth - 1))
    return x2.reshape(B, S, D)
```
